```python
import jax
import jax.numpy as jnp
from jax import lax
import numpy as np

D_MODEL = 2048
BATCH = 4
SEQ = 2048
DEPTH = 1

CTX_LEN = 256
GRID_W = 64
N_MOD = 6
FOUR_GROUPS = 8
FOUR_GROUP_DIM = D_MODEL // 16
FOUR_WIDTH = FOUR_GROUPS * FOUR_GROUP_DIM
RET_HEADS = 8
RET_DK = D_MODEL // 16
RET_DV = D_MODEL // 8
RET_QK_WIDTH = RET_HEADS * RET_DK
RET_V_WIDTH = RET_HEADS * RET_DV
RET_CHUNK = 128
ROPE_BASE = 10000.0
N_GROUPS = 4
EXPERTS_PER_GROUP = 8
N_EXPERTS = N_GROUPS * EXPERTS_PER_GROUP
TOP_K_IN_GROUP = 2
EXPERT_FF = D_MODEL // 4
EPS = 1e-6
Q_OFF = FOUR_WIDTH
K_OFF = Q_OFF + RET_QK_WIDTH
V_OFF = K_OFF + RET_QK_WIDTH
GF_OFF = V_OFF + RET_V_WIDTH
GB_OFF = GF_OFF + RET_V_WIDTH
MF_OFF = GB_OFF + RET_V_WIDTH
MR_OFF = MF_OFF + D_MODEL
IN_WIDTH = MR_OFF + D_MODEL
SPLITS = (Q_OFF, K_OFF, V_OFF, GF_OFF, GB_OFF, MF_OFF, MR_OFF)

kernel_name = 'hybrid_fnet_retnet_hmoe_dit'


def rmsnorm(x, g):
    xf = x.astype(jnp.float32)
    y = xf * lax.rsqrt(jnp.mean(xf * xf, axis=-1, keepdims=True) + EPS)
    return y.astype(x.dtype) * g


def modulate(h, shift, scale):
    return h * (1.0 + scale) + shift


def fourier_mix(u):
    b_, n = u.shape[0], u.shape[1]
    ug = u.reshape(b_, n, FOUR_GROUPS, FOUR_GROUP_DIM).astype(jnp.float32)
    f = jnp.fft.fftn(ug, axes=(1, 3), norm='ortho').real
    return f.reshape(b_, n, FOUR_WIDTH).astype(u.dtype)


def grid_rope_tables(n_tokens):
    rows = n_tokens // GRID_W
    row = jnp.repeat(jnp.arange(rows, dtype=jnp.float32), GRID_W)
    col = jnp.tile(jnp.arange(GRID_W, dtype=jnp.float32), rows)
    n_freq = RET_DK // 4
    inv = ROPE_BASE ** (-jnp.arange(n_freq, dtype=jnp.float32) / n_freq)
    ang = jnp.concatenate([row[:, None] * inv, col[:, None] * inv], axis=-1)
    return jnp.cos(ang), jnp.sin(ang)


def apply_rope(t, cos, sin):
    half = RET_DK // 2
    t1, t2 = t[..., :half], t[..., half:]
    return jnp.concatenate([t1 * cos - t2 * sin, t1 * sin + t2 * cos], axis=-1)


def to_heads(t, d):
    b_, n = t.shape[0], t.shape[1]
    return t.reshape(b_, n, RET_HEADS, d).transpose(0, 2, 1, 3).astype(jnp.float32)


def head_norm(y):
    mu = jnp.mean(y, axis=-1, keepdims=True)
    var = jnp.mean(jnp.square(y - mu), axis=-1, keepdims=True)
    y = (y - mu) * lax.rsqrt(var + EPS)
    b_, h, n, dv = y.shape
    return y.transpose(0, 2, 1, 3).reshape(b_, n, h * dv)


def retention_chunkwise(q, k, v, log_gamma, r0):
    b_, h, n, _ = q.shape
    nc = n // RET_CHUNK
    pos = jnp.arange(RET_CHUNK, dtype=jnp.float32)
    diff = pos[:, None] - pos[None, :]
    lg = log_gamma[:, None, None]
    decay_in = jnp.where(diff >= 0, jnp.exp(jnp.maximum(diff, 0.0) * lg), 0.0)
    xi = jnp.exp((pos + 1.0) * log_gamma[:, None])[..., None]
    zeta = jnp.exp((RET_CHUNK - 1.0 - pos) * log_gamma[:, None])[..., None]
    g_chunk = jnp.exp(RET_CHUNK * log_gamma)[:, None, None]

    def to_chunks(t):
        return jnp.moveaxis(t.reshape(b_, h, nc, RET_CHUNK, t.shape[-1]), 2, 0)

    def step(r, qkv):
        qc, kc, vc = qkv
        s = jnp.einsum('bhnd,bhmd->bhnm', qc, kc) * decay_in
        o = jnp.einsum('bhnm,bhmv->bhnv', s, vc) + jnp.einsum('bhnd,bhdv->bhnv', qc, r) * xi
        r = g_chunk * r + jnp.einsum('bhmd,bhmv->bhdv', kc * zeta, vc)
        return r, o

    _, o = lax.scan(step, r0, (to_chunks(q), to_chunks(k), to_chunks(v)))
    return jnp.moveaxis(o, 0, 2).reshape(b_, h, n, -1)


def final_state(k, v, log_gamma, reverse):
    n = k.shape[2]
    m = jnp.arange(n, dtype=jnp.float32)
    expo = m if reverse else (n - 1.0 - m)
    w = jnp.exp(expo[None, :] * log_gamma[:, None])
    return jnp.einsum('bhmd,hm,bhmv->bhdv', k, w, v)


def context_states(hc, w_in_l, log_g_f, log_g_b):
    k = to_heads(hc @ w_in_l[:, K_OFF:V_OFF], RET_DK)
    v = to_heads(hc @ w_in_l[:, V_OFF:GF_OFF], RET_DV)
    return final_state(k, v, log_g_f, False), final_state(k, v, log_g_b, True)


def token_mixer(h, rope, r0_f, r0_b, w_in_l, w_four_out_l, w_ret_out_l, w_out_l, log_g_f, log_g_b):
    dt = h.dtype
    p = h @ w_in_l
    u_four, q, k, v, g_f, g_b, m_four, m_ret = jnp.split(p, SPLITS, axis=-1)
    four = fourier_mix(u_four) @ w_four_out_l
    q = to_heads(q, RET_DK) * (RET_DK ** -0.5)
    k = to_heads(k, RET_DK)
    v = to_heads(v, RET_DV)
    if rope is not None:
        q = apply_rope(q, rope[0], rope[1])
        k = apply_rope(k, rope[0], rope[1])
    y_f = retention_chunkwise(q, k, v, log_g_f, r0_f)
    y_b = jnp.flip(retention_chunkwise(jnp.flip(q, 2), jnp.flip(k, 2), jnp.flip(v, 2), log_g_b, r0_b), 2)
    ret = jax.nn.silu(g_f) * head_norm(y_f).astype(dt) + jax.nn.silu(g_b) * head_norm(y_b).astype(dt)
    ret = ret @ w_ret_out_l
    merged = jax.nn.sigmoid(m_four) * four + jax.nn.sigmoid(m_ret) * ret
    return (merged @ w_out_l).astype(dt)


def hier_moe(h, w_group_router_l, b_group_router_l, w_expert_router_l, b_expert_router_l, w_gate_l, w_up_l, w_down_l):
    t = h.shape[0]
    hf = h.astype(jnp.float32)
    g_logits = hf @ w_group_router_l.astype(jnp.float32) + b_group_router_l.astype(jnp.float32)
    g_prob = jax.nn.softmax(g_logits, axis=-1)
    grp = jnp.argmax(g_logits, axis=-1)
    g_w = jnp.take_along_axis(g_prob, grp[:, None], axis=-1)
    e_all = jnp.einsum('td,gde->tge', hf, w_expert_router_l.astype(jnp.float32)) + b_expert_router_l.astype(jnp.float32)
    e_logits = jnp.take_along_axis(e_all, grp[:, None, None], axis=1)[:, 0]
    top_v, top_i = lax.top_k(e_logits, TOP_K_IN_GROUP)
    top_w = jax.nn.softmax(top_v, axis=-1) * g_w
    expert_id = grp[:, None] * EXPERTS_PER_GROUP + top_i
    combine = jnp.sum(jax.nn.one_hot(expert_id, N_EXPERTS, dtype=jnp.float32) * top_w[..., None], axis=1)

    def expert_step(acc, xs):
        wg, wu, wd, cw = xs
        hid = jax.nn.silu(h @ wg) * (h @ wu)
        return acc + (hid * cw[:, None].astype(h.dtype)) @ wd, None

    out, _ = lax.scan(expert_step, jnp.zeros((t, h.shape[1]), h.dtype), (w_gate_l, w_up_l, w_down_l, combine.T))
    return out


def setup_inputs(seed: int = 0) -> dict:
    key = jax.random.key(seed)
    ks = jax.random.split(key, 24)

    def nrm(k, shape, scale):
        return jax.random.normal(k, shape, jnp.float32) * scale

    gam = 1.0 - 2.0 ** (-5.0 - np.arange(RET_HEADS, dtype=np.float32))
    decay_logit = jnp.asarray(np.log(gam) - np.log1p(-gam), jnp.float32)
    return {
        'x': nrm(ks[0], (BATCH, SEQ, D_MODEL), 1.0),
        'c': nrm(ks[1], (BATCH, D_MODEL), 1.0),
        'ctx': nrm(ks[2], (BATCH, CTX_LEN, D_MODEL), 1.0),
        'c_ctx': nrm(ks[3], (D_MODEL,), 1.0),
        'w_mod': nrm(ks[4], (DEPTH, D_MODEL, N_MOD * D_MODEL), 0.5 * D_MODEL ** -0.5),
        'b_mod': nrm(ks[5], (DEPTH, N_MOD * D_MODEL), 0.02),
        'norm1_g': 1.0 + nrm(ks[6], (DEPTH, D_MODEL), 0.02),
        'norm2_g': 1.0 + nrm(ks[7], (DEPTH, D_MODEL), 0.02),
        'w_in': nrm(ks[8], (DEPTH, D_MODEL, IN_WIDTH), D_MODEL ** -0.5),
        'w_four_out': nrm(ks[9], (DEPTH, FOUR_WIDTH, D_MODEL), FOUR_WIDTH ** -0.5),
        'w_ret_out': nrm(ks[10], (DEPTH, RET_V_WIDTH, D_MODEL), RET_V_WIDTH ** -0.5),
        'w_out': nrm(ks[11], (DEPTH, D_MODEL, D_MODEL), D_MODEL ** -0.5),
        'ret_decay_f': decay_logit + nrm(ks[12], (DEPTH, RET_HEADS), 0.1),
        'ret_decay_b': decay_logit + nrm(ks[13], (DEPTH, RET_HEADS), 0.1),
        'w_group_router': nrm(ks[14], (DEPTH, D_MODEL, N_GROUPS), D_MODEL ** -0.5),
        'b_group_router': nrm(ks[15], (DEPTH, N_GROUPS), 0.01),
        'w_expert_router': nrm(ks[16], (DEPTH, N_GROUPS, D_MODEL, EXPERTS_PER_GROUP), D_MODEL ** -0.5),
        'b_expert_router': nrm(ks[17], (DEPTH, N_GROUPS, EXPERTS_PER_GROUP), 0.01),
        'w_gate': nrm(ks[18], (DEPTH, N_EXPERTS, D_MODEL, EXPERT_FF), D_MODEL ** -0.5),
        'w_up': nrm(ks[19], (DEPTH, N_EXPERTS, D_MODEL, EXPERT_FF), D_MODEL ** -0.5),
        'w_down': nrm(ks[20], (DEPTH, N_EXPERTS, EXPERT_FF, D_MODEL), EXPERT_FF ** -0.5),
        'final_norm_g': 1.0 + nrm(ks[21], (D_MODEL,), 0.02),
    }


def reference(x, c, ctx, c_ctx, w_mod, b_mod, norm1_g, norm2_g, w_in, w_four_out, w_ret_out, w_out,
              ret_decay_f, ret_decay_b, w_group_router, b_group_router, w_expert_router, b_expert_router,
              w_gate, w_up, w_down, final_norm_g):
    b_, n_lat, d = x.shape
    rope = grid_rope_tables(n_lat)
    sc = jax.nn.silu(c)
    scc = jax.nn.silu(c_ctx)
    for i in range(DEPTH):
        last = i == DEPTH - 1
        log_g_f = jax.nn.log_sigmoid(ret_decay_f[i].astype(jnp.float32))
        log_g_b = jax.nn.log_sigmoid(ret_decay_b[i].astype(jnp.float32))
        mod_x = (sc @ w_mod[i] + b_mod[i])[:, None, :]
        mod_c = scc @ w_mod[i] + b_mod[i]
        sh1, s1, g1, sh2, s2, g2 = jnp.split(mod_x, N_MOD, axis=-1)
        csh1, cs1, cg1, csh2, cs2, cg2 = jnp.split(mod_c, N_MOD, axis=-1)
        hx = modulate(rmsnorm(x, norm1_g[i]), sh1, s1)
        hc = modulate(rmsnorm(ctx, norm1_g[i]), csh1, cs1)
        r_f, r_b = context_states(hc, w_in[i], log_g_f, log_g_b)
        x = x + g1 * token_mixer(hx, rope, r_f, r_b, w_in[i], w_four_out[i], w_ret_out[i], w_out[i], log_g_f, log_g_b)
        if not last:
            zero_state = jnp.zeros_like(r_f)
            ctx = ctx + cg1 * token_mixer(hc, None, zero_state, zero_state, w_in[i], w_four_out[i], w_ret_out[i], w_out[i], log_g_f, log_g_b)
        hx2 = modulate(rmsnorm(x, norm2_g[i]), sh2, s2)
        x = x + g2 * hier_moe(hx2.reshape(-1, d), w_group_router[i], b_group_router[i], w_expert_router[i],
                              b_expert_router[i], w_gate[i], w_up[i], w_down[i]).reshape(x.shape)
        if not last:
            hc2 = modulate(rmsnorm(ctx, norm2_g[i]), csh2, cs2)
            ctx = ctx + cg2 * hier_moe(hc2.reshape(-1, d), w_group_router[i], b_group_router[i], w_expert_router[i],
                                       b_expert_router[i], w_gate[i], w_up[i], w_down[i]).reshape(ctx.shape)
    return rmsnorm(x, final_norm_g)
```

```python
import functools

import jax
import jax.numpy as jnp
import numpy as np
from jax import lax
from jax.experimental import pallas as pl
from jax.experimental.pallas import tpu as pltpu

F32 = jnp.float32
BF16 = jnp.bfloat16

D_MODEL = 2048
SEQ = 2048
CTX_LEN = 256
GRID_W = 64
N_MOD = 6
FOUR_GROUPS = 8
FOUR_GROUP_DIM = 128
FOUR_WIDTH = FOUR_GROUPS * FOUR_GROUP_DIM
RET_HEADS = 8
RET_DK = 128
RET_DV = 256
ROPE_BASE = 10000.0
N_GROUPS = 4
EXPERTS_PER_GROUP = 8
N_EXPERTS = N_GROUPS * EXPERTS_PER_GROUP
EXPERT_FF = 512
EPS = 1e-6
Q_OFF = FOUR_WIDTH
K_OFF = Q_OFF + RET_HEADS * RET_DK
V_OFF = K_OFF + RET_HEADS * RET_DK
GF_OFF = V_OFF + RET_HEADS * RET_DV
GB_OFF = GF_OFF + RET_HEADS * RET_DV
MF_OFF = GB_OFF + RET_HEADS * RET_DV
MR_OFF = MF_OFF + D_MODEL
IN_WIDTH = MR_OFF + D_MODEL

V7X_VMEM_LIMIT_BYTES = 56 * 1024 * 1024
LANES = 128
RET_BLOCK = 256
COL_TILE = 1024
EXPERT_ROWS = 256
FINAL_ROWS = 256
N_EXPERT_TILES = 2 * 4 * SEQ // EXPERT_ROWS + N_EXPERTS


def _params(*sem):
    return pltpu.CompilerParams(dimension_semantics=sem, vmem_limit_bytes=V7X_VMEM_LIMIT_BYTES)


def _dot(a, b):
    return jnp.dot(a, b, preferred_element_type=F32)


def _mod_kernel(c_ref, w_ref, b_ref, o_ref):
    c = c_ref[...]
    sc = c * jax.nn.sigmoid(c)
    o_ref[...] = _dot(sc.astype(BF16), w_ref[...].astype(BF16)) + b_ref[...]


def _modulation(c8, w_mod, b_mod):
    n_out = w_mod.shape[1]
    return pl.pallas_call(
        _mod_kernel,
        grid=(n_out // COL_TILE,),
        in_specs=[
            pl.BlockSpec((8, D_MODEL), lambda j: (0, 0)),
            pl.BlockSpec((D_MODEL, COL_TILE), lambda j: (0, j)),
            pl.BlockSpec((1, COL_TILE), lambda j: (0, j)),
        ],
        out_specs=pl.BlockSpec((8, COL_TILE), lambda j: (0, j)),
        out_shape=jax.ShapeDtypeStruct((8, n_out), F32),
        compiler_params=_params("arbitrary"),
        name="mod",
    )(c8, w_mod, b_mod)


def _rms_modulate(x, g, shift, scale):
    y = x * lax.rsqrt(jnp.mean(x * x, axis=-1, keepdims=True) + EPS) * g
    return y * (1.0 + scale) + shift


def _prenorm_kernel(x_ref, g_ref, sh_ref, sc_ref, o_ref):
    o_ref[0] = _rms_modulate(x_ref[0], g_ref[...], sh_ref[0], sc_ref[0]).astype(o_ref.dtype)


def _prenorm(x, g, mod3, row_of_batch, shift_blk, scale_blk, rows):
    b, n, d = x.shape
    return pl.pallas_call(
        _prenorm_kernel,
        grid=(b, n // rows),
        in_specs=[
            pl.BlockSpec((1, rows, d), lambda i, j: (i, j, 0)),
            pl.BlockSpec((1, d), lambda i, j: (0, 0)),
            pl.BlockSpec((1, 1, d), lambda i, j: (row_of_batch(i), 0, shift_blk)),
            pl.BlockSpec((1, 1, d), lambda i, j: (row_of_batch(i), 0, scale_blk)),
        ],
        out_specs=pl.BlockSpec((1, rows, d), lambda i, j: (i, j, 0)),
        out_shape=jax.ShapeDtypeStruct((b, n, d), BF16),
        compiler_params=_params("arbitrary", "arbitrary"),
        name="prenorm",
    )(x, g, mod3, mod3)


def _proj_kernel(a_ref, w_ref, o_ref, wb_ref):
    @pl.when(pl.program_id(1) == 0)
    def _():
        wb_ref[...] = w_ref[...].astype(BF16)

    o_ref[...] = _dot(a_ref[...], wb_ref[...]).astype(o_ref.dtype)


def _project(a, w, col_tile0, n_col_tiles, rows, name):
    m, k = a.shape
    return pl.pallas_call(
        _proj_kernel,
        grid=(n_col_tiles, m // rows),
        in_specs=[
            pl.BlockSpec((rows, k), lambda j, i: (i, 0)),
            pl.BlockSpec((k, COL_TILE), lambda j, i: (0, j + col_tile0)),
        ],
        out_specs=pl.BlockSpec((rows, COL_TILE), lambda j, i: (i, j)),
        out_shape=jax.ShapeDtypeStruct((m, n_col_tiles * COL_TILE), BF16),
        scratch_shapes=[pltpu.VMEM((k, COL_TILE), BF16)],
        compiler_params=_params("arbitrary", "arbitrary"),
        name=name,
    )(a, w)


def _dft_tables():
    n = np.arange(SEQ, dtype=np.int64)
    ang_n = 2.0 * np.pi * ((n[:, None] * n[None, :]) % SEQ) / SEQ
    pos = np.concatenate([np.cos(ang_n), -np.sin(ang_n)], axis=1).astype(np.float32)
    c = np.arange(FOUR_GROUP_DIM, dtype=np.int64)
    ang_c = 2.0 * np.pi * ((c[:, None] * c[None, :]) % FOUR_GROUP_DIM) / FOUR_GROUP_DIM
    norm = 1.0 / np.sqrt(float(SEQ * FOUR_GROUP_DIM))
    chan = (np.concatenate([np.cos(ang_c), np.sin(ang_c)], axis=1) * norm).astype(np.float32)
    return pos, chan


def _fourier_kernel(u_ref, chan_ref, pos_ref, o_ref, y_ref):
    @pl.when(pl.program_id(1) == 0)
    def _():
        chan = chan_ref[...]
        for g in range(FOUR_GROUPS):
            cols = slice(g * FOUR_GROUP_DIM, (g + 1) * FOUR_GROUP_DIM)
            y = _dot(u_ref[0, :, cols], chan)
            y_ref[0:SEQ, cols] = y[:, :FOUR_GROUP_DIM].astype(BF16)
            y_ref[SEQ:2 * SEQ, cols] = y[:, FOUR_GROUP_DIM:].astype(BF16)

    o_ref[0] = _dot(pos_ref[...], y_ref[...]).astype(o_ref.dtype)


def _fourier(p3, rows=512):
    b = p3.shape[0]
    pos_np, chan_np = _dft_tables()
    pos = jnp.asarray(pos_np).astype(BF16)
    chan = jnp.asarray(chan_np).astype(BF16)
    return pl.pallas_call(
        _fourier_kernel,
        grid=(b, SEQ // rows),
        in_specs=[
            pl.BlockSpec((1, SEQ, FOUR_WIDTH), lambda i, j: (i, 0, 0)),
            pl.BlockSpec((FOUR_GROUP_DIM, 2 * FOUR_GROUP_DIM), lambda i, j: (0, 0)),
            pl.BlockSpec((rows, 2 * SEQ), lambda i, j: (j, 0)),
        ],
        out_specs=pl.BlockSpec((1, rows, FOUR_WIDTH), lambda i, j: (i, j, 0)),
        out_shape=jax.ShapeDtypeStruct((b, SEQ, FOUR_WIDTH), BF16),
        scratch_shapes=[pltpu.VMEM((2 * SEQ, FOUR_WIDTH), BF16)],
        compiler_params=_params("arbitrary", "arbitrary"),
        name="fourier",
    )(p3, chan, pos)


def _head_norm(o):
    mu = jnp.mean(o, axis=-1, keepdims=True)
    d = o - mu
    var = jnp.mean(d * d, axis=-1, keepdims=True)
    return d * lax.rsqrt(var + EPS)


def _silu(x):
    return x * jax.nn.sigmoid(x)


def _retention_kernel(lg_ref, q_ref, k_ref, v_ref, gf_ref, gb_ref, kc_ref, vc_ref, cos_ref, sin_ref,
                      o_ref, acc_ref, qs_ref, ks_ref, kzb_ref, rf_ref, rb_ref,
                      df_ref, db_ref, xif_ref, xib_ref, zf_ref, zb_ref):
    blk = RET_BLOCK
    n_blk = SEQ // blk
    h = pl.program_id(1)
    lgf, lgb, gchf, gchb = lg_ref[0, h], lg_ref[1, h], lg_ref[2, h], lg_ref[3, h]

    row = lax.broadcasted_iota(jnp.int32, (blk, blk), 0)
    col = lax.broadcasted_iota(jnp.int32, (blk, blk), 1)
    diff = (row - col).astype(F32)
    df_ref[...] = jnp.where(diff >= 0, jnp.exp(jnp.maximum(diff, 0.0) * lgf), 0.0)
    db_ref[...] = jnp.where(diff <= 0, jnp.exp(jnp.maximum(-diff, 0.0) * lgb), 0.0)
    pos_v = lax.broadcasted_iota(jnp.int32, (blk, RET_DV), 0).astype(F32)
    xif_ref[...] = jnp.exp((pos_v + 1.0) * lgf)
    xib_ref[...] = jnp.exp((blk - pos_v) * lgb)
    pos_k = lax.broadcasted_iota(jnp.int32, (blk, RET_DK), 0).astype(F32)
    zf_ref[...] = jnp.exp((blk - 1.0 - pos_k) * lgf)
    zb_ref[...] = jnp.exp(pos_k * lgb)

    def state_update(r_ref, kz, v, gch):
        upd = lax.dot_general(kz, v, (((0,), (0,)), ((), ())), preferred_element_type=F32)
        r_ref[...] = gch * r_ref[...] + upd

    kc = kc_ref[0].astype(F32)
    vc = vc_ref[0]
    rf_ref[...] = jnp.zeros_like(rf_ref)
    rb_ref[...] = jnp.zeros_like(rb_ref)
    state_update(rf_ref, (kc * zf_ref[...]).astype(BF16), vc, gchf)
    state_update(rb_ref, (kc * zb_ref[...]).astype(BF16), vc, gchb)

    def rope(t, rows):
        return t * cos_ref[rows, :] + pltpu.roll(t, RET_DK // 2, axis=1) * sin_ref[rows, :]

    def block_out(qb, kb, vb, r_ref, d_ref, xi_ref):
        s = lax.dot_general(qb, kb, (((1,), (1,)), ((), ())), preferred_element_type=F32)
        inner = _dot((s * d_ref[...]).astype(BF16), vb)
        cross = _dot(qb, r_ref[...].astype(BF16)) * xi_ref[...]
        return inner + cross

    def fwd(i, carry):
        rows = pl.ds(pl.multiple_of(i * blk, blk), blk)
        qr = rope(q_ref[0, rows, :].astype(F32), rows) * (RET_DK ** -0.5)
        kr = rope(k_ref[0, rows, :].astype(F32), rows)
        qb = qr.astype(BF16)
        kb = kr.astype(BF16)
        vb = v_ref[0, rows, :]
        qs_ref[rows, :] = qb
        ks_ref[rows, :] = kb
        kzb_ref[rows, :] = (kr * zb_ref[...]).astype(BF16)
        o = block_out(qb, kb, vb, rf_ref, df_ref, xif_ref)
        state_update(rf_ref, (kr * zf_ref[...]).astype(BF16), vb, gchf)
        acc_ref[rows, :] = _silu(gf_ref[0, rows, :].astype(F32)) * _head_norm(o)
        return carry

    lax.fori_loop(0, n_blk, fwd, 0)

    def bwd(j, carry):
        i = n_blk - 1 - j
        rows = pl.ds(pl.multiple_of(i * blk, blk), blk)
        qb = qs_ref[rows, :]
        kb = ks_ref[rows, :]
        vb = v_ref[0, rows, :]
        o = block_out(qb, kb, vb, rb_ref, db_ref, xib_ref)
        state_update(rb_ref, kzb_ref[rows, :], vb, gchb)
        y = acc_ref[rows, :] + _silu(gb_ref[0, rows, :].astype(F32)) * _head_norm(o)
        o_ref[0, rows, :] = y.astype(o_ref.dtype)
        return carry

    lax.fori_loop(0, n_blk, bwd, 0)


def _retention(p3, kvc3, lg, cos2, sin2):
    b = p3.shape[0]
    assert CTX_LEN == RET_BLOCK
    qb0, kb0 = Q_OFF // RET_DK, K_OFF // RET_DK
    vb0, gfb0, gbb0 = V_OFF // RET_DV, GF_OFF // RET_DV, GB_OFF // RET_DV
    kcv0 = (RET_HEADS * RET_DK) // RET_DV
    return pl.pallas_call(
        _retention_kernel,
        grid=(b, RET_HEADS),
        in_specs=[
            pl.BlockSpec(memory_space=pltpu.SMEM),
            pl.BlockSpec((1, SEQ, RET_DK), lambda i, h: (i, 0, qb0 + h)),
            pl.BlockSpec((1, SEQ, RET_DK), lambda i, h: (i, 0, kb0 + h)),
            pl.BlockSpec((1, SEQ, RET_DV), lambda i, h: (i, 0, vb0 + h)),
            pl.BlockSpec((1, SEQ, RET_DV), lambda i, h: (i, 0, gfb0 + h)),
            pl.BlockSpec((1, SEQ, RET_DV), lambda i, h: (i, 0, gbb0 + h)),
            pl.BlockSpec((1, CTX_LEN, RET_DK), lambda i, h: (i, 0, h)),
            pl.BlockSpec((1, CTX_LEN, RET_DV), lambda i, h: (i, 0, kcv0 + h)),
            pl.BlockSpec((SEQ, RET_DK), lambda i, h: (0, 0)),
            pl.BlockSpec((SEQ, RET_DK), lambda i, h: (0, 0)),
        ],
        out_specs=pl.BlockSpec((1, SEQ, RET_DV), lambda i, h: (i, 0, h)),
        out_shape=jax.ShapeDtypeStruct((b, SEQ, RET_HEADS * RET_DV), BF16),
        scratch_shapes=[
            pltpu.VMEM((SEQ, RET_DV), F32),
            pltpu.VMEM((SEQ, RET_DK), BF16),
            pltpu.VMEM((SEQ, RET_DK), BF16),
            pltpu.VMEM((SEQ, RET_DK), BF16),
            pltpu.VMEM((RET_DK, RET_DV), F32),
            pltpu.VMEM((RET_DK, RET_DV), F32),
            pltpu.VMEM((RET_BLOCK, RET_BLOCK), F32),
            pltpu.VMEM((RET_BLOCK, RET_BLOCK), F32),
            pltpu.VMEM((RET_BLOCK, RET_DV), F32),
            pltpu.VMEM((RET_BLOCK, RET_DV), F32),
            pltpu.VMEM((RET_BLOCK, RET_DK), F32),
            pltpu.VMEM((RET_BLOCK, RET_DK), F32),
        ],
        compiler_params=_params("arbitrary", "arbitrary"),
        name="retention",
    )(lg, p3, p3, p3, p3, p3, kvc3, kvc3, cos2, sin2)


def _merge_kernel(fm_ref, ret_ref, wf_ref, wr_ref, mf_ref, mr_ref, o_ref, wfb_ref, wrb_ref):
    @pl.when(pl.program_id(1) == 0)
    def _():
        wfb_ref[...] = wf_ref[...].astype(BF16)
        wrb_ref[...] = wr_ref[...].astype(BF16)

    four = _dot(fm_ref[...], wfb_ref[...])
    ret = _dot(ret_ref[...], wrb_ref[...])
    merged = jax.nn.sigmoid(mf_ref[...].astype(F32)) * four + jax.nn.sigmoid(mr_ref[...].astype(F32)) * ret
    o_ref[...] = merged.astype(o_ref.dtype)


def _merge(fm2, ret2, w_four_out, w_ret_out, p2, rows=512):
    m = fm2.shape[0]
    mf0, mr0 = MF_OFF // COL_TILE, MR_OFF // COL_TILE
    return pl.pallas_call(
        _merge_kernel,
        grid=(D_MODEL // COL_TILE, m // rows),
        in_specs=[
            pl.BlockSpec((rows, FOUR_WIDTH), lambda j, i: (i, 0)),
            pl.BlockSpec((rows, D_MODEL), lambda j, i: (i, 0)),
            pl.BlockSpec((FOUR_WIDTH, COL_TILE), lambda j, i: (0, j)),
            pl.BlockSpec((D_MODEL, COL_TILE), lambda j, i: (0, j)),
            pl.BlockSpec((rows, COL_TILE), lambda j, i: (i, mf0 + j)),
            pl.BlockSpec((rows, COL_TILE), lambda j, i: (i, mr0 + j)),
        ],
        out_specs=pl.BlockSpec((rows, COL_TILE), lambda j, i: (i, j)),
        out_shape=jax.ShapeDtypeStruct((m, D_MODEL), BF16),
        scratch_shapes=[pltpu.VMEM((FOUR_WIDTH, COL_TILE), BF16), pltpu.VMEM((D_MODEL, COL_TILE), BF16)],
        compiler_params=_params("arbitrary", "arbitrary"),
        name="merge",
    )(fm2, ret2, w_four_out, w_ret_out, p2, p2)


def _out_proj_kernel(a_ref, w_ref, x_ref, g_ref, o_ref, wb_ref):
    @pl.when(pl.program_id(1) == 0)
    def _():
        wb_ref[...] = w_ref[...].astype(BF16)

    o_ref[...] = x_ref[...] + g_ref[0] * _dot(a_ref[...], wb_ref[...])


def _out_proj(merged, w_out, x2, mod3, gate_blk0, rows=512):
    m = merged.shape[0]
    return pl.pallas_call(
        _out_proj_kernel,
        grid=(D_MODEL // COL_TILE, m // rows),
        in_specs=[
            pl.BlockSpec((rows, D_MODEL), lambda j, i: (i, 0)),
            pl.BlockSpec((D_MODEL, COL_TILE), lambda j, i: (0, j)),
            pl.BlockSpec((rows, COL_TILE), lambda j, i: (i, j)),
            pl.BlockSpec((1, 1, COL_TILE), lambda j, i: (i * rows // SEQ, 0, gate_blk0 + j)),
        ],
        out_specs=pl.BlockSpec((rows, COL_TILE), lambda j, i: (i, j)),
        out_shape=jax.ShapeDtypeStruct((m, D_MODEL), F32),
        scratch_shapes=[pltpu.VMEM((D_MODEL, COL_TILE), BF16)],
        compiler_params=_params("arbitrary", "arbitrary"),
        name="out_proj",
    )(merged, w_out, x2, mod3)


def _route_kernel(x_ref, g_ref, sh_ref, sc_ref, whi_ref, wlo_ref, b_ref, h_ref, r_ref):
    h = _rms_modulate(x_ref[...], g_ref[...], sh_ref[0], sc_ref[0])
    h_ref[...] = h
    hi = h.astype(BF16)
    lo = (h - hi.astype(F32)).astype(BF16)
    logits = _dot(hi, whi_ref[...]) + _dot(lo, whi_ref[...]) + _dot(hi, wlo_ref[...]) + b_ref[...]

    lane = lax.broadcasted_iota(jnp.int32, logits.shape, 1).astype(F32)
    neg = -jnp.inf
    first = lambda hit: jnp.min(jnp.where(hit, lane, float(LANES)), axis=1, keepdims=True)
    is_grp = lane < float(N_GROUPS)
    gl = jnp.where(is_grp, logits, neg)
    gmax = jnp.max(gl, axis=1, keepdims=True)
    grp = first(gl == gmax)
    g_w = 1.0 / jnp.sum(jnp.where(is_grp, jnp.exp(logits - gmax), 0.0), axis=1, keepdims=True)
    lo_lane = float(N_GROUPS) + grp * float(EXPERTS_PER_GROUP)
    el = jnp.where(lane >= lo_lane, jnp.where(lane < lo_lane + float(EXPERTS_PER_GROUP), logits, neg), neg)
    v1 = jnp.max(el, axis=1, keepdims=True)
    i1 = first(el == v1)
    el2 = jnp.where(lane == i1, neg, el)
    v2 = jnp.max(el2, axis=1, keepdims=True)
    i2 = first(el2 == v2)
    e = jnp.exp(v2 - v1)
    w1 = g_w / (1.0 + e)
    w2 = g_w * e / (1.0 + e)
    out = jnp.where(lane == 0.0, i1 - float(N_GROUPS),
                    jnp.where(lane == 1.0, i2 - float(N_GROUPS),
                              jnp.where(lane == 2.0, w1, jnp.where(lane == 3.0, w2, 0.0))))
    r_ref[...] = out


def _route(x1, g, mod3, shift_blk, scale_blk, w_hi, w_lo, bias, rows=512):
    m = x1.shape[0]
    return pl.pallas_call(
        _route_kernel,
        grid=(m // rows,),
        in_specs=[
            pl.BlockSpec((rows, D_MODEL), lambda i: (i, 0)),
            pl.BlockSpec((1, D_MODEL), lambda i: (0, 0)),
            pl.BlockSpec((1, 1, D_MODEL), lambda i: (i * rows // SEQ, 0, shift_blk)),
            pl.BlockSpec((1, 1, D_MODEL), lambda i: (i * rows // SEQ, 0, scale_blk)),
            pl.BlockSpec((D_MODEL, LANES), lambda i: (0, 0)),
            pl.BlockSpec((D_MODEL, LANES), lambda i: (0, 0)),
            pl.BlockSpec((1, LANES), lambda i: (0, 0)),
        ],
        out_specs=[
            pl.BlockSpec((rows, D_MODEL), lambda i: (i, 0)),
            pl.BlockSpec((rows, LANES), lambda i: (i, 0)),
        ],
        out_shape=[jax.ShapeDtypeStruct((m, D_MODEL), F32), jax.ShapeDtypeStruct((m, LANES), F32)],
        compiler_params=_params("arbitrary"),
        name="route",
    )(x1, g, mod3, mod3, w_hi, w_lo, bias)


def _dispatch_plan(route):
    t = route.shape[0]
    rows = EXPERT_ROWS
    flat = route[:, :2].astype(jnp.int32).reshape(-1)
    onehot = (flat[:, None] == jnp.arange(N_EXPERTS, dtype=jnp.int32)[None, :]).astype(jnp.int32)
    csum = jnp.cumsum(onehot, axis=0)
    rank = jnp.take_along_axis(csum, flat[:, None], axis=1)[:, 0] - 1
    counts = csum[-1]
    tiles = (counts + rows - 1) // rows
    tile_end = jnp.cumsum(tiles)
    tile_start = tile_end - tiles
    pos = tile_start[flat] * rows + rank
    n_used = tile_end[-1]
    tile_ids = jnp.arange(N_EXPERT_TILES, dtype=jnp.int32)
    tile_expert = jnp.sum((tile_end[None, :] <= tile_ids[:, None]).astype(jnp.int32), axis=1)
    tile_expert = jnp.minimum(tile_expert, N_EXPERTS - 1)
    prev = jnp.concatenate([jnp.full((1,), -1, jnp.int32), tile_expert[:-1]])
    tile_first = (tile_expert != prev).astype(jnp.int32)
    row_token = jnp.zeros((N_EXPERT_TILES * rows,), jnp.int32).at[pos].set(
        jnp.arange(2 * t, dtype=jnp.int32) // 2)
    return pos, row_token, tile_expert, tile_first, n_used.astype(jnp.int32).reshape(1)


def _row_gather_copy(src_hbm, row, dst, slot, r, sem):
    return pltpu.make_async_copy(src_hbm.at[pl.ds(row, 1), :], dst.at[slot, pl.ds(r, 1), :], sem.at[slot])


def _experts_kernel(te_ref, tf_ref, nu_ref, tok_ref, tokn_ref, h_hbm, wg_ref, wu_ref, wd_ref, y_ref,
                    xbuf, sem, wgb_ref, wub_ref, wdb_ref):
    rows = EXPERT_ROWS
    i = pl.program_id(0)
    n_used = nu_ref[0]
    slot = i % 2

    def issue(idx_ref, s):
        def body(r, c):
            _row_gather_copy(h_hbm, idx_ref[0, 0, r], xbuf, s, r, sem).start()
            return c
        lax.fori_loop(0, rows, body, 0, unroll=8)

    @pl.when(i == 0)
    def _():
        issue(tok_ref, 0)

    @pl.when(i + 1 < n_used)
    def _():
        issue(tokn_ref, 1 - slot)

    @pl.when(i >= n_used)
    def _():
        y_ref[...] = jnp.zeros_like(y_ref)

    @pl.when(i < n_used)
    def _():
        @pl.when(tf_ref[i] == 1)
        def _():
            wgb_ref[...] = wg_ref[0].astype(BF16)
            wub_ref[...] = wu_ref[0].astype(BF16)
            wdb_ref[...] = wd_ref[0].astype(BF16)

        def wait_body(r, c):
            _row_gather_copy(h_hbm, 0, xbuf, slot, r, sem).wait()
            return c
        lax.fori_loop(0, rows, wait_body, 0, unroll=8)

        x = xbuf[slot].astype(BF16)
        hg = _dot(x, wgb_ref[...])
        hu = _dot(x, wub_ref[...])
        hid = _silu(hg) * hu
        y_ref[...] = _dot(hid.astype(BF16), wdb_ref[...])


def _experts(h2, row_token, tile_expert, tile_first, n_used, w_gate, w_up, w_down):
    rows = EXPERT_ROWS
    nt = N_EXPERT_TILES
    tok3 = row_token.reshape(nt, 1, rows)
    grid_spec = pltpu.PrefetchScalarGridSpec(
        num_scalar_prefetch=3,
        grid=(nt,),
        in_specs=[
            pl.BlockSpec((1, 1, rows), lambda i, te, tf, nu: (i, 0, 0), memory_space=pltpu.SMEM),
            pl.BlockSpec((1, 1, rows), lambda i, te, tf, nu: (jnp.minimum(i + 1, nt - 1), 0, 0),
                         memory_space=pltpu.SMEM),
            pl.BlockSpec(memory_space=pl.ANY),
            pl.BlockSpec((1, D_MODEL, EXPERT_FF), lambda i, te, tf, nu: (te[i], 0, 0)),
            pl.BlockSpec((1, D_MODEL, EXPERT_FF), lambda i, te, tf, nu: (te[i], 0, 0)),
            pl.BlockSpec((1, EXPERT_FF, D_MODEL), lambda i, te, tf, nu: (te[i], 0, 0)),
        ],
        out_specs=pl.BlockSpec((rows, D_MODEL), lambda i, te, tf, nu: (i, 0)),
        scratch_shapes=[
            pltpu.VMEM((2, rows, D_MODEL), F32),
            pltpu.SemaphoreType.DMA((2,)),
            pltpu.VMEM((D_MODEL, EXPERT_FF), BF16),
            pltpu.VMEM((D_MODEL, EXPERT_FF), BF16),
            pltpu.VMEM((EXPERT_FF, D_MODEL), BF16),
        ],
    )
    return pl.pallas_call(
        _experts_kernel,
        grid_spec=grid_spec,
        out_shape=jax.ShapeDtypeStruct((nt * rows, D_MODEL), F32),
        compiler_params=_params("arbitrary"),
        name="experts",
    )(tile_expert, tile_first, n_used, tok3, tok3, h2, w_gate, w_up, w_down)


def _final_kernel(pos_ref, posn_ref, x_ref, r_ref, g2_ref, fg_ref, y_hbm, o_ref, ybuf, sem):
    rows = FINAL_ROWS
    i = pl.program_id(0)
    n = pl.num_programs(0)
    slot = i % 2

    def issue(idx_ref, s):
        def body(r, c):
            _row_gather_copy(y_hbm, idx_ref[0, 0, r], ybuf, s, r, sem).start()
            return c
        lax.fori_loop(0, 2 * rows, body, 0, unroll=8)

    @pl.when(i == 0)
    def _():
        issue(pos_ref, 0)

    @pl.when(i + 1 < n)
    def _():
        issue(posn_ref, 1 - slot)

    def wait_body(r, c):
        _row_gather_copy(y_hbm, 0, ybuf, slot, r, sem).wait()
        return c
    lax.fori_loop(0, 2 * rows, wait_body, 0, unroll=8)

    route = r_ref[...]
    w1 = route[:, 2:3]
    w2 = route[:, 3:4]
    moe = w1 * ybuf[slot, 0:rows, :] + w2 * ybuf[slot, rows:2 * rows, :]
    x = x_ref[...] + g2_ref[0] * moe
    o_ref[...] = x * lax.rsqrt(jnp.mean(x * x, axis=-1, keepdims=True) + EPS) * fg_ref[...]


def _final(x1, route, mod3, gate_blk, final_g, y, pos):
    rows = FINAL_ROWS
    m = x1.shape[0]
    nt = m // rows
    pos3 = pos.reshape(nt, rows, 2).transpose(0, 2, 1).reshape(nt, 1, 2 * rows)
    return pl.pallas_call(
        _final_kernel,
        grid=(nt,),
        in_specs=[
            pl.BlockSpec((1, 1, 2 * rows), lambda i: (i, 0, 0), memory_space=pltpu.SMEM),
            pl.BlockSpec((1, 1, 2 * rows), lambda i: (jnp.minimum(i + 1, nt - 1), 0, 0), memory_space=pltpu.SMEM),
            pl.BlockSpec((rows, D_MODEL), lambda i: (i, 0)),
            pl.BlockSpec((rows, LANES), lambda i: (i, 0)),
            pl.BlockSpec((1, 1, D_MODEL), lambda i: (i * rows // SEQ, 0, gate_blk)),
            pl.BlockSpec((1, D_MODEL), lambda i: (0, 0)),
            pl.BlockSpec(memory_space=pl.ANY),
        ],
        out_specs=pl.BlockSpec((rows, D_MODEL), lambda i: (i, 0)),
        out_shape=jax.ShapeDtypeStruct((m, D_MODEL), F32),
        scratch_shapes=[pltpu.VMEM((2, 2 * rows, D_MODEL), F32), pltpu.SemaphoreType.DMA((2,))],
        compiler_params=_params("arbitrary"),
        name="final",
    )(pos3, pos3, x1, route, mod3, final_g, y)


def _rope_tables():
    rows = SEQ // GRID_W
    row = jnp.repeat(jnp.arange(rows, dtype=F32), GRID_W)
    col = jnp.tile(jnp.arange(GRID_W, dtype=F32), rows)
    n_freq = RET_DK // 4
    inv = ROPE_BASE ** (-jnp.arange(n_freq, dtype=F32) / n_freq)
    ang = jnp.concatenate([row[:, None] * inv, col[:, None] * inv], axis=-1)
    cos, sin = jnp.cos(ang), jnp.sin(ang)
    return jnp.concatenate([cos, cos], axis=-1), jnp.concatenate([-sin, sin], axis=-1)


def kernel(x, c, ctx, c_ctx, w_mod, b_mod, norm1_g, norm2_g, w_in, w_four_out, w_ret_out, w_out,
           ret_decay_f, ret_decay_b, w_group_router, b_group_router, w_expert_router, b_expert_router,
           w_gate, w_up, w_down, final_norm_g):
    b, n, d = x.shape
    assert (n, d) == (SEQ, D_MODEL) and ctx.shape[1] == CTX_LEN and w_mod.shape[0] == 1
    t = b * n

    c8 = jnp.zeros((8, d), F32).at[:b].set(c).at[b].set(c_ctx)
    mod3 = _modulation(c8, w_mod[0], b_mod).reshape(8, 1, N_MOD * d)

    hx = _prenorm(x, norm1_g, mod3, lambda i: i, 0, 1, rows=512)
    hc = _prenorm(ctx, norm1_g, mod3, lambda i: b, 0, 1, rows=CTX_LEN)

    p2 = _project(hx.reshape(t, d), w_in[0], 0, IN_WIDTH // COL_TILE, 1024, "in_proj")
    p3 = p2.reshape(b, n, IN_WIDTH)
    kvc = _project(hc.reshape(b * CTX_LEN, d), w_in[0], K_OFF // COL_TILE, (GF_OFF - K_OFF) // COL_TILE,
                   b * CTX_LEN, "ctx_proj")
    kvc3 = kvc.reshape(b, CTX_LEN, GF_OFF - K_OFF)

    fm = _fourier(p3)

    lg_f = jax.nn.log_sigmoid(ret_decay_f[0].astype(F32))
    lg_b = jax.nn.log_sigmoid(ret_decay_b[0].astype(F32))
    lg = jnp.stack([lg_f, lg_b, jnp.exp(RET_BLOCK * lg_f), jnp.exp(RET_BLOCK * lg_b)])
    cos2, sin2 = _rope_tables()
    ret = _retention(p3, kvc3, lg, cos2, sin2)

    merged = _merge(fm.reshape(t, FOUR_WIDTH), ret.reshape(t, d), w_four_out[0], w_ret_out[0], p2)
    x1 = _out_proj(merged, w_out[0], x.reshape(t, d), mod3, 2 * d // COL_TILE)

    w_router = jnp.concatenate(
        [w_group_router[0], w_expert_router[0].transpose(1, 0, 2).reshape(d, N_EXPERTS)], axis=1).astype(F32)
    w_router = jnp.pad(w_router, ((0, 0), (0, LANES - w_router.shape[1])))
    w_hi = w_router.astype(BF16)
    w_lo = (w_router - w_hi.astype(F32)).astype(BF16)
    bias = jnp.pad(jnp.concatenate([b_group_router[0], b_expert_router[0].reshape(-1)]).astype(F32),
                   (0, LANES - N_GROUPS - N_EXPERTS)).reshape(1, LANES)
    h2, route = _route(x1, norm2_g, mod3, 3, 4, w_hi, w_lo, bias)

    pos, row_token, tile_expert, tile_first, n_used = _dispatch_plan(route)
    y = _experts(h2, row_token, tile_expert, tile_first, n_used, w_gate[0], w_up[0], w_down[0])
    out = _final(x1, route, mod3, 5, final_norm_g.reshape(1, d), y, pos)
    return out.reshape(b, n, d)
```

```python
import functools

import jax
import jax.numpy as jnp
import numpy as np
from jax import lax
from jax.experimental import pallas as pl
from jax.experimental.pallas import tpu as pltpu

F32 = jnp.float32
BF16 = jnp.bfloat16

D_MODEL = 2048
SEQ = 2048
CTX_LEN = 256
GRID_W = 64
N_MOD = 6
FOUR_GROUPS = 8
FOUR_GROUP_DIM = 128
FOUR_WIDTH = FOUR_GROUPS * FOUR_GROUP_DIM
RET_HEADS = 8
RET_DK = 128
RET_DV = 256
ROPE_BASE = 10000.0
N_GROUPS = 4
EXPERTS_PER_GROUP = 8
N_EXPERTS = N_GROUPS * EXPERTS_PER_GROUP
EXPERT_FF = 512
EPS = 1e-6
Q_OFF = FOUR_WIDTH
K_OFF = Q_OFF + RET_HEADS * RET_DK
V_OFF = K_OFF + RET_HEADS * RET_DK
GF_OFF = V_OFF + RET_HEADS * RET_DV
GB_OFF = GF_OFF + RET_HEADS * RET_DV
MF_OFF = GB_OFF + RET_HEADS * RET_DV
MR_OFF = MF_OFF + D_MODEL
IN_WIDTH = MR_OFF + D_MODEL

V7X_VMEM_LIMIT_BYTES = 56 * 1024 * 1024
LANES = 128
RET_BLOCK = 256
COL_TILE = 1024
EXPERT_ROWS = 256
DISPATCH_ROWS = 256
FINAL_ROWS = 256
N_WORK_ITEMS = 2 * 4 * SEQ // EXPERT_ROWS + N_EXPERTS


def _params(*sem):
    return pltpu.CompilerParams(dimension_semantics=sem, vmem_limit_bytes=V7X_VMEM_LIMIT_BYTES)


def _dot(a, b):
    return jnp.dot(a, b, preferred_element_type=F32)


def _mod_kernel(c_ref, w_ref, b_ref, o_ref):
    c = c_ref[...]
    sc = c * jax.nn.sigmoid(c)
    o_ref[...] = _dot(sc.astype(BF16), w_ref[...].astype(BF16)) + b_ref[...]


def _modulation(c8, w_mod, b_mod):
    n_out = w_mod.shape[1]
    return pl.pallas_call(
        _mod_kernel,
        grid=(n_out // COL_TILE,),
        in_specs=[
            pl.BlockSpec((8, D_MODEL), lambda j: (0, 0)),
            pl.BlockSpec((D_MODEL, COL_TILE), lambda j: (0, j)),
            pl.BlockSpec((1, COL_TILE), lambda j: (0, j)),
        ],
        out_specs=pl.BlockSpec((8, COL_TILE), lambda j: (0, j)),
        out_shape=jax.ShapeDtypeStruct((8, n_out), F32),
        compiler_params=_params("arbitrary"),
        name="mod",
    )(c8, w_mod, b_mod)


def _rms_modulate(x, g, shift, scale):
    y = x * lax.rsqrt(jnp.mean(x * x, axis=-1, keepdims=True) + EPS) * g
    return y * (1.0 + scale) + shift


def _prenorm_kernel(x_ref, g_ref, sh_ref, sc_ref, o_ref):
    o_ref[0] = _rms_modulate(x_ref[0], g_ref[...], sh_ref[0], sc_ref[0]).astype(o_ref.dtype)


def _prenorm(x, g, mod3, row_of_batch, shift_blk, scale_blk, rows):
    b, n, d = x.shape
    return pl.pallas_call(
        _prenorm_kernel,
        grid=(b, n // rows),
        in_specs=[
            pl.BlockSpec((1, rows, d), lambda i, j: (i, j, 0)),
            pl.BlockSpec((1, d), lambda i, j: (0, 0)),
            pl.BlockSpec((1, 1, d), lambda i, j: (row_of_batch(i), 0, shift_blk)),
            pl.BlockSpec((1, 1, d), lambda i, j: (row_of_batch(i), 0, scale_blk)),
        ],
        out_specs=pl.BlockSpec((1, rows, d), lambda i, j: (i, j, 0)),
        out_shape=jax.ShapeDtypeStruct((b, n, d), BF16),
        compiler_params=_params("arbitrary", "arbitrary"),
        name="prenorm",
    )(x, g, mod3, mod3)


def _proj_kernel(a_ref, w_ref, o_ref, wb_ref):
    @pl.when(pl.program_id(1) == 0)
    def _():
        wb_ref[...] = w_ref[...].astype(BF16)

    o_ref[...] = _dot(a_ref[...], wb_ref[...]).astype(o_ref.dtype)


def _project(a, w, col_tile0, n_col_tiles, rows, name):
    m, k = a.shape
    return pl.pallas_call(
        _proj_kernel,
        grid=(n_col_tiles, m // rows),
        in_specs=[
            pl.BlockSpec((rows, k), lambda j, i: (i, 0)),
            pl.BlockSpec((k, COL_TILE), lambda j, i: (0, j + col_tile0)),
        ],
        out_specs=pl.BlockSpec((rows, COL_TILE), lambda j, i: (i, j)),
        out_shape=jax.ShapeDtypeStruct((m, n_col_tiles * COL_TILE), BF16),
        scratch_shapes=[pltpu.VMEM((k, COL_TILE), BF16)],
        compiler_params=_params("arbitrary", "arbitrary"),
        name=name,
    )(a, w)


def _dft_tables():
    n = np.arange(SEQ, dtype=np.int64)
    ang_n = 2.0 * np.pi * ((n[:, None] * n[None, :]) % SEQ) / SEQ
    pos = np.concatenate([np.cos(ang_n), -np.sin(ang_n)], axis=1).astype(np.float32)
    c = np.arange(FOUR_GROUP_DIM, dtype=np.int64)
    ang_c = 2.0 * np.pi * ((c[:, None] * c[None, :]) % FOUR_GROUP_DIM) / FOUR_GROUP_DIM
    norm = 1.0 / np.sqrt(float(SEQ * FOUR_GROUP_DIM))
    chan = (np.concatenate([np.cos(ang_c), np.sin(ang_c)], axis=1) * norm).astype(np.float32)
    return pos, chan


def _fourier_kernel(u_ref, chan_ref, pos_ref, o_ref, y_ref):
    @pl.when(pl.program_id(1) == 0)
    def _():
        chan = chan_ref[...]
        for g in range(FOUR_GROUPS):
            cols = slice(g * FOUR_GROUP_DIM, (g + 1) * FOUR_GROUP_DIM)
            y = _dot(u_ref[0, :, cols], chan)
            y_ref[0:SEQ, cols] = y[:, :FOUR_GROUP_DIM].astype(BF16)
            y_ref[SEQ:2 * SEQ, cols] = y[:, FOUR_GROUP_DIM:].astype(BF16)

    o_ref[0] = _dot(pos_ref[...], y_ref[...]).astype(o_ref.dtype)


def _fourier(p3, rows=512):
    b = p3.shape[0]
    pos_np, chan_np = _dft_tables()
    pos = jnp.asarray(pos_np).astype(BF16)
    chan = jnp.asarray(chan_np).astype(BF16)
    return pl.pallas_call(
        _fourier_kernel,
        grid=(b, SEQ // rows),
        in_specs=[
            pl.BlockSpec((1, SEQ, FOUR_WIDTH), lambda i, j: (i, 0, 0)),
            pl.BlockSpec((FOUR_GROUP_DIM, 2 * FOUR_GROUP_DIM), lambda i, j: (0, 0)),
            pl.BlockSpec((rows, 2 * SEQ), lambda i, j: (j, 0)),
        ],
        out_specs=pl.BlockSpec((1, rows, FOUR_WIDTH), lambda i, j: (i, j, 0)),
        out_shape=jax.ShapeDtypeStruct((b, SEQ, FOUR_WIDTH), BF16),
        scratch_shapes=[pltpu.VMEM((2 * SEQ, FOUR_WIDTH), BF16)],
        compiler_params=_params("arbitrary", "arbitrary"),
        name="fourier",
    )(p3, chan, pos)


def _head_norm(o):
    mu = jnp.mean(o, axis=-1, keepdims=True)
    d = o - mu
    var = jnp.mean(d * d, axis=-1, keepdims=True)
    return d * lax.rsqrt(var + EPS)


def _silu(x):
    return x * jax.nn.sigmoid(x)


def _retention_kernel(lg_ref, q_ref, k_ref, v_ref, gf_ref, gb_ref, kc_ref, vc_ref, cos_ref, sin_ref,
                      o_ref, acc_ref, qs_ref, ks_ref, kzb_ref, rf_ref, rb_ref,
                      df_ref, db_ref, xif_ref, xib_ref, zf_ref, zb_ref):
    blk = RET_BLOCK
    n_blk = SEQ // blk
    h = pl.program_id(1)
    lgf, lgb, gchf, gchb = lg_ref[0, h], lg_ref[1, h], lg_ref[2, h], lg_ref[3, h]

    row = lax.broadcasted_iota(jnp.int32, (blk, blk), 0)
    col = lax.broadcasted_iota(jnp.int32, (blk, blk), 1)
    diff = (row - col).astype(F32)
    df_ref[...] = jnp.where(diff >= 0, jnp.exp(jnp.maximum(diff, 0.0) * lgf), 0.0)
    db_ref[...] = jnp.where(diff <= 0, jnp.exp(jnp.maximum(-diff, 0.0) * lgb), 0.0)
    pos_v = lax.broadcasted_iota(jnp.int32, (blk, RET_DV), 0).astype(F32)
    xif_ref[...] = jnp.exp((pos_v + 1.0) * lgf)
    xib_ref[...] = jnp.exp((blk - pos_v) * lgb)
    pos_k = lax.broadcasted_iota(jnp.int32, (blk, RET_DK), 0).astype(F32)
    zf_ref[...] = jnp.exp((blk - 1.0 - pos_k) * lgf)
    zb_ref[...] = jnp.exp(pos_k * lgb)

    def state_update(r_ref, kz, v, gch):
        upd = lax.dot_general(kz, v, (((0,), (0,)), ((), ())), preferred_element_type=F32)
        r_ref[...] = gch * r_ref[...] + upd

    kc = kc_ref[0].astype(F32)
    vc = vc_ref[0]
    rf_ref[...] = jnp.zeros_like(rf_ref)
    rb_ref[...] = jnp.zeros_like(rb_ref)
    state_update(rf_ref, (kc * zf_ref[...]).astype(BF16), vc, gchf)
    state_update(rb_ref, (kc * zb_ref[...]).astype(BF16), vc, gchb)

    def rope(t, rows):
        return t * cos_ref[rows, :] + pltpu.roll(t, RET_DK // 2, axis=1) * sin_ref[rows, :]

    def block_out(qb, kb, vb, r_ref, d_ref, xi_ref):
        s = lax.dot_general(qb, kb, (((1,), (1,)), ((), ())), preferred_element_type=F32)
        inner = _dot((s * d_ref[...]).astype(BF16), vb)
        cross = _dot(qb, r_ref[...].astype(BF16)) * xi_ref[...]
        return inner + cross

    def fwd(i, carry):
        rows = pl.ds(pl.multiple_of(i * blk, blk), blk)
        qr = rope(q_ref[0, rows, :].astype(F32), rows) * (RET_DK ** -0.5)
        kr = rope(k_ref[0, rows, :].astype(F32), rows)
        qb = qr.astype(BF16)
        kb = kr.astype(BF16)
        vb = v_ref[0, rows, :]
        qs_ref[rows, :] = qb
        ks_ref[rows, :] = kb
        kzb_ref[rows, :] = (kr * zb_ref[...]).astype(BF16)
        o = block_out(qb, kb, vb, rf_ref, df_ref, xif_ref)
        state_update(rf_ref, (kr * zf_ref[...]).astype(BF16), vb, gchf)
        acc_ref[rows, :] = _silu(gf_ref[0, rows, :].astype(F32)) * _head_norm(o)
        return carry

    lax.fori_loop(0, n_blk, fwd, 0)

    def bwd(j, carry):
        i = n_blk - 1 - j
        rows = pl.ds(pl.multiple_of(i * blk, blk), blk)
        qb = qs_ref[rows, :]
        kb = ks_ref[rows, :]
        vb = v_ref[0, rows, :]
        o = block_out(qb, kb, vb, rb_ref, db_ref, xib_ref)
        state_update(rb_ref, kzb_ref[rows, :], vb, gchb)
        y = acc_ref[rows, :] + _silu(gb_ref[0, rows, :].astype(F32)) * _head_norm(o)
        o_ref[0, rows, :] = y.astype(o_ref.dtype)
        return carry

    lax.fori_loop(0, n_blk, bwd, 0)


def _retention(p3, kvc3, lg, cos2, sin2):
    b = p3.shape[0]
    assert CTX_LEN == RET_BLOCK
    qb0, kb0 = Q_OFF // RET_DK, K_OFF // RET_DK
    vb0, gfb0, gbb0 = V_OFF // RET_DV, GF_OFF // RET_DV, GB_OFF // RET_DV
    kcv0 = (RET_HEADS * RET_DK) // RET_DV
    return pl.pallas_call(
        _retention_kernel,
        grid=(b, RET_HEADS),
        in_specs=[
            pl.BlockSpec(memory_space=pltpu.SMEM),
            pl.BlockSpec((1, SEQ, RET_DK), lambda i, h: (i, 0, qb0 + h)),
            pl.BlockSpec((1, SEQ, RET_DK), lambda i, h: (i, 0, kb0 + h)),
            pl.BlockSpec((1, SEQ, RET_DV), lambda i, h: (i, 0, vb0 + h)),
            pl.BlockSpec((1, SEQ, RET_DV), lambda i, h: (i, 0, gfb0 + h)),
            pl.BlockSpec((1, SEQ, RET_DV), lambda i, h: (i, 0, gbb0 + h)),
            pl.BlockSpec((1, CTX_LEN, RET_DK), lambda i, h: (i, 0, h)),
            pl.BlockSpec((1, CTX_LEN, RET_DV), lambda i, h: (i, 0, kcv0 + h)),
            pl.BlockSpec((SEQ, RET_DK), lambda i, h: (0, 0)),
            pl.BlockSpec((SEQ, RET_DK), lambda i, h: (0, 0)),
        ],
        out_specs=pl.BlockSpec((1, SEQ, RET_DV), lambda i, h: (i, 0, h)),
        out_shape=jax.ShapeDtypeStruct((b, SEQ, RET_HEADS * RET_DV), BF16),
        scratch_shapes=[
            pltpu.VMEM((SEQ, RET_DV), F32),
            pltpu.VMEM((SEQ, RET_DK), BF16),
            pltpu.VMEM((SEQ, RET_DK), BF16),
            pltpu.VMEM((SEQ, RET_DK), BF16),
            pltpu.VMEM((RET_DK, RET_DV), F32),
            pltpu.VMEM((RET_DK, RET_DV), F32),
            pltpu.VMEM((RET_BLOCK, RET_BLOCK), F32),
            pltpu.VMEM((RET_BLOCK, RET_BLOCK), F32),
            pltpu.VMEM((RET_BLOCK, RET_DV), F32),
            pltpu.VMEM((RET_BLOCK, RET_DV), F32),
            pltpu.VMEM((RET_BLOCK, RET_DK), F32),
            pltpu.VMEM((RET_BLOCK, RET_DK), F32),
        ],
        compiler_params=_params("arbitrary", "arbitrary"),
        name="retention",
    )(lg, p3, p3, p3, p3, p3, kvc3, kvc3, cos2, sin2)


def _merge_kernel(fm_ref, ret_ref, wf_ref, wr_ref, mf_ref, mr_ref, o_ref, wfb_ref, wrb_ref):
    @pl.when(pl.program_id(1) == 0)
    def _():
        wfb_ref[...] = wf_ref[...].astype(BF16)
        wrb_ref[...] = wr_ref[...].astype(BF16)

    four = _dot(fm_ref[...], wfb_ref[...])
    ret = _dot(ret_ref[...], wrb_ref[...])
    merged = jax.nn.sigmoid(mf_ref[...].astype(F32)) * four + jax.nn.sigmoid(mr_ref[...].astype(F32)) * ret
    o_ref[...] = merged.astype(o_ref.dtype)


def _merge(fm2, ret2, w_four_out, w_ret_out, p2, rows=512):
    m = fm2.shape[0]
    mf0, mr0 = MF_OFF // COL_TILE, MR_OFF // COL_TILE
    return pl.pallas_call(
        _merge_kernel,
        grid=(D_MODEL // COL_TILE, m // rows),
        in_specs=[
            pl.BlockSpec((rows, FOUR_WIDTH), lambda j, i: (i, 0)),
            pl.BlockSpec((rows, D_MODEL), lambda j, i: (i, 0)),
            pl.BlockSpec((FOUR_WIDTH, COL_TILE), lambda j, i: (0, j)),
            pl.BlockSpec((D_MODEL, COL_TILE), lambda j, i: (0, j)),
            pl.BlockSpec((rows, COL_TILE), lambda j, i: (i, mf0 + j)),
            pl.BlockSpec((rows, COL_TILE), lambda j, i: (i, mr0 + j)),
        ],
        out_specs=pl.BlockSpec((rows, COL_TILE), lambda j, i: (i, j)),
        out_shape=jax.ShapeDtypeStruct((m, D_MODEL), BF16),
        scratch_shapes=[pltpu.VMEM((FOUR_WIDTH, COL_TILE), BF16), pltpu.VMEM((D_MODEL, COL_TILE), BF16)],
        compiler_params=_params("arbitrary", "arbitrary"),
        name="merge",
    )(fm2, ret2, w_four_out, w_ret_out, p2, p2)


def _out_proj_kernel(a_ref, w_ref, x_ref, g_ref, o_ref, wb_ref):
    @pl.when(pl.program_id(1) == 0)
    def _():
        wb_ref[...] = w_ref[...].astype(BF16)

    o_ref[...] = x_ref[...] + g_ref[0] * _dot(a_ref[...], wb_ref[...])


def _out_proj(merged, w_out, x2, mod3, gate_blk0, rows=512):
    m = merged.shape[0]
    return pl.pallas_call(
        _out_proj_kernel,
        grid=(D_MODEL // COL_TILE, m // rows),
        in_specs=[
            pl.BlockSpec((rows, D_MODEL), lambda j, i: (i, 0)),
            pl.BlockSpec((D_MODEL, COL_TILE), lambda j, i: (0, j)),
            pl.BlockSpec((rows, COL_TILE), lambda j, i: (i, j)),
            pl.BlockSpec((1, 1, COL_TILE), lambda j, i: (i * rows // SEQ, 0, gate_blk0 + j)),
        ],
        out_specs=pl.BlockSpec((rows, COL_TILE), lambda j, i: (i, j)),
        out_shape=jax.ShapeDtypeStruct((m, D_MODEL), F32),
        scratch_shapes=[pltpu.VMEM((D_MODEL, COL_TILE), BF16)],
        compiler_params=_params("arbitrary", "arbitrary"),
        name="out_proj",
    )(merged, w_out, x2, mod3)


def _route_kernel(x_ref, g_ref, sh_ref, sc_ref, whi_ref, wlo_ref, b_ref, r_ref):
    h = _rms_modulate(x_ref[...], g_ref[...], sh_ref[0], sc_ref[0])
    hi = h.astype(BF16)
    lo = (h - hi.astype(F32)).astype(BF16)
    logits = _dot(hi, whi_ref[...]) + _dot(lo, whi_ref[...]) + _dot(hi, wlo_ref[...]) + b_ref[...]

    lane = lax.broadcasted_iota(jnp.int32, logits.shape, 1).astype(F32)
    neg = -jnp.inf
    first = lambda hit: jnp.min(jnp.where(hit, lane, float(LANES)), axis=1, keepdims=True)
    is_grp = lane < float(N_GROUPS)
    gl = jnp.where(is_grp, logits, neg)
    gmax = jnp.max(gl, axis=1, keepdims=True)
    grp = first(gl == gmax)
    g_w = 1.0 / jnp.sum(jnp.where(is_grp, jnp.exp(logits - gmax), 0.0), axis=1, keepdims=True)
    lo_lane = float(N_GROUPS) + grp * float(EXPERTS_PER_GROUP)
    el = jnp.where(lane >= lo_lane, jnp.where(lane < lo_lane + float(EXPERTS_PER_GROUP), logits, neg), neg)
    v1 = jnp.max(el, axis=1, keepdims=True)
    i1 = first(el == v1)
    el2 = jnp.where(lane == i1, neg, el)
    v2 = jnp.max(el2, axis=1, keepdims=True)
    i2 = first(el2 == v2)
    e = jnp.exp(v2 - v1)
    w1 = g_w / (1.0 + e)
    w2 = g_w * e / (1.0 + e)
    out = jnp.where(lane == 0.0, i1 - float(N_GROUPS),
                    jnp.where(lane == 1.0, i2 - float(N_GROUPS),
                              jnp.where(lane == 2.0, w1, jnp.where(lane == 3.0, w2, 0.0))))
    r_ref[...] = out


def _route(x1, g, mod3, shift_blk, scale_blk, w_hi, w_lo, bias, rows=512):
    m = x1.shape[0]
    return pl.pallas_call(
        _route_kernel,
        grid=(m // rows,),
        in_specs=[
            pl.BlockSpec((rows, D_MODEL), lambda i: (i, 0)),
            pl.BlockSpec((1, D_MODEL), lambda i: (0, 0)),
            pl.BlockSpec((1, 1, D_MODEL), lambda i: (i * rows // SEQ, 0, shift_blk)),
            pl.BlockSpec((1, 1, D_MODEL), lambda i: (i * rows // SEQ, 0, scale_blk)),
            pl.BlockSpec((D_MODEL, LANES), lambda i: (0, 0)),
            pl.BlockSpec((D_MODEL, LANES), lambda i: (0, 0)),
            pl.BlockSpec((1, LANES), lambda i: (0, 0)),
        ],
        out_specs=pl.BlockSpec((rows, LANES), lambda i: (i, 0)),
        out_shape=jax.ShapeDtypeStruct((m, LANES), F32),
        compiler_params=_params("arbitrary"),
        name="route",
    )(x1, g, mod3, mod3, w_hi, w_lo, bias)


def _dispatch_plan(route):
    rows = EXPERT_ROWS
    flat = route[:, :2].astype(jnp.int32).reshape(-1)
    n_tiles = flat.shape[0] // rows
    experts = jnp.arange(N_EXPERTS, dtype=jnp.int32)
    onehot = (flat[:, None] == experts[None, :]).astype(jnp.int32)
    csum = jnp.cumsum(onehot, axis=0)
    rank = jnp.sum(csum * onehot, axis=1) - 1
    counts = csum[-1]
    cend = jnp.cumsum(counts)
    cstart = cend - counts
    pos = jnp.sum(cstart[None, :] * onehot, axis=1) + rank
    first_tile = cstart // rows
    items = jnp.where(counts > 0, (cend - 1) // rows - first_tile + 1, 0)
    wend = jnp.cumsum(items)
    wstart = wend - items
    n_work = wend[-1]
    w = jnp.arange(N_WORK_ITEMS, dtype=jnp.int32)
    valid = w < n_work
    last_expert = jnp.max(jnp.where(counts > 0, experts, 0))
    e_w = jnp.minimum(jnp.sum((wend[None, :] <= w[:, None]).astype(jnp.int32), axis=1), N_EXPERTS - 1)
    e_w = jnp.where(valid, e_w, last_expert)
    sel = (e_w[:, None] == experts[None, :]).astype(jnp.int32)
    pick = lambda v: jnp.sum(sel * v[None, :], axis=1)
    tile_w = jnp.where(valid, pick(first_tile) + w - pick(wstart), n_tiles - 1)
    lo_w = jnp.where(valid, jnp.clip(pick(cstart) - tile_w * rows, 0, rows), 0)
    hi_w = jnp.where(valid, jnp.clip(pick(cend) - tile_w * rows, 0, rows), 0)
    prev = lambda v: jnp.concatenate([jnp.full((1,), -1, jnp.int32), v[:-1]])
    new_expert = (e_w != prev(e_w)).astype(jnp.int32)
    new_tile = (tile_w != prev(tile_w)).astype(jnp.int32)
    plan = jnp.stack([tile_w, e_w, new_expert, new_tile, lo_w, hi_w,
                      jnp.broadcast_to(n_work, w.shape), jnp.zeros_like(w)]).astype(jnp.int32)
    return pos, plan


def _dispatch_kernel(pos_ref, x_ref, g_ref, sh_ref, sc_ref, xs_hbm, hbuf, sem):
    rows = DISPATCH_ROWS
    i = pl.program_id(0)
    n = pl.num_programs(0)
    slot = i % 2

    def row_copy(s, r, dst_row):
        return pltpu.make_async_copy(hbuf.at[s, pl.ds(r, 1), :], xs_hbm.at[pl.ds(dst_row, 1), :], sem.at[s])

    def drain(s):
        def body(r, c):
            row_copy(s, 0, 0).wait()
            return c
        lax.fori_loop(0, 2 * rows, body, 0, unroll=8)

    @pl.when(i >= 2)
    def _():
        drain(slot)

    hbuf[slot] = _rms_modulate(x_ref[...], g_ref[...], sh_ref[0], sc_ref[0])

    def issue(r, c):
        row_copy(slot, r, pos_ref[0, 0, r]).start()
        row_copy(slot, r, pos_ref[0, 0, rows + r]).start()
        return c
    lax.fori_loop(0, rows, issue, 0, unroll=8)

    @pl.when(i == n - 1)
    def _():
        drain(slot)
        drain(1 - slot)


def _dispatch(x1, g, mod3, shift_blk, scale_blk, pos3, n_sorted):
    rows = DISPATCH_ROWS
    m = x1.shape[0]
    assert m // rows >= 2
    return pl.pallas_call(
        _dispatch_kernel,
        grid=(m // rows,),
        in_specs=[
            pl.BlockSpec((1, 1, 2 * rows), lambda i: (i, 0, 0), memory_space=pltpu.SMEM),
            pl.BlockSpec((rows, D_MODEL), lambda i: (i, 0)),
            pl.BlockSpec((1, D_MODEL), lambda i: (0, 0)),
            pl.BlockSpec((1, 1, D_MODEL), lambda i: (i * rows // SEQ, 0, shift_blk)),
            pl.BlockSpec((1, 1, D_MODEL), lambda i: (i * rows // SEQ, 0, scale_blk)),
        ],
        out_specs=pl.BlockSpec(memory_space=pl.ANY),
        out_shape=jax.ShapeDtypeStruct((n_sorted, D_MODEL), F32),
        scratch_shapes=[pltpu.VMEM((2, rows, D_MODEL), F32), pltpu.SemaphoreType.DMA((2,))],
        compiler_params=_params("arbitrary"),
        name="dispatch",
    )(pos3, x1, g, mod3, mod3)


def _experts_kernel(plan_ref, x_ref, wg_ref, wu_ref, wd_ref, y_ref, wgb_ref, wub_ref, wdb_ref):
    w = pl.program_id(0)

    @pl.when(w < plan_ref[6, w])
    def _():
        @pl.when(plan_ref[2, w] == 1)
        def _():
            wgb_ref[...] = wg_ref[0].astype(BF16)
            wub_ref[...] = wu_ref[0].astype(BF16)
            wdb_ref[...] = wd_ref[0].astype(BF16)

        x = x_ref[...].astype(BF16)
        hg = _dot(x, wgb_ref[...])
        hu = _dot(x, wub_ref[...])
        hid = _silu(hg) * hu
        y = _dot(hid.astype(BF16), wdb_ref[...])
        row = lax.broadcasted_iota(jnp.int32, (EXPERT_ROWS, 1), 0)
        mine = jnp.logical_and(row >= plan_ref[4, w], row < plan_ref[5, w])

        @pl.when(plan_ref[3, w] == 1)
        def _():
            y_ref[...] = jnp.where(mine, y, 0.0)

        @pl.when(plan_ref[3, w] == 0)
        def _():
            y_ref[...] = jnp.where(mine, y, y_ref[...])


def _experts(xs, plan, w_gate, w_up, w_down):
    rows = EXPERT_ROWS
    grid_spec = pltpu.PrefetchScalarGridSpec(
        num_scalar_prefetch=1,
        grid=(N_WORK_ITEMS,),
        in_specs=[
            pl.BlockSpec((rows, D_MODEL), lambda w, p: (p[0, w], 0)),
            pl.BlockSpec((1, D_MODEL, EXPERT_FF), lambda w, p: (p[1, w], 0, 0)),
            pl.BlockSpec((1, D_MODEL, EXPERT_FF), lambda w, p: (p[1, w], 0, 0)),
            pl.BlockSpec((1, EXPERT_FF, D_MODEL), lambda w, p: (p[1, w], 0, 0)),
        ],
        out_specs=pl.BlockSpec((rows, D_MODEL), lambda w, p: (p[0, w], 0)),
        scratch_shapes=[
            pltpu.VMEM((D_MODEL, EXPERT_FF), BF16),
            pltpu.VMEM((D_MODEL, EXPERT_FF), BF16),
            pltpu.VMEM((EXPERT_FF, D_MODEL), BF16),
        ],
    )
    return pl.pallas_call(
        _experts_kernel,
        grid_spec=grid_spec,
        out_shape=jax.ShapeDtypeStruct(xs.shape, F32),
        compiler_params=_params("arbitrary"),
        name="experts",
    )(plan, xs, w_gate, w_up, w_down)


def _final_kernel(pos_ref, posn_ref, x_ref, r_ref, g2_ref, fg_ref, y_hbm, o_ref, ybuf, sem):
    rows = FINAL_ROWS
    i = pl.program_id(0)
    n = pl.num_programs(0)
    slot = i % 2

    def row_copy(src_row, s, r):
        return pltpu.make_async_copy(y_hbm.at[pl.ds(src_row, 1), :], ybuf.at[s, pl.ds(r, 1), :], sem.at[s])

    def issue(idx_ref, s):
        def body(r, c):
            row_copy(idx_ref[0, 0, r], s, r).start()
            return c
        lax.fori_loop(0, 2 * rows, body, 0, unroll=8)

    @pl.when(i == 0)
    def _():
        issue(pos_ref, 0)

    @pl.when(i + 1 < n)
    def _():
        issue(posn_ref, 1 - slot)

    def wait_body(r, c):
        row_copy(0, slot, r).wait()
        return c
    lax.fori_loop(0, 2 * rows, wait_body, 0, unroll=8)

    route = r_ref[...]
    w1 = route[:, 2:3]
    w2 = route[:, 3:4]
    moe = w1 * ybuf[slot, 0:rows, :] + w2 * ybuf[slot, rows:2 * rows, :]
    x = x_ref[...] + g2_ref[0] * moe
    o_ref[...] = x * lax.rsqrt(jnp.mean(x * x, axis=-1, keepdims=True) + EPS) * fg_ref[...]


def _final(x1, route, mod3, gate_blk, final_g, y, pos3):
    rows = FINAL_ROWS
    m = x1.shape[0]
    nt = m // rows
    return pl.pallas_call(
        _final_kernel,
        grid=(nt,),
        in_specs=[
            pl.BlockSpec((1, 1, 2 * rows), lambda i: (i, 0, 0), memory_space=pltpu.SMEM),
            pl.BlockSpec((1, 1, 2 * rows), lambda i: (jnp.minimum(i + 1, nt - 1), 0, 0), memory_space=pltpu.SMEM),
            pl.BlockSpec((rows, D_MODEL), lambda i: (i, 0)),
            pl.BlockSpec((rows, LANES), lambda i: (i, 0)),
            pl.BlockSpec((1, 1, D_MODEL), lambda i: (i * rows // SEQ, 0, gate_blk)),
            pl.BlockSpec((1, D_MODEL), lambda i: (0, 0)),
            pl.BlockSpec(memory_space=pl.ANY),
        ],
        out_specs=pl.BlockSpec((rows, D_MODEL), lambda i: (i, 0)),
        out_shape=jax.ShapeDtypeStruct((m, D_MODEL), F32),
        scratch_shapes=[pltpu.VMEM((2, 2 * rows, D_MODEL), F32), pltpu.SemaphoreType.DMA((2,))],
        compiler_params=_params("arbitrary"),
        name="final",
    )(pos3, pos3, x1, route, mod3, final_g, y)


def _rope_tables():
    rows = SEQ // GRID_W
    row = jnp.repeat(jnp.arange(rows, dtype=F32), GRID_W)
    col = jnp.tile(jnp.arange(GRID_W, dtype=F32), rows)
    n_freq = RET_DK // 4
    inv = ROPE_BASE ** (-jnp.arange(n_freq, dtype=F32) / n_freq)
    ang = jnp.concatenate([row[:, None] * inv, col[:, None] * inv], axis=-1)
    cos, sin = jnp.cos(ang), jnp.sin(ang)
    return jnp.concatenate([cos, cos], axis=-1), jnp.concatenate([-sin, sin], axis=-1)


def kernel(x, c, ctx, c_ctx, w_mod, b_mod, norm1_g, norm2_g, w_in, w_four_out, w_ret_out, w_out,
           ret_decay_f, ret_decay_b, w_group_router, b_group_router, w_expert_router, b_expert_router,
           w_gate, w_up, w_down, final_norm_g):
    b, n, d = x.shape
    assert (n, d) == (SEQ, D_MODEL) and ctx.shape[1] == CTX_LEN and w_mod.shape[0] == 1
    t = b * n

    c8 = jnp.zeros((8, d), F32).at[:b].set(c).at[b].set(c_ctx)
    mod3 = _modulation(c8, w_mod[0], b_mod).reshape(8, 1, N_MOD * d)

    hx = _prenorm(x, norm1_g, mod3, lambda i: i, 0, 1, rows=512)
    hc = _prenorm(ctx, norm1_g, mod3, lambda i: b, 0, 1, rows=CTX_LEN)

    p2 = _project(hx.reshape(t, d), w_in[0], 0, IN_WIDTH // COL_TILE, 1024, "in_proj")
    p3 = p2.reshape(b, n, IN_WIDTH)
    kvc = _project(hc.reshape(b * CTX_LEN, d), w_in[0], K_OFF // COL_TILE, (GF_OFF - K_OFF) // COL_TILE,
                   b * CTX_LEN, "ctx_proj")
    kvc3 = kvc.reshape(b, CTX_LEN, GF_OFF - K_OFF)

    fm = _fourier(p3)

    lg_f = jax.nn.log_sigmoid(ret_decay_f[0].astype(F32))
    lg_b = jax.nn.log_sigmoid(ret_decay_b[0].astype(F32))
    lg = jnp.stack([lg_f, lg_b, jnp.exp(RET_BLOCK * lg_f), jnp.exp(RET_BLOCK * lg_b)])
    cos2, sin2 = _rope_tables()
    ret = _retention(p3, kvc3, lg, cos2, sin2)

    merged = _merge(fm.reshape(t, FOUR_WIDTH), ret.reshape(t, d), w_four_out[0], w_ret_out[0], p2)
    x1 = _out_proj(merged, w_out[0], x.reshape(t, d), mod3, 2 * d // COL_TILE)

    w_router = jnp.concatenate(
        [w_group_router[0], w_expert_router[0].transpose(1, 0, 2).reshape(d, N_EXPERTS)], axis=1).astype(F32)
    w_router = jnp.pad(w_router, ((0, 0), (0, LANES - w_router.shape[1])))
    w_hi = w_router.astype(BF16)
    w_lo = (w_router - w_hi.astype(F32)).astype(BF16)
    bias = jnp.pad(jnp.concatenate([b_group_router[0], b_expert_router[0].reshape(-1)]).astype(F32),
                   (0, LANES - N_GROUPS - N_EXPERTS)).reshape(1, LANES)
    route = _route(x1, norm2_g, mod3, 3, 4, w_hi, w_lo, bias)

    pos, plan = _dispatch_plan(route)
    assert DISPATCH_ROWS == FINAL_ROWS
    pos3 = pos.reshape(t // FINAL_ROWS, FINAL_ROWS, 2).transpose(0, 2, 1).reshape(t // FINAL_ROWS, 1, 2 * FINAL_ROWS)
    xs = _dispatch(x1, norm2_g, mod3, 3, 4, pos3, 2 * t)
    y = _experts(xs, plan, w_gate[0], w_up[0], w_down[0])
    out = _final(x1, route, mod3, 5, final_norm_g.reshape(1, d), y, pos3)
    return out.reshape(b, n, d)
```

```python
import functools

import jax
import jax.numpy as jnp
import numpy as np
from jax import lax
from jax.experimental import pallas as pl
from jax.experimental.pallas import tpu as pltpu

F32 = jnp.float32
BF16 = jnp.bfloat16

D_MODEL = 2048
SEQ = 2048
CTX_LEN = 256
GRID_W = 64
N_MOD = 6
FOUR_GROUPS = 8
FOUR_GROUP_DIM = 128
FOUR_WIDTH = FOUR_GROUPS * FOUR_GROUP_DIM
RET_HEADS = 8
RET_DK = 128
RET_DV = 256
ROPE_BASE = 10000.0
N_GROUPS = 4
EXPERTS_PER_GROUP = 8
N_EXPERTS = N_GROUPS * EXPERTS_PER_GROUP
EXPERT_FF = 512
EPS = 1e-6
Q_OFF = FOUR_WIDTH
K_OFF = Q_OFF + RET_HEADS * RET_DK
V_OFF = K_OFF + RET_HEADS * RET_DK
GF_OFF = V_OFF + RET_HEADS * RET_DV
GB_OFF = GF_OFF + RET_HEADS * RET_DV
MF_OFF = GB_OFF + RET_HEADS * RET_DV
MR_OFF = MF_OFF + D_MODEL
IN_WIDTH = MR_OFF + D_MODEL

V7X_VMEM_LIMIT_BYTES = 56 * 1024 * 1024
LANES = 128
RET_BLOCK = 256
COL_TILE = 1024
EXPERT_ROWS = 256
DISPATCH_ROWS = 256
FINAL_ROWS = 256
N_WORK_ITEMS = 2 * 4 * SEQ // EXPERT_ROWS + N_EXPERTS


def _params(*sem):
    return pltpu.CompilerParams(dimension_semantics=sem, vmem_limit_bytes=V7X_VMEM_LIMIT_BYTES)


def _dot(a, b):
    return jnp.dot(a, b, preferred_element_type=F32)


def _mod_kernel(c_ref, w_ref, b_ref, o_ref):
    c = c_ref[...]
    sc = c * jax.nn.sigmoid(c)
    o_ref[...] = _dot(sc.astype(BF16), w_ref[...].astype(BF16)) + b_ref[...]


def _modulation(c8, w_mod, b_mod):
    n_out = w_mod.shape[1]
    return pl.pallas_call(
        _mod_kernel,
        grid=(n_out // COL_TILE,),
        in_specs=[
            pl.BlockSpec((8, D_MODEL), lambda j: (0, 0)),
            pl.BlockSpec((D_MODEL, COL_TILE), lambda j: (0, j)),
            pl.BlockSpec((1, COL_TILE), lambda j: (0, j)),
        ],
        out_specs=pl.BlockSpec((8, COL_TILE), lambda j: (0, j)),
        out_shape=jax.ShapeDtypeStruct((8, n_out), F32),
        compiler_params=_params("arbitrary"),
        name="mod",
    )(c8, w_mod, b_mod)


def _rms_modulate(x, g, shift, scale):
    y = x * lax.rsqrt(jnp.mean(x * x, axis=-1, keepdims=True) + EPS) * g
    return y * (1.0 + scale) + shift


def _prenorm_kernel(x_ref, g_ref, sh_ref, sc_ref, o_ref):
    o_ref[0] = _rms_modulate(x_ref[0], g_ref[...], sh_ref[0], sc_ref[0]).astype(o_ref.dtype)


def _prenorm(x, g, mod3, row_of_batch, shift_blk, scale_blk, rows):
    b, n, d = x.shape
    return pl.pallas_call(
        _prenorm_kernel,
        grid=(b, n // rows),
        in_specs=[
            pl.BlockSpec((1, rows, d), lambda i, j: (i, j, 0)),
            pl.BlockSpec((1, d), lambda i, j: (0, 0)),
            pl.BlockSpec((1, 1, d), lambda i, j: (row_of_batch(i), 0, shift_blk)),
            pl.BlockSpec((1, 1, d), lambda i, j: (row_of_batch(i), 0, scale_blk)),
        ],
        out_specs=pl.BlockSpec((1, rows, d), lambda i, j: (i, j, 0)),
        out_shape=jax.ShapeDtypeStruct((b, n, d), BF16),
        compiler_params=_params("arbitrary", "arbitrary"),
        name="prenorm",
    )(x, g, mod3, mod3)


def _proj_kernel(a_ref, w_ref, o_ref, wb_ref):
    @pl.when(pl.program_id(1) == 0)
    def _():
        wb_ref[...] = w_ref[...].astype(BF16)

    o_ref[...] = _dot(a_ref[...], wb_ref[...]).astype(o_ref.dtype)


def _project(a, w, col_tile0, n_col_tiles, rows, name):
    m, k = a.shape
    return pl.pallas_call(
        _proj_kernel,
        grid=(n_col_tiles, m // rows),
        in_specs=[
            pl.BlockSpec((rows, k), lambda j, i: (i, 0)),
            pl.BlockSpec((k, COL_TILE), lambda j, i: (0, j + col_tile0)),
        ],
        out_specs=pl.BlockSpec((rows, COL_TILE), lambda j, i: (i, j)),
        out_shape=jax.ShapeDtypeStruct((m, n_col_tiles * COL_TILE), BF16),
        scratch_shapes=[pltpu.VMEM((k, COL_TILE), BF16)],
        compiler_params=_params("arbitrary", "arbitrary"),
        name=name,
    )(a, w)


def _rope(t, cos2, sin2):
    return t * cos2 + pltpu.roll(t, RET_DK // 2, axis=1) * sin2


def _in_proj_kernel(a_ref, w_ref, cos_ref, sin_ref, o_ref, wb_ref):
    j = pl.program_id(0)

    @pl.when(pl.program_id(1) == 0)
    def _():
        wb_ref[...] = w_ref[...].astype(BF16)

    q_tile, k_tile = Q_OFF // COL_TILE, K_OFF // COL_TILE
    gate_tile, merge_tile = GF_OFF // COL_TILE, MF_OFF // COL_TILE
    is_qk = jnp.logical_or(j == q_tile, j == k_tile)
    is_gate = jnp.logical_and(j >= gate_tile, j < merge_tile)

    @pl.when(jnp.logical_and(j < gate_tile, jnp.logical_not(is_qk)))
    def _():
        o_ref[...] = _dot(a_ref[...], wb_ref[...]).astype(o_ref.dtype)

    @pl.when(is_qk)
    def _():
        acc = _dot(a_ref[...], wb_ref[...])
        scale = jnp.where(j == q_tile, RET_DK ** -0.5, 1.0).astype(F32)
        cos2, sin2 = cos_ref[...], sin_ref[...]
        for h in range(COL_TILE // RET_DK):
            cols = slice(h * RET_DK, (h + 1) * RET_DK)
            o_ref[:, cols] = (_rope(acc[:, cols], cos2, sin2) * scale).astype(o_ref.dtype)

    @pl.when(is_gate)
    def _():
        o_ref[...] = _silu(_dot(a_ref[...], wb_ref[...])).astype(o_ref.dtype)

    @pl.when(j >= merge_tile)
    def _():
        o_ref[...] = jax.nn.sigmoid(_dot(a_ref[...], wb_ref[...])).astype(o_ref.dtype)


def _in_proj(a, w, cos2, sin2, rows=1024):
    m, k = a.shape
    n_col_tiles = IN_WIDTH // COL_TILE
    pos_blocks = SEQ // rows
    return pl.pallas_call(
        _in_proj_kernel,
        grid=(n_col_tiles, m // rows),
        in_specs=[
            pl.BlockSpec((rows, k), lambda j, i: (i, 0)),
            pl.BlockSpec((k, COL_TILE), lambda j, i: (0, j)),
            pl.BlockSpec((rows, RET_DK), lambda j, i: (i % pos_blocks, 0)),
            pl.BlockSpec((rows, RET_DK), lambda j, i: (i % pos_blocks, 0)),
        ],
        out_specs=pl.BlockSpec((rows, COL_TILE), lambda j, i: (i, j)),
        out_shape=jax.ShapeDtypeStruct((m, IN_WIDTH), BF16),
        scratch_shapes=[pltpu.VMEM((k, COL_TILE), BF16)],
        compiler_params=_params("arbitrary", "arbitrary"),
        name="in_proj",
    )(a, w, cos2, sin2)


def _dft_tables():
    n = np.arange(SEQ, dtype=np.int64)
    ang_n = 2.0 * np.pi * ((n[:, None] * n[None, :]) % SEQ) / SEQ
    pos = np.concatenate([np.cos(ang_n), -np.sin(ang_n)], axis=1).astype(np.float32)
    c = np.arange(FOUR_GROUP_DIM, dtype=np.int64)
    ang_c = 2.0 * np.pi * ((c[:, None] * c[None, :]) % FOUR_GROUP_DIM) / FOUR_GROUP_DIM
    norm = 1.0 / np.sqrt(float(SEQ * FOUR_GROUP_DIM))
    chan = (np.concatenate([np.cos(ang_c), np.sin(ang_c)], axis=1) * norm).astype(np.float32)
    return pos, chan


def _fourier_kernel(u_ref, chan_ref, pos_ref, o_ref, y_ref):
    @pl.when(pl.program_id(1) == 0)
    def _():
        chan = chan_ref[...]
        for g in range(FOUR_GROUPS):
            cols = slice(g * FOUR_GROUP_DIM, (g + 1) * FOUR_GROUP_DIM)
            y = _dot(u_ref[0, :, cols], chan)
            y_ref[0:SEQ, cols] = y[:, :FOUR_GROUP_DIM].astype(BF16)
            y_ref[SEQ:2 * SEQ, cols] = y[:, FOUR_GROUP_DIM:].astype(BF16)

    o_ref[0] = _dot(pos_ref[...], y_ref[...]).astype(o_ref.dtype)


def _fourier(p3, rows=512):
    b = p3.shape[0]
    pos_np, chan_np = _dft_tables()
    pos = jnp.asarray(pos_np).astype(BF16)
    chan = jnp.asarray(chan_np).astype(BF16)
    return pl.pallas_call(
        _fourier_kernel,
        grid=(b, SEQ // rows),
        in_specs=[
            pl.BlockSpec((1, SEQ, FOUR_WIDTH), lambda i, j: (i, 0, 0)),
            pl.BlockSpec((FOUR_GROUP_DIM, 2 * FOUR_GROUP_DIM), lambda i, j: (0, 0)),
            pl.BlockSpec((rows, 2 * SEQ), lambda i, j: (j, 0)),
        ],
        out_specs=pl.BlockSpec((1, rows, FOUR_WIDTH), lambda i, j: (i, j, 0)),
        out_shape=jax.ShapeDtypeStruct((b, SEQ, FOUR_WIDTH), BF16),
        scratch_shapes=[pltpu.VMEM((2 * SEQ, FOUR_WIDTH), BF16)],
        compiler_params=_params("arbitrary", "arbitrary"),
        name="fourier",
    )(p3, chan, pos)


def _head_norm(o):
    mu = jnp.mean(o, axis=-1, keepdims=True)
    d = o - mu
    var = jnp.mean(d * d, axis=-1, keepdims=True)
    return d * lax.rsqrt(var + EPS)


def _silu(x):
    return x * jax.nn.sigmoid(x)


def _retention_kernel(lg_ref, q_ref, k_ref, v_ref, gf_ref, gb_ref, kc_ref, vc_ref,
                      o_ref, acc_ref, df_ref, db_ref, xif_ref, xib_ref, zf_ref, zb_ref):
    blk = RET_BLOCK
    n_blk = SEQ // blk
    h = pl.program_id(0)
    lgf, lgb, gchf, gchb = lg_ref[0, h], lg_ref[1, h], lg_ref[2, h], lg_ref[3, h]

    @pl.when(pl.program_id(1) == 0)
    def _():
        row = lax.broadcasted_iota(jnp.int32, (blk, blk), 0)
        col = lax.broadcasted_iota(jnp.int32, (blk, blk), 1)
        diff = (row - col).astype(F32)
        df_ref[...] = jnp.where(diff >= 0, jnp.exp(jnp.maximum(diff, 0.0) * lgf), 0.0)
        db_ref[...] = jnp.where(diff <= 0, jnp.exp(jnp.maximum(-diff, 0.0) * lgb), 0.0)
        pos_v = lax.broadcasted_iota(jnp.int32, (blk, RET_DV), 0).astype(F32)
        xif_ref[...] = jnp.exp((pos_v + 1.0) * lgf)
        xib_ref[...] = jnp.exp((blk - pos_v) * lgb)
        pos_k = lax.broadcasted_iota(jnp.int32, (blk, RET_DK), 0).astype(F32)
        zf_ref[...] = jnp.exp((blk - 1.0 - pos_k) * lgf)
        zb_ref[...] = jnp.exp(pos_k * lgb)

    def state_update(r, k, z_ref, v, gch):
        kz = (k.astype(F32) * z_ref[...]).astype(BF16)
        return gch * r + lax.dot_general(kz, v, (((0,), (0,)), ((), ())), preferred_element_type=F32)

    def block_out(qb, kb, vb, r, d_ref, xi_ref):
        s = lax.dot_general(qb, kb, (((1,), (1,)), ((), ())), preferred_element_type=F32)
        inner = _dot((s * d_ref[...]).astype(BF16), vb)
        cross = _dot(qb, r.astype(BF16)) * xi_ref[...]
        return inner + cross

    zero = jnp.zeros((RET_DK, RET_DV), F32)
    r_f = state_update(zero, kc_ref[0], zf_ref, vc_ref[0], gchf)
    r_b = state_update(zero, kc_ref[0], zb_ref, vc_ref[0], gchb)

    def emit(i, part, reached_first):
        rows = slice(i * blk, (i + 1) * blk)
        if reached_first:
            acc_ref[rows, :] = part
        else:
            o_ref[0, rows, :] = (acc_ref[rows, :] + part).astype(o_ref.dtype)

    for j in range(n_blk):
        i_f, i_b = j, n_blk - 1 - j
        rows_f = slice(i_f * blk, (i_f + 1) * blk)
        rows_b = slice(i_b * blk, (i_b + 1) * blk)
        qf, kf, vf = q_ref[0, rows_f, :], k_ref[0, rows_f, :], v_ref[0, rows_f, :]
        qk, kk, vk = q_ref[0, rows_b, :], k_ref[0, rows_b, :], v_ref[0, rows_b, :]
        o_f = block_out(qf, kf, vf, r_f, df_ref, xif_ref)
        o_b = block_out(qk, kk, vk, r_b, db_ref, xib_ref)
        r_f = state_update(r_f, kf, zf_ref, vf, gchf)
        r_b = state_update(r_b, kk, zb_ref, vk, gchb)
        first_visit = j < n_blk - 1 - j
        emit(i_f, gf_ref[0, rows_f, :].astype(F32) * _head_norm(o_f), first_visit)
        emit(i_b, gb_ref[0, rows_b, :].astype(F32) * _head_norm(o_b), first_visit)


def _retention(p3, kvc3, lg):
    b = p3.shape[0]
    assert CTX_LEN == RET_BLOCK and (SEQ // RET_BLOCK) % 2 == 0
    qb0, kb0 = Q_OFF // RET_DK, K_OFF // RET_DK
    vb0, gfb0, gbb0 = V_OFF // RET_DV, GF_OFF // RET_DV, GB_OFF // RET_DV
    kcv0 = (RET_HEADS * RET_DK) // RET_DV
    return pl.pallas_call(
        _retention_kernel,
        grid=(RET_HEADS, b),
        in_specs=[
            pl.BlockSpec(memory_space=pltpu.SMEM),
            pl.BlockSpec((1, SEQ, RET_DK), lambda h, i: (i, 0, qb0 + h)),
            pl.BlockSpec((1, SEQ, RET_DK), lambda h, i: (i, 0, kb0 + h)),
            pl.BlockSpec((1, SEQ, RET_DV), lambda h, i: (i, 0, vb0 + h)),
            pl.BlockSpec((1, SEQ, RET_DV), lambda h, i: (i, 0, gfb0 + h)),
            pl.BlockSpec((1, SEQ, RET_DV), lambda h, i: (i, 0, gbb0 + h)),
            pl.BlockSpec((1, CTX_LEN, RET_DK), lambda h, i: (i, 0, h)),
            pl.BlockSpec((1, CTX_LEN, RET_DV), lambda h, i: (i, 0, kcv0 + h)),
        ],
        out_specs=pl.BlockSpec((1, SEQ, RET_DV), lambda h, i: (i, 0, h)),
        out_shape=jax.ShapeDtypeStruct((b, SEQ, RET_HEADS * RET_DV), BF16),
        scratch_shapes=[
            pltpu.VMEM((SEQ, RET_DV), F32),
            pltpu.VMEM((RET_BLOCK, RET_BLOCK), F32),
            pltpu.VMEM((RET_BLOCK, RET_BLOCK), F32),
            pltpu.VMEM((RET_BLOCK, RET_DV), F32),
            pltpu.VMEM((RET_BLOCK, RET_DV), F32),
            pltpu.VMEM((RET_BLOCK, RET_DK), F32),
            pltpu.VMEM((RET_BLOCK, RET_DK), F32),
        ],
        compiler_params=_params("arbitrary", "arbitrary"),
        name="retention",
    )(lg, p3, p3, p3, p3, p3, kvc3, kvc3)


def _merge_kernel(fm_ref, ret_ref, wf_ref, wr_ref, mf_ref, mr_ref, o_ref, wfb_ref, wrb_ref):
    @pl.when(pl.program_id(1) == 0)
    def _():
        wfb_ref[...] = wf_ref[...].astype(BF16)
        wrb_ref[...] = wr_ref[...].astype(BF16)

    four = _dot(fm_ref[...], wfb_ref[...])
    ret = _dot(ret_ref[...], wrb_ref[...])
    merged = mf_ref[...].astype(F32) * four + mr_ref[...].astype(F32) * ret
    o_ref[...] = merged.astype(o_ref.dtype)


def _merge(fm2, ret2, w_four_out, w_ret_out, p2, rows=512):
    m = fm2.shape[0]
    mf0, mr0 = MF_OFF // COL_TILE, MR_OFF // COL_TILE
    return pl.pallas_call(
        _merge_kernel,
        grid=(D_MODEL // COL_TILE, m // rows),
        in_specs=[
            pl.BlockSpec((rows, FOUR_WIDTH), lambda j, i: (i, 0)),
            pl.BlockSpec((rows, D_MODEL), lambda j, i: (i, 0)),
            pl.BlockSpec((FOUR_WIDTH, COL_TILE), lambda j, i: (0, j)),
            pl.BlockSpec((D_MODEL, COL_TILE), lambda j, i: (0, j)),
            pl.BlockSpec((rows, COL_TILE), lambda j, i: (i, mf0 + j)),
            pl.BlockSpec((rows, COL_TILE), lambda j, i: (i, mr0 + j)),
        ],
        out_specs=pl.BlockSpec((rows, COL_TILE), lambda j, i: (i, j)),
        out_shape=jax.ShapeDtypeStruct((m, D_MODEL), BF16),
        scratch_shapes=[pltpu.VMEM((FOUR_WIDTH, COL_TILE), BF16), pltpu.VMEM((D_MODEL, COL_TILE), BF16)],
        compiler_params=_params("arbitrary", "arbitrary"),
        name="merge",
    )(fm2, ret2, w_four_out, w_ret_out, p2, p2)


def _out_proj_kernel(a_ref, w_ref, x_ref, g_ref, o_ref, wb_ref):
    @pl.when(pl.program_id(1) == 0)
    def _():
        wb_ref[...] = w_ref[...].astype(BF16)

    o_ref[...] = x_ref[...] + g_ref[0] * _dot(a_ref[...], wb_ref[...])


def _out_proj(merged, w_out, x2, mod3, gate_blk0, rows=512):
    m = merged.shape[0]
    return pl.pallas_call(
        _out_proj_kernel,
        grid=(D_MODEL // COL_TILE, m // rows),
        in_specs=[
            pl.BlockSpec((rows, D_MODEL), lambda j, i: (i, 0)),
            pl.BlockSpec((D_MODEL, COL_TILE), lambda j, i: (0, j)),
            pl.BlockSpec((rows, COL_TILE), lambda j, i: (i, j)),
            pl.BlockSpec((1, 1, COL_TILE), lambda j, i: (i * rows // SEQ, 0, gate_blk0 + j)),
        ],
        out_specs=pl.BlockSpec((rows, COL_TILE), lambda j, i: (i, j)),
        out_shape=jax.ShapeDtypeStruct((m, D_MODEL), F32),
        scratch_shapes=[pltpu.VMEM((D_MODEL, COL_TILE), BF16)],
        compiler_params=_params("arbitrary", "arbitrary"),
        name="out_proj",
    )(merged, w_out, x2, mod3)


def _route_kernel(x_ref, g_ref, sh_ref, sc_ref, whi_ref, wlo_ref, b_ref, r_ref):
    h = _rms_modulate(x_ref[...], g_ref[...], sh_ref[0], sc_ref[0])
    hi = h.astype(BF16)
    lo = (h - hi.astype(F32)).astype(BF16)
    logits = _dot(hi, whi_ref[...]) + _dot(lo, whi_ref[...]) + _dot(hi, wlo_ref[...]) + b_ref[...]

    lane = lax.broadcasted_iota(jnp.int32, logits.shape, 1).astype(F32)
    neg = -jnp.inf
    first = lambda hit: jnp.min(jnp.where(hit, lane, float(LANES)), axis=1, keepdims=True)
    is_grp = lane < float(N_GROUPS)
    gl = jnp.where(is_grp, logits, neg)
    gmax = jnp.max(gl, axis=1, keepdims=True)
    grp = first(gl == gmax)
    g_w = 1.0 / jnp.sum(jnp.where(is_grp, jnp.exp(logits - gmax), 0.0), axis=1, keepdims=True)
    lo_lane = float(N_GROUPS) + grp * float(EXPERTS_PER_GROUP)
    el = jnp.where(lane >= lo_lane, jnp.where(lane < lo_lane + float(EXPERTS_PER_GROUP), logits, neg), neg)
    v1 = jnp.max(el, axis=1, keepdims=True)
    i1 = first(el == v1)
    el2 = jnp.where(lane == i1, neg, el)
    v2 = jnp.max(el2, axis=1, keepdims=True)
    i2 = first(el2 == v2)
    e = jnp.exp(v2 - v1)
    w1 = g_w / (1.0 + e)
    w2 = g_w * e / (1.0 + e)
    out = jnp.where(lane == 0.0, i1 - float(N_GROUPS),
                    jnp.where(lane == 1.0, i2 - float(N_GROUPS),
                              jnp.where(lane == 2.0, w1, jnp.where(lane == 3.0, w2, 0.0))))
    r_ref[...] = out


def _route(x1, g, mod3, shift_blk, scale_blk, w_hi, w_lo, bias, rows=512):
    m = x1.shape[0]
    return pl.pallas_call(
        _route_kernel,
        grid=(m // rows,),
        in_specs=[
            pl.BlockSpec((rows, D_MODEL), lambda i: (i, 0)),
            pl.BlockSpec((1, D_MODEL), lambda i: (0, 0)),
            pl.BlockSpec((1, 1, D_MODEL), lambda i: (i * rows // SEQ, 0, shift_blk)),
            pl.BlockSpec((1, 1, D_MODEL), lambda i: (i * rows // SEQ, 0, scale_blk)),
            pl.BlockSpec((D_MODEL, LANES), lambda i: (0, 0)),
            pl.BlockSpec((D_MODEL, LANES), lambda i: (0, 0)),
            pl.BlockSpec((1, LANES), lambda i: (0, 0)),
        ],
        out_specs=pl.BlockSpec((rows, LANES), lambda i: (i, 0)),
        out_shape=jax.ShapeDtypeStruct((m, LANES), F32),
        compiler_params=_params("arbitrary"),
        name="route",
    )(x1, g, mod3, mod3, w_hi, w_lo, bias)


def _dispatch_plan(route):
    rows = EXPERT_ROWS
    flat = route[:, :2].astype(jnp.int32).reshape(-1)
    n_tiles = flat.shape[0] // rows
    experts = jnp.arange(N_EXPERTS, dtype=jnp.int32)
    onehot = (flat[:, None] == experts[None, :]).astype(jnp.int32)
    csum = jnp.cumsum(onehot, axis=0)
    rank = jnp.sum(csum * onehot, axis=1) - 1
    counts = csum[-1]
    cend = jnp.cumsum(counts)
    cstart = cend - counts
    pos = jnp.sum(cstart[None, :] * onehot, axis=1) + rank
    first_tile = cstart // rows
    items = jnp.where(counts > 0, (cend - 1) // rows - first_tile + 1, 0)
    wend = jnp.cumsum(items)
    wstart = wend - items
    n_work = wend[-1]
    w = jnp.arange(N_WORK_ITEMS, dtype=jnp.int32)
    valid = w < n_work
    last_expert = jnp.max(jnp.where(counts > 0, experts, 0))
    e_w = jnp.minimum(jnp.sum((wend[None, :] <= w[:, None]).astype(jnp.int32), axis=1), N_EXPERTS - 1)
    e_w = jnp.where(valid, e_w, last_expert)
    sel = (e_w[:, None] == experts[None, :]).astype(jnp.int32)
    pick = lambda v: jnp.sum(sel * v[None, :], axis=1)
    tile_w = jnp.where(valid, pick(first_tile) + w - pick(wstart), n_tiles - 1)
    lo_w = jnp.where(valid, jnp.clip(pick(cstart) - tile_w * rows, 0, rows), 0)
    hi_w = jnp.where(valid, jnp.clip(pick(cend) - tile_w * rows, 0, rows), 0)
    prev = lambda v: jnp.concatenate([jnp.full((1,), -1, jnp.int32), v[:-1]])
    new_expert = (e_w != prev(e_w)).astype(jnp.int32)
    new_tile = (tile_w != prev(tile_w)).astype(jnp.int32)
    plan = jnp.stack([tile_w, e_w, new_expert, new_tile, lo_w, hi_w,
                      jnp.broadcast_to(n_work, w.shape), jnp.zeros_like(w)]).astype(jnp.int32)
    return pos, plan


def _dispatch_kernel(pos_ref, x_ref, g_ref, sh_ref, sc_ref, xs_hbm, hbuf, sem):
    rows = DISPATCH_ROWS
    i = pl.program_id(0)
    n = pl.num_programs(0)
    slot = i % 2

    def row_copy(s, r, dst_row):
        return pltpu.make_async_copy(hbuf.at[s, pl.ds(r, 1), :], xs_hbm.at[pl.ds(dst_row, 1), :], sem.at[s])

    def drain(s):
        def body(r, c):
            row_copy(s, 0, 0).wait()
            return c
        lax.fori_loop(0, 2 * rows, body, 0, unroll=8)

    @pl.when(i >= 2)
    def _():
        drain(slot)

    hbuf[slot] = _rms_modulate(x_ref[...], g_ref[...], sh_ref[0], sc_ref[0])

    def issue(r, c):
        row_copy(slot, r, pos_ref[0, 0, r]).start()
        row_copy(slot, r, pos_ref[0, 0, rows + r]).start()
        return c
    lax.fori_loop(0, rows, issue, 0, unroll=8)

    @pl.when(i == n - 1)
    def _():
        drain(slot)
        drain(1 - slot)


def _dispatch(x1, g, mod3, shift_blk, scale_blk, pos3, n_sorted):
    rows = DISPATCH_ROWS
    m = x1.shape[0]
    assert m // rows >= 2
    return pl.pallas_call(
        _dispatch_kernel,
        grid=(m // rows,),
        in_specs=[
            pl.BlockSpec((1, 1, 2 * rows), lambda i: (i, 0, 0), memory_space=pltpu.SMEM),
            pl.BlockSpec((rows, D_MODEL), lambda i: (i, 0)),
            pl.BlockSpec((1, D_MODEL), lambda i: (0, 0)),
            pl.BlockSpec((1, 1, D_MODEL), lambda i: (i * rows // SEQ, 0, shift_blk)),
            pl.BlockSpec((1, 1, D_MODEL), lambda i: (i * rows // SEQ, 0, scale_blk)),
        ],
        out_specs=pl.BlockSpec(memory_space=pl.ANY),
        out_shape=jax.ShapeDtypeStruct((n_sorted, D_MODEL), F32),
        scratch_shapes=[pltpu.VMEM((2, rows, D_MODEL), F32), pltpu.SemaphoreType.DMA((2,))],
        compiler_params=_params("arbitrary"),
        name="dispatch",
    )(pos3, x1, g, mod3, mod3)


def _experts_kernel(plan_ref, x_ref, wg_ref, wu_ref, wd_ref, y_ref, wgb_ref, wub_ref, wdb_ref):
    w = pl.program_id(0)

    @pl.when(w < plan_ref[6, w])
    def _():
        @pl.when(plan_ref[2, w] == 1)
        def _():
            wgb_ref[...] = wg_ref[0].astype(BF16)
            wub_ref[...] = wu_ref[0].astype(BF16)
            wdb_ref[...] = wd_ref[0].astype(BF16)

        x = x_ref[...].astype(BF16)
        hg = _dot(x, wgb_ref[...])
        hu = _dot(x, wub_ref[...])
        hid = _silu(hg) * hu
        y = _dot(hid.astype(BF16), wdb_ref[...])
        row = lax.broadcasted_iota(jnp.int32, (EXPERT_ROWS, 1), 0)
        mine = jnp.logical_and(row >= plan_ref[4, w], row < plan_ref[5, w])

        @pl.when(plan_ref[3, w] == 1)
        def _():
            y_ref[...] = jnp.where(mine, y, 0.0)

        @pl.when(plan_ref[3, w] == 0)
        def _():
            y_ref[...] = jnp.where(mine, y, y_ref[...])


def _experts(xs, plan, w_gate, w_up, w_down):
    rows = EXPERT_ROWS
    grid_spec = pltpu.PrefetchScalarGridSpec(
        num_scalar_prefetch=1,
        grid=(N_WORK_ITEMS,),
        in_specs=[
            pl.BlockSpec((rows, D_MODEL), lambda w, p: (p[0, w], 0)),
            pl.BlockSpec((1, D_MODEL, EXPERT_FF), lambda w, p: (p[1, w], 0, 0)),
            pl.BlockSpec((1, D_MODEL, EXPERT_FF), lambda w, p: (p[1, w], 0, 0)),
            pl.BlockSpec((1, EXPERT_FF, D_MODEL), lambda w, p: (p[1, w], 0, 0)),
        ],
        out_specs=pl.BlockSpec((rows, D_MODEL), lambda w, p: (p[0, w], 0)),
        scratch_shapes=[
            pltpu.VMEM((D_MODEL, EXPERT_FF), BF16),
            pltpu.VMEM((D_MODEL, EXPERT_FF), BF16),
            pltpu.VMEM((EXPERT_FF, D_MODEL), BF16),
        ],
    )
    return pl.pallas_call(
        _experts_kernel,
        grid_spec=grid_spec,
        out_shape=jax.ShapeDtypeStruct(xs.shape, F32),
        compiler_params=_params("arbitrary"),
        name="experts",
    )(plan, xs, w_gate, w_up, w_down)


def _final_kernel(pos_ref, posn_ref, x_ref, r_ref, g2_ref, fg_ref, y_hbm, o_ref, ybuf, sem):
    rows = FINAL_ROWS
    i = pl.program_id(0)
    n = pl.num_programs(0)
    slot = i % 2

    def row_copy(src_row, s, r):
        return pltpu.make_async_copy(y_hbm.at[pl.ds(src_row, 1), :], ybuf.at[s, pl.ds(r, 1), :], sem.at[s])

    def issue(idx_ref, s):
        def body(r, c):
            row_copy(idx_ref[0, 0, r], s, r).start()
            return c
        lax.fori_loop(0, 2 * rows, body, 0, unroll=8)

    @pl.when(i == 0)
    def _():
        issue(pos_ref, 0)

    @pl.when(i + 1 < n)
    def _():
        issue(posn_ref, 1 - slot)

    def wait_body(r, c):
        row_copy(0, slot, r).wait()
        return c
    lax.fori_loop(0, 2 * rows, wait_body, 0, unroll=8)

    route = r_ref[...]
    w1 = route[:, 2:3]
    w2 = route[:, 3:4]
    moe = w1 * ybuf[slot, 0:rows, :] + w2 * ybuf[slot, rows:2 * rows, :]
    x = x_ref[...] + g2_ref[0] * moe
    o_ref[...] = x * lax.rsqrt(jnp.mean(x * x, axis=-1, keepdims=True) + EPS) * fg_ref[...]


def _final(x1, route, mod3, gate_blk, final_g, y, pos3):
    rows = FINAL_ROWS
    m = x1.shape[0]
    nt = m // rows
    return pl.pallas_call(
        _final_kernel,
        grid=(nt,),
        in_specs=[
            pl.BlockSpec((1, 1, 2 * rows), lambda i: (i, 0, 0), memory_space=pltpu.SMEM),
            pl.BlockSpec((1, 1, 2 * rows), lambda i: (jnp.minimum(i + 1, nt - 1), 0, 0), memory_space=pltpu.SMEM),
            pl.BlockSpec((rows, D_MODEL), lambda i: (i, 0)),
            pl.BlockSpec((rows, LANES), lambda i: (i, 0)),
            pl.BlockSpec((1, 1, D_MODEL), lambda i: (i * rows // SEQ, 0, gate_blk)),
            pl.BlockSpec((1, D_MODEL), lambda i: (0, 0)),
            pl.BlockSpec(memory_space=pl.ANY),
        ],
        out_specs=pl.BlockSpec((rows, D_MODEL), lambda i: (i, 0)),
        out_shape=jax.ShapeDtypeStruct((m, D_MODEL), F32),
        scratch_shapes=[pltpu.VMEM((2, 2 * rows, D_MODEL), F32), pltpu.SemaphoreType.DMA((2,))],
        compiler_params=_params("arbitrary"),
        name="final",
    )(pos3, pos3, x1, route, mod3, final_g, y)


def _rope_tables():
    rows = SEQ // GRID_W
    row = jnp.repeat(jnp.arange(rows, dtype=F32), GRID_W)
    col = jnp.tile(jnp.arange(GRID_W, dtype=F32), rows)
    n_freq = RET_DK // 4
    inv = ROPE_BASE ** (-jnp.arange(n_freq, dtype=F32) / n_freq)
    ang = jnp.concatenate([row[:, None] * inv, col[:, None] * inv], axis=-1)
    cos, sin = jnp.cos(ang), jnp.sin(ang)
    return jnp.concatenate([cos, cos], axis=-1), jnp.concatenate([-sin, sin], axis=-1)


def kernel(x, c, ctx, c_ctx, w_mod, b_mod, norm1_g, norm2_g, w_in, w_four_out, w_ret_out, w_out,
           ret_decay_f, ret_decay_b, w_group_router, b_group_router, w_expert_router, b_expert_router,
           w_gate, w_up, w_down, final_norm_g):
    b, n, d = x.shape
    assert (n, d) == (SEQ, D_MODEL) and ctx.shape[1] == CTX_LEN and w_mod.shape[0] == 1
    t = b * n

    c8 = jnp.zeros((8, d), F32).at[:b].set(c).at[b].set(c_ctx)
    mod3 = _modulation(c8, w_mod[0], b_mod).reshape(8, 1, N_MOD * d)

    hx = _prenorm(x, norm1_g, mod3, lambda i: i, 0, 1, rows=512)
    hc = _prenorm(ctx, norm1_g, mod3, lambda i: b, 0, 1, rows=CTX_LEN)

    cos2, sin2 = _rope_tables()
    p2 = _in_proj(hx.reshape(t, d), w_in[0], cos2, sin2)
    p3 = p2.reshape(b, n, IN_WIDTH)
    kvc = _project(hc.reshape(b * CTX_LEN, d), w_in[0], K_OFF // COL_TILE, (GF_OFF - K_OFF) // COL_TILE,
                   b * CTX_LEN, "ctx_proj")
    kvc3 = kvc.reshape(b, CTX_LEN, GF_OFF - K_OFF)

    fm = _fourier(p3)

    lg_f = jax.nn.log_sigmoid(ret_decay_f[0].astype(F32))
    lg_b = jax.nn.log_sigmoid(ret_decay_b[0].astype(F32))
    lg = jnp.stack([lg_f, lg_b, jnp.exp(RET_BLOCK * lg_f), jnp.exp(RET_BLOCK * lg_b)])
    ret = _retention(p3, kvc3, lg)

    merged = _merge(fm.reshape(t, FOUR_WIDTH), ret.reshape(t, d), w_four_out[0], w_ret_out[0], p2)
    x1 = _out_proj(merged, w_out[0], x.reshape(t, d), mod3, 2 * d // COL_TILE)

    w_router = jnp.concatenate(
        [w_group_router[0], w_expert_router[0].transpose(1, 0, 2).reshape(d, N_EXPERTS)], axis=1).astype(F32)
    w_router = jnp.pad(w_router, ((0, 0), (0, LANES - w_router.shape[1])))
    w_hi = w_router.astype(BF16)
    w_lo = (w_router - w_hi.astype(F32)).astype(BF16)
    bias = jnp.pad(jnp.concatenate([b_group_router[0], b_expert_router[0].reshape(-1)]).astype(F32),
                   (0, LANES - N_GROUPS - N_EXPERTS)).reshape(1, LANES)
    route = _route(x1, norm2_g, mod3, 3, 4, w_hi, w_lo, bias)

    pos, plan = _dispatch_plan(route)
    assert DISPATCH_ROWS == FINAL_ROWS
    pos3 = pos.reshape(t // FINAL_ROWS, FINAL_ROWS, 2).transpose(0, 2, 1).reshape(t // FINAL_ROWS, 1, 2 * FINAL_ROWS)
    xs = _dispatch(x1, norm2_g, mod3, 3, 4, pos3, 2 * t)
    y = _experts(xs, plan, w_gate[0], w_up[0], w_down[0])
    out = _final(x1, route, mod3, 5, final_norm_g.reshape(1, d), y, pos3)
    return out.reshape(b, n, d)
```

```python
import functools

import jax
import jax.numpy as jnp
import numpy as np
from jax import lax
from jax.experimental import pallas as pl
from jax.experimental.pallas import tpu as pltpu

F32 = jnp.float32
BF16 = jnp.bfloat16

D_MODEL = 2048
SEQ = 2048
CTX_LEN = 256
GRID_W = 64
N_MOD = 6
FOUR_GROUPS = 8
FOUR_GROUP_DIM = 128
FOUR_WIDTH = FOUR_GROUPS * FOUR_GROUP_DIM
RET_HEADS = 8
RET_DK = 128
RET_DV = 256
ROPE_BASE = 10000.0
N_GROUPS = 4
EXPERTS_PER_GROUP = 8
N_EXPERTS = N_GROUPS * EXPERTS_PER_GROUP
EXPERT_FF = 512
EPS = 1e-6
Q_OFF = FOUR_WIDTH
K_OFF = Q_OFF + RET_HEADS * RET_DK
V_OFF = K_OFF + RET_HEADS * RET_DK
GF_OFF = V_OFF + RET_HEADS * RET_DV
GB_OFF = GF_OFF + RET_HEADS * RET_DV
MF_OFF = GB_OFF + RET_HEADS * RET_DV
MR_OFF = MF_OFF + D_MODEL
IN_WIDTH = MR_OFF + D_MODEL

V7X_VMEM_LIMIT_BYTES = 56 * 1024 * 1024
LANES = 128
RET_BLOCK = 256
COL_TILE = 1024
EXPERT_ROWS = 256
DISPATCH_ROWS = 256
FINAL_ROWS = 256
N_WORK_ITEMS = 2 * 4 * SEQ // EXPERT_ROWS + N_EXPERTS


def _params(*sem):
    return pltpu.CompilerParams(dimension_semantics=sem, vmem_limit_bytes=V7X_VMEM_LIMIT_BYTES)


def _dot(a, b):
    return jnp.dot(a, b, preferred_element_type=F32)


def _mod_kernel(c_ref, w_ref, b_ref, o_ref):
    c = c_ref[...]
    sc = c * jax.nn.sigmoid(c)
    o_ref[...] = _dot(sc.astype(BF16), w_ref[...].astype(BF16)) + b_ref[...]


def _modulation(c8, w_mod, b_mod):
    n_out = w_mod.shape[1]
    return pl.pallas_call(
        _mod_kernel,
        grid=(n_out // COL_TILE,),
        in_specs=[
            pl.BlockSpec((8, D_MODEL), lambda j: (0, 0)),
            pl.BlockSpec((D_MODEL, COL_TILE), lambda j: (0, j)),
            pl.BlockSpec((1, COL_TILE), lambda j: (0, j)),
        ],
        out_specs=pl.BlockSpec((8, COL_TILE), lambda j: (0, j)),
        out_shape=jax.ShapeDtypeStruct((8, n_out), F32),
        compiler_params=_params("arbitrary"),
        name="mod",
    )(c8, w_mod, b_mod)


def _rms_modulate(x, g, shift, scale):
    y = x * lax.rsqrt(jnp.mean(x * x, axis=-1, keepdims=True) + EPS) * g
    return y * (1.0 + scale) + shift


def _prenorm_kernel(x_ref, g_ref, sh_ref, sc_ref, o_ref):
    o_ref[0] = _rms_modulate(x_ref[0], g_ref[...], sh_ref[0], sc_ref[0]).astype(o_ref.dtype)


def _prenorm(x, g, mod3, row_of_batch, shift_blk, scale_blk, rows):
    b, n, d = x.shape
    return pl.pallas_call(
        _prenorm_kernel,
        grid=(b, n // rows),
        in_specs=[
            pl.BlockSpec((1, rows, d), lambda i, j: (i, j, 0)),
            pl.BlockSpec((1, d), lambda i, j: (0, 0)),
            pl.BlockSpec((1, 1, d), lambda i, j: (row_of_batch(i), 0, shift_blk)),
            pl.BlockSpec((1, 1, d), lambda i, j: (row_of_batch(i), 0, scale_blk)),
        ],
        out_specs=pl.BlockSpec((1, rows, d), lambda i, j: (i, j, 0)),
        out_shape=jax.ShapeDtypeStruct((b, n, d), BF16),
        compiler_params=_params("arbitrary", "arbitrary"),
        name="prenorm",
    )(x, g, mod3, mod3)


def _proj_kernel(a_ref, w_ref, o_ref, wb_ref):
    @pl.when(pl.program_id(1) == 0)
    def _():
        wb_ref[...] = w_ref[...].astype(BF16)

    o_ref[...] = _dot(a_ref[...], wb_ref[...]).astype(o_ref.dtype)


def _project(a, w, col_tile0, n_col_tiles, rows, name):
    m, k = a.shape
    return pl.pallas_call(
        _proj_kernel,
        grid=(n_col_tiles, m // rows),
        in_specs=[
            pl.BlockSpec((rows, k), lambda j, i: (i, 0)),
            pl.BlockSpec((k, COL_TILE), lambda j, i: (0, j + col_tile0)),
        ],
        out_specs=pl.BlockSpec((rows, COL_TILE), lambda j, i: (i, j)),
        out_shape=jax.ShapeDtypeStruct((m, n_col_tiles * COL_TILE), BF16),
        scratch_shapes=[pltpu.VMEM((k, COL_TILE), BF16)],
        compiler_params=_params("arbitrary", "arbitrary"),
        name=name,
    )(a, w)


def _rope(t, cos2, sin2):
    return t * cos2 + pltpu.roll(t, RET_DK // 2, axis=1) * sin2


def _in_proj_kernel(a_ref, w_ref, cos_ref, sin_ref, o_ref, wb_ref):
    j = pl.program_id(0)

    @pl.when(pl.program_id(1) == 0)
    def _():
        wb_ref[...] = w_ref[...].astype(BF16)

    q_tile, k_tile = Q_OFF // COL_TILE, K_OFF // COL_TILE
    gate_tile, merge_tile = GF_OFF // COL_TILE, MF_OFF // COL_TILE
    is_qk = jnp.logical_or(j == q_tile, j == k_tile)
    is_gate = jnp.logical_and(j >= gate_tile, j < merge_tile)

    @pl.when(jnp.logical_and(j < gate_tile, jnp.logical_not(is_qk)))
    def _():
        o_ref[...] = _dot(a_ref[...], wb_ref[...]).astype(o_ref.dtype)

    @pl.when(is_qk)
    def _():
        acc = _dot(a_ref[...], wb_ref[...])
        scale = jnp.where(j == q_tile, RET_DK ** -0.5, 1.0).astype(F32)
        cos2, sin2 = cos_ref[...], sin_ref[...]
        for h in range(COL_TILE // RET_DK):
            cols = slice(h * RET_DK, (h + 1) * RET_DK)
            o_ref[:, cols] = (_rope(acc[:, cols], cos2, sin2) * scale).astype(o_ref.dtype)

    @pl.when(is_gate)
    def _():
        o_ref[...] = _silu(_dot(a_ref[...], wb_ref[...])).astype(o_ref.dtype)

    @pl.when(j >= merge_tile)
    def _():
        o_ref[...] = jax.nn.sigmoid(_dot(a_ref[...], wb_ref[...])).astype(o_ref.dtype)


def _in_proj(a, w, cos2, sin2, rows=1024):
    m, k = a.shape
    n_col_tiles = IN_WIDTH // COL_TILE
    pos_blocks = SEQ // rows
    return pl.pallas_call(
        _in_proj_kernel,
        grid=(n_col_tiles, m // rows),
        in_specs=[
            pl.BlockSpec((rows, k), lambda j, i: (i, 0)),
            pl.BlockSpec((k, COL_TILE), lambda j, i: (0, j)),
            pl.BlockSpec((rows, RET_DK), lambda j, i: (i % pos_blocks, 0)),
            pl.BlockSpec((rows, RET_DK), lambda j, i: (i % pos_blocks, 0)),
        ],
        out_specs=pl.BlockSpec((rows, COL_TILE), lambda j, i: (i, j)),
        out_shape=jax.ShapeDtypeStruct((m, IN_WIDTH), BF16),
        scratch_shapes=[pltpu.VMEM((k, COL_TILE), BF16)],
        compiler_params=_params("arbitrary", "arbitrary"),
        name="in_proj",
    )(a, w, cos2, sin2)


def _dft_tables():
    n = np.arange(SEQ, dtype=np.int64)
    ang_n = 2.0 * np.pi * ((n[:, None] * n[None, :]) % SEQ) / SEQ
    pos = np.concatenate([np.cos(ang_n), -np.sin(ang_n)], axis=1).astype(np.float32)
    c = np.arange(FOUR_GROUP_DIM, dtype=np.int64)
    ang_c = 2.0 * np.pi * ((c[:, None] * c[None, :]) % FOUR_GROUP_DIM) / FOUR_GROUP_DIM
    norm = 1.0 / np.sqrt(float(SEQ * FOUR_GROUP_DIM))
    chan = (np.concatenate([np.cos(ang_c), np.sin(ang_c)], axis=1) * norm).astype(np.float32)
    return pos, chan


def _fourier_kernel(u_ref, chan_ref, pos_ref, o_ref, y_ref):
    @pl.when(pl.program_id(1) == 0)
    def _():
        chan = chan_ref[...]
        for g in range(FOUR_GROUPS):
            cols = slice(g * FOUR_GROUP_DIM, (g + 1) * FOUR_GROUP_DIM)
            y = _dot(u_ref[0, :, cols], chan)
            y_ref[0:SEQ, cols] = y[:, :FOUR_GROUP_DIM].astype(BF16)
            y_ref[SEQ:2 * SEQ, cols] = y[:, FOUR_GROUP_DIM:].astype(BF16)

    o_ref[0] = _dot(pos_ref[...], y_ref[...]).astype(o_ref.dtype)


def _fourier(p3, rows=512):
    b = p3.shape[0]
    pos_np, chan_np = _dft_tables()
    pos = jnp.asarray(pos_np).astype(BF16)
    chan = jnp.asarray(chan_np).astype(BF16)
    return pl.pallas_call(
        _fourier_kernel,
        grid=(b, SEQ // rows),
        in_specs=[
            pl.BlockSpec((1, SEQ, FOUR_WIDTH), lambda i, j: (i, 0, 0)),
            pl.BlockSpec((FOUR_GROUP_DIM, 2 * FOUR_GROUP_DIM), lambda i, j: (0, 0)),
            pl.BlockSpec((rows, 2 * SEQ), lambda i, j: (j, 0)),
        ],
        out_specs=pl.BlockSpec((1, rows, FOUR_WIDTH), lambda i, j: (i, j, 0)),
        out_shape=jax.ShapeDtypeStruct((b, SEQ, FOUR_WIDTH), BF16),
        scratch_shapes=[pltpu.VMEM((2 * SEQ, FOUR_WIDTH), BF16)],
        compiler_params=_params("arbitrary", "arbitrary"),
        name="fourier",
    )(p3, chan, pos)


def _head_norm(o):
    mu = jnp.mean(o, axis=-1, keepdims=True)
    d = o - mu
    var = jnp.mean(d * d, axis=-1, keepdims=True)
    return d * lax.rsqrt(var + EPS)


def _silu(x):
    return x * jax.nn.sigmoid(x)


def _retention_kernel(lg_ref, q_ref, k_ref, v_ref, gf_ref, gb_ref, kc_ref, vc_ref,
                      o_ref, acc_ref, df_ref, db_ref, xif_ref, xib_ref, zf_ref, zb_ref):
    blk = RET_BLOCK
    n_blk = SEQ // blk
    h = pl.program_id(0)
    lgf, lgb, gchf, gchb = lg_ref[0, h], lg_ref[1, h], lg_ref[2, h], lg_ref[3, h]

    @pl.when(pl.program_id(1) == 0)
    def _():
        row = lax.broadcasted_iota(jnp.int32, (blk, blk), 0)
        col = lax.broadcasted_iota(jnp.int32, (blk, blk), 1)
        diff = (row - col).astype(F32)
        df_ref[...] = jnp.where(diff >= 0, jnp.exp(jnp.maximum(diff, 0.0) * lgf), 0.0)
        db_ref[...] = jnp.where(diff <= 0, jnp.exp(jnp.maximum(-diff, 0.0) * lgb), 0.0)
        pos_v = lax.broadcasted_iota(jnp.int32, (blk, RET_DV), 0).astype(F32)
        xif_ref[...] = jnp.exp((pos_v + 1.0) * lgf)
        xib_ref[...] = jnp.exp((blk - pos_v) * lgb)
        pos_k = lax.broadcasted_iota(jnp.int32, (blk, RET_DK), 0).astype(F32)
        zf_ref[...] = jnp.exp((blk - 1.0 - pos_k) * lgf)
        zb_ref[...] = jnp.exp(pos_k * lgb)

    def state_update(r, k, z_ref, v, gch):
        kz = (k.astype(F32) * z_ref[...]).astype(BF16)
        return gch * r + lax.dot_general(kz, v, (((0,), (0,)), ((), ())), preferred_element_type=F32)

    def block_out(qb, kb, vb, r, d_ref, xi_ref):
        s = lax.dot_general(qb, kb, (((1,), (1,)), ((), ())), preferred_element_type=F32)
        inner = _dot((s * d_ref[...]).astype(BF16), vb)
        cross = _dot(qb, r.astype(BF16)) * xi_ref[...]
        return inner + cross

    zero = jnp.zeros((RET_DK, RET_DV), F32)
    r_f = state_update(zero, kc_ref[0], zf_ref, vc_ref[0], gchf)
    r_b = state_update(zero, kc_ref[0], zb_ref, vc_ref[0], gchb)

    def emit(i, part, reached_first):
        rows = slice(i * blk, (i + 1) * blk)
        if reached_first:
            acc_ref[rows, :] = part
        else:
            o_ref[0, rows, :] = (acc_ref[rows, :] + part).astype(o_ref.dtype)

    for j in range(n_blk):
        i_f, i_b = j, n_blk - 1 - j
        rows_f = slice(i_f * blk, (i_f + 1) * blk)
        rows_b = slice(i_b * blk, (i_b + 1) * blk)
        qf, kf, vf = q_ref[0, rows_f, :], k_ref[0, rows_f, :], v_ref[0, rows_f, :]
        qk, kk, vk = q_ref[0, rows_b, :], k_ref[0, rows_b, :], v_ref[0, rows_b, :]
        o_f = block_out(qf, kf, vf, r_f, df_ref, xif_ref)
        o_b = block_out(qk, kk, vk, r_b, db_ref, xib_ref)
        r_f = state_update(r_f, kf, zf_ref, vf, gchf)
        r_b = state_update(r_b, kk, zb_ref, vk, gchb)
        first_visit = j < n_blk - 1 - j
        emit(i_f, gf_ref[0, rows_f, :].astype(F32) * _head_norm(o_f), first_visit)
        emit(i_b, gb_ref[0, rows_b, :].astype(F32) * _head_norm(o_b), first_visit)


def _retention(p3, kvc3, lg):
    b = p3.shape[0]
    assert CTX_LEN == RET_BLOCK and (SEQ // RET_BLOCK) % 2 == 0
    qb0, kb0 = Q_OFF // RET_DK, K_OFF // RET_DK
    vb0, gfb0, gbb0 = V_OFF // RET_DV, GF_OFF // RET_DV, GB_OFF // RET_DV
    kcv0 = (RET_HEADS * RET_DK) // RET_DV
    return pl.pallas_call(
        _retention_kernel,
        grid=(RET_HEADS, b),
        in_specs=[
            pl.BlockSpec(memory_space=pltpu.SMEM),
            pl.BlockSpec((1, SEQ, RET_DK), lambda h, i: (i, 0, qb0 + h)),
            pl.BlockSpec((1, SEQ, RET_DK), lambda h, i: (i, 0, kb0 + h)),
            pl.BlockSpec((1, SEQ, RET_DV), lambda h, i: (i, 0, vb0 + h)),
            pl.BlockSpec((1, SEQ, RET_DV), lambda h, i: (i, 0, gfb0 + h)),
            pl.BlockSpec((1, SEQ, RET_DV), lambda h, i: (i, 0, gbb0 + h)),
            pl.BlockSpec((1, CTX_LEN, RET_DK), lambda h, i: (i, 0, h)),
            pl.BlockSpec((1, CTX_LEN, RET_DV), lambda h, i: (i, 0, kcv0 + h)),
        ],
        out_specs=pl.BlockSpec((1, SEQ, RET_DV), lambda h, i: (i, 0, h)),
        out_shape=jax.ShapeDtypeStruct((b, SEQ, RET_HEADS * RET_DV), BF16),
        scratch_shapes=[
            pltpu.VMEM((SEQ, RET_DV), F32),
            pltpu.VMEM((RET_BLOCK, RET_BLOCK), F32),
            pltpu.VMEM((RET_BLOCK, RET_BLOCK), F32),
            pltpu.VMEM((RET_BLOCK, RET_DV), F32),
            pltpu.VMEM((RET_BLOCK, RET_DV), F32),
            pltpu.VMEM((RET_BLOCK, RET_DK), F32),
            pltpu.VMEM((RET_BLOCK, RET_DK), F32),
        ],
        compiler_params=_params("arbitrary", "arbitrary"),
        name="retention",
    )(lg, p3, p3, p3, p3, p3, kvc3, kvc3)


def _merge_kernel(fm_ref, ret_ref, wf_ref, wr_ref, mf_ref, mr_ref, o_ref, wfb_ref, wrb_ref):
    @pl.when(pl.program_id(1) == 0)
    def _():
        wfb_ref[...] = wf_ref[...].astype(BF16)
        wrb_ref[...] = wr_ref[...].astype(BF16)

    four = _dot(fm_ref[...], wfb_ref[...])
    ret = _dot(ret_ref[...], wrb_ref[...])
    merged = mf_ref[...].astype(F32) * four + mr_ref[...].astype(F32) * ret
    o_ref[...] = merged.astype(o_ref.dtype)


def _merge(fm2, ret2, w_four_out, w_ret_out, p2, rows=512):
    m = fm2.shape[0]
    mf0, mr0 = MF_OFF // COL_TILE, MR_OFF // COL_TILE
    return pl.pallas_call(
        _merge_kernel,
        grid=(D_MODEL // COL_TILE, m // rows),
        in_specs=[
            pl.BlockSpec((rows, FOUR_WIDTH), lambda j, i: (i, 0)),
            pl.BlockSpec((rows, D_MODEL), lambda j, i: (i, 0)),
            pl.BlockSpec((FOUR_WIDTH, COL_TILE), lambda j, i: (0, j)),
            pl.BlockSpec((D_MODEL, COL_TILE), lambda j, i: (0, j)),
            pl.BlockSpec((rows, COL_TILE), lambda j, i: (i, mf0 + j)),
            pl.BlockSpec((rows, COL_TILE), lambda j, i: (i, mr0 + j)),
        ],
        out_specs=pl.BlockSpec((rows, COL_TILE), lambda j, i: (i, j)),
        out_shape=jax.ShapeDtypeStruct((m, D_MODEL), BF16),
        scratch_shapes=[pltpu.VMEM((FOUR_WIDTH, COL_TILE), BF16), pltpu.VMEM((D_MODEL, COL_TILE), BF16)],
        compiler_params=_params("arbitrary", "arbitrary"),
        name="merge",
    )(fm2, ret2, w_four_out, w_ret_out, p2, p2)


def _out_proj_kernel(a_ref, w_ref, x_ref, g_ref, o_ref, wb_ref):
    @pl.when(pl.program_id(1) == 0)
    def _():
        wb_ref[...] = w_ref[...].astype(BF16)

    o_ref[...] = x_ref[...] + g_ref[0] * _dot(a_ref[...], wb_ref[...])


def _out_proj(merged, w_out, x2, mod3, gate_blk0, rows=512):
    m = merged.shape[0]
    return pl.pallas_call(
        _out_proj_kernel,
        grid=(D_MODEL // COL_TILE, m // rows),
        in_specs=[
            pl.BlockSpec((rows, D_MODEL), lambda j, i: (i, 0)),
            pl.BlockSpec((D_MODEL, COL_TILE), lambda j, i: (0, j)),
            pl.BlockSpec((rows, COL_TILE), lambda j, i: (i, j)),
            pl.BlockSpec((1, 1, COL_TILE), lambda j, i: (i * rows // SEQ, 0, gate_blk0 + j)),
        ],
        out_specs=pl.BlockSpec((rows, COL_TILE), lambda j, i: (i, j)),
        out_shape=jax.ShapeDtypeStruct((m, D_MODEL), F32),
        scratch_shapes=[pltpu.VMEM((D_MODEL, COL_TILE), BF16)],
        compiler_params=_params("arbitrary", "arbitrary"),
        name="out_proj",
    )(merged, w_out, x2, mod3)


def _route_kernel(x_ref, g_ref, sh_ref, sc_ref, whi_ref, wlo_ref, b_ref, r_ref):
    h = _rms_modulate(x_ref[...], g_ref[...], sh_ref[0], sc_ref[0])
    hi = h.astype(BF16)
    lo = (h - hi.astype(F32)).astype(BF16)
    logits = _dot(hi, whi_ref[...]) + _dot(lo, whi_ref[...]) + _dot(hi, wlo_ref[...]) + b_ref[...]

    lane = lax.broadcasted_iota(jnp.int32, logits.shape, 1).astype(F32)
    neg = -jnp.inf
    first = lambda hit: jnp.min(jnp.where(hit, lane, float(LANES)), axis=1, keepdims=True)
    is_grp = lane < float(N_GROUPS)
    gl = jnp.where(is_grp, logits, neg)
    gmax = jnp.max(gl, axis=1, keepdims=True)
    grp = first(gl == gmax)
    g_w = 1.0 / jnp.sum(jnp.where(is_grp, jnp.exp(logits - gmax), 0.0), axis=1, keepdims=True)
    lo_lane = float(N_GROUPS) + grp * float(EXPERTS_PER_GROUP)
    el = jnp.where(lane >= lo_lane, jnp.where(lane < lo_lane + float(EXPERTS_PER_GROUP), logits, neg), neg)
    v1 = jnp.max(el, axis=1, keepdims=True)
    i1 = first(el == v1)
    el2 = jnp.where(lane == i1, neg, el)
    v2 = jnp.max(el2, axis=1, keepdims=True)
    i2 = first(el2 == v2)
    e = jnp.exp(v2 - v1)
    w1 = g_w / (1.0 + e)
    w2 = g_w * e / (1.0 + e)
    out = jnp.where(lane == 0.0, i1 - float(N_GROUPS),
                    jnp.where(lane == 1.0, i2 - float(N_GROUPS),
                              jnp.where(lane == 2.0, w1, jnp.where(lane == 3.0, w2, 0.0))))
    r_ref[...] = out


def _route(x1, g, mod3, shift_blk, scale_blk, w_hi, w_lo, bias, rows=512):
    m = x1.shape[0]
    return pl.pallas_call(
        _route_kernel,
        grid=(m // rows,),
        in_specs=[
            pl.BlockSpec((rows, D_MODEL), lambda i: (i, 0)),
            pl.BlockSpec((1, D_MODEL), lambda i: (0, 0)),
            pl.BlockSpec((1, 1, D_MODEL), lambda i: (i * rows // SEQ, 0, shift_blk)),
            pl.BlockSpec((1, 1, D_MODEL), lambda i: (i * rows // SEQ, 0, scale_blk)),
            pl.BlockSpec((D_MODEL, LANES), lambda i: (0, 0)),
            pl.BlockSpec((D_MODEL, LANES), lambda i: (0, 0)),
            pl.BlockSpec((1, LANES), lambda i: (0, 0)),
        ],
        out_specs=pl.BlockSpec((rows, LANES), lambda i: (i, 0)),
        out_shape=jax.ShapeDtypeStruct((m, LANES), F32),
        compiler_params=_params("arbitrary"),
        name="route",
    )(x1, g, mod3, mod3, w_hi, w_lo, bias)


def _dispatch_plan(route):
    rows = EXPERT_ROWS
    flat = route[:, :2].astype(jnp.int32).reshape(-1)
    n_tiles = flat.shape[0] // rows
    experts = jnp.arange(N_EXPERTS, dtype=jnp.int32)
    onehot = (flat[:, None] == experts[None, :]).astype(jnp.int32)
    csum = jnp.cumsum(onehot, axis=0)
    rank = jnp.sum(csum * onehot, axis=1) - 1
    counts = csum[-1]
    cend = jnp.cumsum(counts)
    cstart = cend - counts
    pos = jnp.sum(cstart[None, :] * onehot, axis=1) + rank
    first_tile = cstart // rows
    items = jnp.where(counts > 0, (cend - 1) // rows - first_tile + 1, 0)
    wend = jnp.cumsum(items)
    wstart = wend - items
    n_work = wend[-1]
    w = jnp.arange(N_WORK_ITEMS, dtype=jnp.int32)
    valid = w < n_work
    last_expert = jnp.max(jnp.where(counts > 0, experts, 0))
    e_w = jnp.minimum(jnp.sum((wend[None, :] <= w[:, None]).astype(jnp.int32), axis=1), N_EXPERTS - 1)
    e_w = jnp.where(valid, e_w, last_expert)
    sel = (e_w[:, None] == experts[None, :]).astype(jnp.int32)
    pick = lambda v: jnp.sum(sel * v[None, :], axis=1)
    tile_w = jnp.where(valid, pick(first_tile) + w - pick(wstart), n_tiles - 1)
    lo_w = jnp.where(valid, jnp.clip(pick(cstart) - tile_w * rows, 0, rows), 0)
    hi_w = jnp.where(valid, jnp.clip(pick(cend) - tile_w * rows, 0, rows), 0)
    prev = lambda v: jnp.concatenate([jnp.full((1,), -1, jnp.int32), v[:-1]])
    new_expert = (e_w != prev(e_w)).astype(jnp.int32)
    new_tile = (tile_w != prev(tile_w)).astype(jnp.int32)
    later = jnp.logical_and(counts[None, :] > 0, experts[None, :] > experts[:, None])
    next_of = jnp.min(jnp.where(later, experts[None, :], N_EXPERTS), axis=1)
    next_of = jnp.where(next_of < N_EXPERTS, next_of, -1)
    buf_w = (jnp.cumsum(new_expert) - 1) % 2
    plan = jnp.stack([tile_w, e_w, new_expert, new_tile, lo_w, hi_w,
                      jnp.broadcast_to(n_work, w.shape), pick(next_of), buf_w]).astype(jnp.int32)
    return pos, plan


def _dispatch_kernel(pos_ref, x_ref, g_ref, sh_ref, sc_ref, xs_hbm, hbuf, sem):
    rows = DISPATCH_ROWS
    i = pl.program_id(0)
    n = pl.num_programs(0)
    slot = i % 2

    def row_copy(s, r, dst_row):
        return pltpu.make_async_copy(hbuf.at[s, pl.ds(r, 1), :], xs_hbm.at[pl.ds(dst_row, 1), :], sem.at[s])

    def drain(s):
        def body(r, c):
            row_copy(s, 0, 0).wait()
            return c
        lax.fori_loop(0, 2 * rows, body, 0, unroll=8)

    @pl.when(i >= 2)
    def _():
        drain(slot)

    hbuf[slot] = _rms_modulate(x_ref[...], g_ref[...], sh_ref[0], sc_ref[0])

    def issue(r, c):
        row_copy(slot, r, pos_ref[0, 0, r]).start()
        row_copy(slot, r, pos_ref[0, 0, rows + r]).start()
        return c
    lax.fori_loop(0, rows, issue, 0, unroll=8)

    @pl.when(i == n - 1)
    def _():
        drain(slot)
        drain(1 - slot)


def _dispatch(x1, g, mod3, shift_blk, scale_blk, pos3, n_sorted):
    rows = DISPATCH_ROWS
    m = x1.shape[0]
    assert m // rows >= 2
    return pl.pallas_call(
        _dispatch_kernel,
        grid=(m // rows,),
        in_specs=[
            pl.BlockSpec((1, 1, 2 * rows), lambda i: (i, 0, 0), memory_space=pltpu.SMEM),
            pl.BlockSpec((rows, D_MODEL), lambda i: (i, 0)),
            pl.BlockSpec((1, D_MODEL), lambda i: (0, 0)),
            pl.BlockSpec((1, 1, D_MODEL), lambda i: (i * rows // SEQ, 0, shift_blk)),
            pl.BlockSpec((1, 1, D_MODEL), lambda i: (i * rows // SEQ, 0, scale_blk)),
        ],
        out_specs=pl.BlockSpec(memory_space=pl.ANY),
        out_shape=jax.ShapeDtypeStruct((n_sorted, D_MODEL), F32),
        scratch_shapes=[pltpu.VMEM((2, rows, D_MODEL), F32), pltpu.SemaphoreType.DMA((2,))],
        compiler_params=_params("arbitrary"),
        name="dispatch",
    )(pos3, x1, g, mod3, mod3)


def _experts_kernel(plan_ref, x_ref, wg_hbm, wu_hbm, wd_hbm, y_ref,
                    wg_buf, wu_buf, wd_buf, sem, wgb_ref, wub_ref, wdb_ref):
    w = pl.program_id(0)

    def weight_copies(e, s):
        return (pltpu.make_async_copy(wg_hbm.at[e], wg_buf.at[s], sem.at[s, 0]),
                pltpu.make_async_copy(wu_hbm.at[e], wu_buf.at[s], sem.at[s, 1]),
                pltpu.make_async_copy(wd_hbm.at[e], wd_buf.at[s], sem.at[s, 2]))

    @pl.when(w == 0)
    def _():
        for cp in weight_copies(plan_ref[1, 0], 0):
            cp.start()

    @pl.when(w < plan_ref[6, w])
    def _():
        @pl.when(plan_ref[2, w] == 1)
        def _():
            s = plan_ref[8, w]
            for cp in weight_copies(plan_ref[1, w], s):
                cp.wait()
            wgb_ref[...] = wg_buf[s].astype(BF16)
            wub_ref[...] = wu_buf[s].astype(BF16)
            wdb_ref[...] = wd_buf[s].astype(BF16)

            @pl.when(plan_ref[7, w] >= 0)
            def _():
                for cp in weight_copies(plan_ref[7, w], 1 - s):
                    cp.start()

        x = x_ref[...].astype(BF16)
        hg = _dot(x, wgb_ref[...])
        hu = _dot(x, wub_ref[...])
        hid = _silu(hg) * hu
        y = _dot(hid.astype(BF16), wdb_ref[...])
        row = lax.broadcasted_iota(jnp.int32, (EXPERT_ROWS, 1), 0)
        mine = jnp.logical_and(row >= plan_ref[4, w], row < plan_ref[5, w])

        @pl.when(plan_ref[3, w] == 1)
        def _():
            y_ref[...] = jnp.where(mine, y, 0.0)

        @pl.when(plan_ref[3, w] == 0)
        def _():
            y_ref[...] = jnp.where(mine, y, y_ref[...])


def _experts(xs, plan, w_gate, w_up, w_down):
    rows = EXPERT_ROWS
    grid_spec = pltpu.PrefetchScalarGridSpec(
        num_scalar_prefetch=1,
        grid=(N_WORK_ITEMS,),
        in_specs=[
            pl.BlockSpec((rows, D_MODEL), lambda w, p: (p[0, w], 0)),
            pl.BlockSpec(memory_space=pl.ANY),
            pl.BlockSpec(memory_space=pl.ANY),
            pl.BlockSpec(memory_space=pl.ANY),
        ],
        out_specs=pl.BlockSpec((rows, D_MODEL), lambda w, p: (p[0, w], 0)),
        scratch_shapes=[
            pltpu.VMEM((2, D_MODEL, EXPERT_FF), F32),
            pltpu.VMEM((2, D_MODEL, EXPERT_FF), F32),
            pltpu.VMEM((2, EXPERT_FF, D_MODEL), F32),
            pltpu.SemaphoreType.DMA((2, 3)),
            pltpu.VMEM((D_MODEL, EXPERT_FF), BF16),
            pltpu.VMEM((D_MODEL, EXPERT_FF), BF16),
            pltpu.VMEM((EXPERT_FF, D_MODEL), BF16),
        ],
    )
    return pl.pallas_call(
        _experts_kernel,
        grid_spec=grid_spec,
        out_shape=jax.ShapeDtypeStruct(xs.shape, F32),
        compiler_params=_params("arbitrary"),
        name="experts",
    )(plan, xs, w_gate, w_up, w_down)


def _final_kernel(pos_ref, posn_ref, x_ref, r_ref, g2_ref, fg_ref, y_hbm, o_ref, ybuf, sem):
    rows = FINAL_ROWS
    i = pl.program_id(0)
    n = pl.num_programs(0)
    slot = i % 2

    def row_copy(src_row, s, r):
        return pltpu.make_async_copy(y_hbm.at[pl.ds(src_row, 1), :], ybuf.at[s, pl.ds(r, 1), :], sem.at[s])

    def issue(idx_ref, s):
        def body(r, c):
            row_copy(idx_ref[0, 0, r], s, r).start()
            return c
        lax.fori_loop(0, 2 * rows, body, 0, unroll=8)

    @pl.when(i == 0)
    def _():
        issue(pos_ref, 0)

    @pl.when(i + 1 < n)
    def _():
        issue(posn_ref, 1 - slot)

    def wait_body(r, c):
        row_copy(0, slot, r).wait()
        return c
    lax.fori_loop(0, 2 * rows, wait_body, 0, unroll=8)

    route = r_ref[...]
    w1 = route[:, 2:3]
    w2 = route[:, 3:4]
    moe = w1 * ybuf[slot, 0:rows, :] + w2 * ybuf[slot, rows:2 * rows, :]
    x = x_ref[...] + g2_ref[0] * moe
    o_ref[...] = x * lax.rsqrt(jnp.mean(x * x, axis=-1, keepdims=True) + EPS) * fg_ref[...]


def _final(x1, route, mod3, gate_blk, final_g, y, pos3):
    rows = FINAL_ROWS
    m = x1.shape[0]
    nt = m // rows
    return pl.pallas_call(
        _final_kernel,
        grid=(nt,),
        in_specs=[
            pl.BlockSpec((1, 1, 2 * rows), lambda i: (i, 0, 0), memory_space=pltpu.SMEM),
            pl.BlockSpec((1, 1, 2 * rows), lambda i: (jnp.minimum(i + 1, nt - 1), 0, 0), memory_space=pltpu.SMEM),
            pl.BlockSpec((rows, D_MODEL), lambda i: (i, 0)),
            pl.BlockSpec((rows, LANES), lambda i: (i, 0)),
            pl.BlockSpec((1, 1, D_MODEL), lambda i: (i * rows // SEQ, 0, gate_blk)),
            pl.BlockSpec((1, D_MODEL), lambda i: (0, 0)),
            pl.BlockSpec(memory_space=pl.ANY),
        ],
        out_specs=pl.BlockSpec((rows, D_MODEL), lambda i: (i, 0)),
        out_shape=jax.ShapeDtypeStruct((m, D_MODEL), F32),
        scratch_shapes=[pltpu.VMEM((2, 2 * rows, D_MODEL), F32), pltpu.SemaphoreType.DMA((2,))],
        compiler_params=_params("arbitrary"),
        name="final",
    )(pos3, pos3, x1, route, mod3, final_g, y)


def _rope_tables():
    rows = SEQ // GRID_W
    row = jnp.repeat(jnp.arange(rows, dtype=F32), GRID_W)
    col = jnp.tile(jnp.arange(GRID_W, dtype=F32), rows)
    n_freq = RET_DK // 4
    inv = ROPE_BASE ** (-jnp.arange(n_freq, dtype=F32) / n_freq)
    ang = jnp.concatenate([row[:, None] * inv, col[:, None] * inv], axis=-1)
    cos, sin = jnp.cos(ang), jnp.sin(ang)
    return jnp.concatenate([cos, cos], axis=-1), jnp.concatenate([-sin, sin], axis=-1)


def kernel(x, c, ctx, c_ctx, w_mod, b_mod, norm1_g, norm2_g, w_in, w_four_out, w_ret_out, w_out,
           ret_decay_f, ret_decay_b, w_group_router, b_group_router, w_expert_router, b_expert_router,
           w_gate, w_up, w_down, final_norm_g):
    b, n, d = x.shape
    assert (n, d) == (SEQ, D_MODEL) and ctx.shape[1] == CTX_LEN and w_mod.shape[0] == 1
    t = b * n

    c8 = jnp.zeros((8, d), F32).at[:b].set(c).at[b].set(c_ctx)
    mod3 = _modulation(c8, w_mod[0], b_mod).reshape(8, 1, N_MOD * d)

    hx = _prenorm(x, norm1_g, mod3, lambda i: i, 0, 1, rows=512)
    hc = _prenorm(ctx, norm1_g, mod3, lambda i: b, 0, 1, rows=CTX_LEN)

    cos2, sin2 = _rope_tables()
    p2 = _in_proj(hx.reshape(t, d), w_in[0], cos2, sin2)
    p3 = p2.reshape(b, n, IN_WIDTH)
    kvc = _project(hc.reshape(b * CTX_LEN, d), w_in[0], K_OFF // COL_TILE, (GF_OFF - K_OFF) // COL_TILE,
                   b * CTX_LEN, "ctx_proj")
    kvc3 = kvc.reshape(b, CTX_LEN, GF_OFF - K_OFF)

    fm = _fourier(p3)

    lg_f = jax.nn.log_sigmoid(ret_decay_f[0].astype(F32))
    lg_b = jax.nn.log_sigmoid(ret_decay_b[0].astype(F32))
    lg = jnp.stack([lg_f, lg_b, jnp.exp(RET_BLOCK * lg_f), jnp.exp(RET_BLOCK * lg_b)])
    ret = _retention(p3, kvc3, lg)

    merged = _merge(fm.reshape(t, FOUR_WIDTH), ret.reshape(t, d), w_four_out[0], w_ret_out[0], p2)
    x1 = _out_proj(merged, w_out[0], x.reshape(t, d), mod3, 2 * d // COL_TILE)

    w_router = jnp.concatenate(
        [w_group_router[0], w_expert_router[0].transpose(1, 0, 2).reshape(d, N_EXPERTS)], axis=1).astype(F32)
    w_router = jnp.pad(w_router, ((0, 0), (0, LANES - w_router.shape[1])))
    w_hi = w_router.astype(BF16)
    w_lo = (w_router - w_hi.astype(F32)).astype(BF16)
    bias = jnp.pad(jnp.concatenate([b_group_router[0], b_expert_router[0].reshape(-1)]).astype(F32),
                   (0, LANES - N_GROUPS - N_EXPERTS)).reshape(1, LANES)
    route = _route(x1, norm2_g, mod3, 3, 4, w_hi, w_lo, bias)

    pos, plan = _dispatch_plan(route)
    assert DISPATCH_ROWS == FINAL_ROWS
    pos3 = pos.reshape(t // FINAL_ROWS, FINAL_ROWS, 2).transpose(0, 2, 1).reshape(t // FINAL_ROWS, 1, 2 * FINAL_ROWS)
    xs = _dispatch(x1, norm2_g, mod3, 3, 4, pos3, 2 * t)
    y = _experts(xs, plan, w_gate[0], w_up[0], w_down[0])
    out = _final(x1, route, mod3, 5, final_norm_g.reshape(1, d), y, pos3)
    return out.reshape(b, n, d)
```

```python
import functools

import jax
import jax.numpy as jnp
import numpy as np
from jax import lax
from jax.experimental import pallas as pl
from jax.experimental.pallas import tpu as pltpu

F32 = jnp.float32
BF16 = jnp.bfloat16

D_MODEL = 2048
SEQ = 2048
CTX_LEN = 256
GRID_W = 64
N_MOD = 6
FOUR_GROUPS = 8
FOUR_GROUP_DIM = 128
FOUR_WIDTH = FOUR_GROUPS * FOUR_GROUP_DIM
RET_HEADS = 8
RET_DK = 128
RET_DV = 256
ROPE_BASE = 10000.0
N_GROUPS = 4
EXPERTS_PER_GROUP = 8
N_EXPERTS = N_GROUPS * EXPERTS_PER_GROUP
EXPERT_FF = 512
EPS = 1e-6
Q_OFF = FOUR_WIDTH
K_OFF = Q_OFF + RET_HEADS * RET_DK
V_OFF = K_OFF + RET_HEADS * RET_DK
GF_OFF = V_OFF + RET_HEADS * RET_DV
GB_OFF = GF_OFF + RET_HEADS * RET_DV
MF_OFF = GB_OFF + RET_HEADS * RET_DV
MR_OFF = MF_OFF + D_MODEL
IN_WIDTH = MR_OFF + D_MODEL

V7X_VMEM_LIMIT_BYTES = 56 * 1024 * 1024
LANES = 128
RET_BLOCK = 256
COL_TILE = 1024
IN_COL_TILE = 1024
IN_ROWS = 1024
EXPERT_ROWS = 256
DISPATCH_ROWS = 256
FINAL_ROWS = 256
N_WORK_ITEMS = 2 * 4 * SEQ // EXPERT_ROWS + N_EXPERTS


def _params(*sem):
    return pltpu.CompilerParams(dimension_semantics=sem, vmem_limit_bytes=V7X_VMEM_LIMIT_BYTES)


def _dot(a, b):
    return jnp.dot(a, b, preferred_element_type=F32)


def _mod_kernel(c_ref, w_ref, b_ref, o_ref):
    c = c_ref[...]
    sc = c * jax.nn.sigmoid(c)
    o_ref[...] = _dot(sc.astype(BF16), w_ref[...].astype(BF16)) + b_ref[...]


def _modulation(c8, w_mod, b_mod):
    n_out = w_mod.shape[1]
    return pl.pallas_call(
        _mod_kernel,
        grid=(n_out // COL_TILE,),
        in_specs=[
            pl.BlockSpec((8, D_MODEL), lambda j: (0, 0)),
            pl.BlockSpec((D_MODEL, COL_TILE), lambda j: (0, j)),
            pl.BlockSpec((1, COL_TILE), lambda j: (0, j)),
        ],
        out_specs=pl.BlockSpec((8, COL_TILE), lambda j: (0, j)),
        out_shape=jax.ShapeDtypeStruct((8, n_out), F32),
        compiler_params=_params("arbitrary"),
        name="mod",
    )(c8, w_mod, b_mod)


def _rms_modulate(x, g, shift, scale):
    y = x * lax.rsqrt(jnp.mean(x * x, axis=-1, keepdims=True) + EPS) * g
    return y * (1.0 + scale) + shift


def _prenorm_kernel(x_ref, g_ref, sh_ref, sc_ref, o_ref):
    o_ref[0] = _rms_modulate(x_ref[0], g_ref[...], sh_ref[0], sc_ref[0]).astype(o_ref.dtype)


def _prenorm(x, g, mod3, row_of_batch, shift_blk, scale_blk, rows):
    b, n, d = x.shape
    return pl.pallas_call(
        _prenorm_kernel,
        grid=(b, n // rows),
        in_specs=[
            pl.BlockSpec((1, rows, d), lambda i, j: (i, j, 0)),
            pl.BlockSpec((1, d), lambda i, j: (0, 0)),
            pl.BlockSpec((1, 1, d), lambda i, j: (row_of_batch(i), 0, shift_blk)),
            pl.BlockSpec((1, 1, d), lambda i, j: (row_of_batch(i), 0, scale_blk)),
        ],
        out_specs=pl.BlockSpec((1, rows, d), lambda i, j: (i, j, 0)),
        out_shape=jax.ShapeDtypeStruct((b, n, d), BF16),
        compiler_params=_params("arbitrary", "arbitrary"),
        name="prenorm",
    )(x, g, mod3, mod3)


def _proj_kernel(a_ref, w_ref, o_ref, wb_ref):
    @pl.when(pl.program_id(1) == 0)
    def _():
        wb_ref[...] = w_ref[...].astype(BF16)

    o_ref[...] = _dot(a_ref[...], wb_ref[...]).astype(o_ref.dtype)


def _project(a, w, col_tile0, n_col_tiles, rows, name):
    m, k = a.shape
    return pl.pallas_call(
        _proj_kernel,
        grid=(n_col_tiles, m // rows),
        in_specs=[
            pl.BlockSpec((rows, k), lambda j, i: (i, 0)),
            pl.BlockSpec((k, COL_TILE), lambda j, i: (0, j + col_tile0)),
        ],
        out_specs=pl.BlockSpec((rows, COL_TILE), lambda j, i: (i, j)),
        out_shape=jax.ShapeDtypeStruct((m, n_col_tiles * COL_TILE), BF16),
        scratch_shapes=[pltpu.VMEM((k, COL_TILE), BF16)],
        compiler_params=_params("arbitrary", "arbitrary"),
        name=name,
    )(a, w)


def _rope(t, cos2, sin2):
    return t * cos2 + pltpu.roll(t, RET_DK // 2, axis=1) * sin2


def _in_proj_kernel(a_ref, w_ref, cos_ref, sin_ref, o_ref, wb_ref):
    j = pl.program_id(0)

    @pl.when(pl.program_id(1) == 0)
    def _():
        wb_ref[...] = w_ref[...].astype(BF16)

    q_tile, k_tile, v_tile = Q_OFF // IN_COL_TILE, K_OFF // IN_COL_TILE, V_OFF // IN_COL_TILE
    gate_tile, merge_tile = GF_OFF // IN_COL_TILE, MF_OFF // IN_COL_TILE
    is_qk = jnp.logical_and(j >= q_tile, j < v_tile)
    is_gate = jnp.logical_and(j >= gate_tile, j < merge_tile)

    @pl.when(jnp.logical_not(jnp.logical_or(is_qk, is_gate)))
    def _():
        o_ref[...] = _dot(a_ref[...], wb_ref[...]).astype(o_ref.dtype)

    @pl.when(is_qk)
    def _():
        acc = _dot(a_ref[...], wb_ref[...])
        scale = jnp.where(j < k_tile, RET_DK ** -0.5, 1.0).astype(F32)
        cos2, sin2 = cos_ref[...], sin_ref[...]
        for h in range(IN_COL_TILE // RET_DK):
            cols = slice(h * RET_DK, (h + 1) * RET_DK)
            o_ref[:, cols] = (_rope(acc[:, cols], cos2, sin2) * scale).astype(o_ref.dtype)

    @pl.when(is_gate)
    def _():
        o_ref[...] = _silu(_dot(a_ref[...], wb_ref[...])).astype(o_ref.dtype)


def _in_proj(a, w, cos2, sin2, rows=IN_ROWS):
    m, k = a.shape
    n_col_tiles = IN_WIDTH // IN_COL_TILE
    pos_blocks = SEQ // rows
    return pl.pallas_call(
        _in_proj_kernel,
        grid=(n_col_tiles, m // rows),
        in_specs=[
            pl.BlockSpec((rows, k), lambda j, i: (i, 0)),
            pl.BlockSpec((k, IN_COL_TILE), lambda j, i: (0, j)),
            pl.BlockSpec((rows, RET_DK), lambda j, i: (i % pos_blocks, 0)),
            pl.BlockSpec((rows, RET_DK), lambda j, i: (i % pos_blocks, 0)),
        ],
        out_specs=pl.BlockSpec((rows, IN_COL_TILE), lambda j, i: (i, j)),
        out_shape=jax.ShapeDtypeStruct((m, IN_WIDTH), BF16),
        scratch_shapes=[pltpu.VMEM((k, IN_COL_TILE), BF16)],
        compiler_params=_params("arbitrary", "arbitrary"),
        name="in_proj",
    )(a, w, cos2, sin2)


def _dft_tables():
    n = np.arange(SEQ, dtype=np.int64)
    ang_n = 2.0 * np.pi * ((n[:, None] * n[None, :]) % SEQ) / SEQ
    pos = np.concatenate([np.cos(ang_n), -np.sin(ang_n)], axis=1).astype(np.float32)
    c = np.arange(FOUR_GROUP_DIM, dtype=np.int64)
    ang_c = 2.0 * np.pi * ((c[:, None] * c[None, :]) % FOUR_GROUP_DIM) / FOUR_GROUP_DIM
    norm = 1.0 / np.sqrt(float(SEQ * FOUR_GROUP_DIM))
    chan = (np.concatenate([np.cos(ang_c), np.sin(ang_c)], axis=1) * norm).astype(np.float32)
    return pos, chan


def _fourier_kernel(u_ref, chan_ref, pos_ref, o_ref, y_ref):
    @pl.when(pl.program_id(1) == 0)
    def _():
        chan = chan_ref[...]
        for g in range(FOUR_GROUPS):
            cols = slice(g * FOUR_GROUP_DIM, (g + 1) * FOUR_GROUP_DIM)
            y = _dot(u_ref[0, :, cols], chan)
            y_ref[0:SEQ, cols] = y[:, :FOUR_GROUP_DIM].astype(BF16)
            y_ref[SEQ:2 * SEQ, cols] = y[:, FOUR_GROUP_DIM:].astype(BF16)

    o_ref[0] = _dot(pos_ref[...], y_ref[...]).astype(o_ref.dtype)


def _fourier(p3, rows=1024):
    b = p3.shape[0]
    pos_np, chan_np = _dft_tables()
    pos = jnp.asarray(pos_np).astype(BF16)
    chan = jnp.asarray(chan_np).astype(BF16)
    return pl.pallas_call(
        _fourier_kernel,
        grid=(b, SEQ // rows),
        in_specs=[
            pl.BlockSpec((1, SEQ, FOUR_WIDTH), lambda i, j: (i, 0, 0)),
            pl.BlockSpec((FOUR_GROUP_DIM, 2 * FOUR_GROUP_DIM), lambda i, j: (0, 0)),
            pl.BlockSpec((rows, 2 * SEQ), lambda i, j: (j, 0)),
        ],
        out_specs=pl.BlockSpec((1, rows, FOUR_WIDTH), lambda i, j: (i, j, 0)),
        out_shape=jax.ShapeDtypeStruct((b, SEQ, FOUR_WIDTH), BF16),
        scratch_shapes=[pltpu.VMEM((2 * SEQ, FOUR_WIDTH), BF16)],
        compiler_params=_params("arbitrary", "arbitrary"),
        name="fourier",
    )(p3, chan, pos)


def _head_norm(o):
    mu = jnp.mean(o, axis=-1, keepdims=True)
    d = o - mu
    var = jnp.mean(d * d, axis=-1, keepdims=True)
    return d * lax.rsqrt(var + EPS)


def _silu(x):
    return x * jax.nn.sigmoid(x)


def _retention_kernel(lg_ref, q_ref, k_ref, v_ref, gf_ref, gb_ref, kc_ref, vc_ref,
                      o_ref, acc_ref, df_ref, db_ref, xif_ref, xib_ref, zf_ref, zb_ref):
    blk = RET_BLOCK
    n_blk = SEQ // blk
    h = pl.program_id(0)
    lgf, lgb, gchf, gchb = lg_ref[0, h], lg_ref[1, h], lg_ref[2, h], lg_ref[3, h]

    @pl.when(pl.program_id(1) == 0)
    def _():
        row = lax.broadcasted_iota(jnp.int32, (blk, blk), 0)
        col = lax.broadcasted_iota(jnp.int32, (blk, blk), 1)
        diff = (row - col).astype(F32)
        df_ref[...] = jnp.where(diff >= 0, jnp.exp(jnp.maximum(diff, 0.0) * lgf), 0.0)
        db_ref[...] = jnp.where(diff <= 0, jnp.exp(jnp.maximum(-diff, 0.0) * lgb), 0.0)
        pos_v = lax.broadcasted_iota(jnp.int32, (blk, RET_DV), 0).astype(F32)
        xif_ref[...] = jnp.exp((pos_v + 1.0) * lgf)
        xib_ref[...] = jnp.exp((blk - pos_v) * lgb)
        pos_k = lax.broadcasted_iota(jnp.int32, (blk, RET_DK), 0).astype(F32)
        zf_ref[...] = jnp.exp((blk - 1.0 - pos_k) * lgf)
        zb_ref[...] = jnp.exp(pos_k * lgb)

    def state_update(r, k, z_ref, v, gch):
        kz = (k.astype(F32) * z_ref[...]).astype(BF16)
        return gch * r + lax.dot_general(kz, v, (((0,), (0,)), ((), ())), preferred_element_type=F32)

    def block_out(qb, kb, vb, r, d_ref, xi_ref):
        s = lax.dot_general(qb, kb, (((1,), (1,)), ((), ())), preferred_element_type=F32)
        inner = _dot((s * d_ref[...]).astype(BF16), vb)
        cross = _dot(qb, r.astype(BF16)) * xi_ref[...]
        return inner + cross

    zero = jnp.zeros((RET_DK, RET_DV), F32)
    r_f = state_update(zero, kc_ref[0], zf_ref, vc_ref[0], gchf)
    r_b = state_update(zero, kc_ref[0], zb_ref, vc_ref[0], gchb)

    def emit(i, part, reached_first):
        rows = slice(i * blk, (i + 1) * blk)
        if reached_first:
            acc_ref[rows, :] = part
        else:
            o_ref[0, rows, :] = (acc_ref[rows, :] + part).astype(o_ref.dtype)

    for j in range(n_blk):
        i_f, i_b = j, n_blk - 1 - j
        rows_f = slice(i_f * blk, (i_f + 1) * blk)
        rows_b = slice(i_b * blk, (i_b + 1) * blk)
        qf, kf, vf = q_ref[0, rows_f, :], k_ref[0, rows_f, :], v_ref[0, rows_f, :]
        qk, kk, vk = q_ref[0, rows_b, :], k_ref[0, rows_b, :], v_ref[0, rows_b, :]
        o_f = block_out(qf, kf, vf, r_f, df_ref, xif_ref)
        o_b = block_out(qk, kk, vk, r_b, db_ref, xib_ref)
        r_f = state_update(r_f, kf, zf_ref, vf, gchf)
        r_b = state_update(r_b, kk, zb_ref, vk, gchb)
        first_visit = j < n_blk - 1 - j
        emit(i_f, gf_ref[0, rows_f, :].astype(F32) * _head_norm(o_f), first_visit)
        emit(i_b, gb_ref[0, rows_b, :].astype(F32) * _head_norm(o_b), first_visit)


def _retention(p3, kvc3, lg):
    b = p3.shape[0]
    assert CTX_LEN == RET_BLOCK and (SEQ // RET_BLOCK) % 2 == 0
    qb0, kb0 = Q_OFF // RET_DK, K_OFF // RET_DK
    vb0, gfb0, gbb0 = V_OFF // RET_DV, GF_OFF // RET_DV, GB_OFF // RET_DV
    kcv0 = (RET_HEADS * RET_DK) // RET_DV
    return pl.pallas_call(
        _retention_kernel,
        grid=(RET_HEADS, b),
        in_specs=[
            pl.BlockSpec(memory_space=pltpu.SMEM),
            pl.BlockSpec((1, SEQ, RET_DK), lambda h, i: (i, 0, qb0 + h)),
            pl.BlockSpec((1, SEQ, RET_DK), lambda h, i: (i, 0, kb0 + h)),
            pl.BlockSpec((1, SEQ, RET_DV), lambda h, i: (i, 0, vb0 + h)),
            pl.BlockSpec((1, SEQ, RET_DV), lambda h, i: (i, 0, gfb0 + h)),
            pl.BlockSpec((1, SEQ, RET_DV), lambda h, i: (i, 0, gbb0 + h)),
            pl.BlockSpec((1, CTX_LEN, RET_DK), lambda h, i: (i, 0, h)),
            pl.BlockSpec((1, CTX_LEN, RET_DV), lambda h, i: (i, 0, kcv0 + h)),
        ],
        out_specs=pl.BlockSpec((1, SEQ, RET_DV), lambda h, i: (i, 0, h)),
        out_shape=jax.ShapeDtypeStruct((b, SEQ, RET_HEADS * RET_DV), BF16),
        scratch_shapes=[
            pltpu.VMEM((SEQ, RET_DV), F32),
            pltpu.VMEM((RET_BLOCK, RET_BLOCK), F32),
            pltpu.VMEM((RET_BLOCK, RET_BLOCK), F32),
            pltpu.VMEM((RET_BLOCK, RET_DV), F32),
            pltpu.VMEM((RET_BLOCK, RET_DV), F32),
            pltpu.VMEM((RET_BLOCK, RET_DK), F32),
            pltpu.VMEM((RET_BLOCK, RET_DK), F32),
        ],
        compiler_params=_params("arbitrary", "arbitrary"),
        name="retention",
    )(lg, p3, p3, p3, p3, p3, kvc3, kvc3)


def _merge_kernel(fm_ref, ret_ref, wf_ref, wr_ref, mf_ref, mr_ref, o_ref, wfb_ref, wrb_ref):
    @pl.when(pl.program_id(1) == 0)
    def _():
        wfb_ref[...] = wf_ref[...].astype(BF16)
        wrb_ref[...] = wr_ref[...].astype(BF16)

    four = _dot(fm_ref[...], wfb_ref[...])
    ret = _dot(ret_ref[...], wrb_ref[...])
    merged = jax.nn.sigmoid(mf_ref[...].astype(F32)) * four + jax.nn.sigmoid(mr_ref[...].astype(F32)) * ret
    o_ref[...] = merged.astype(o_ref.dtype)


def _merge(fm2, ret2, w_four_out, w_ret_out, p2, rows=512):
    m = fm2.shape[0]
    mf0, mr0 = MF_OFF // COL_TILE, MR_OFF // COL_TILE
    return pl.pallas_call(
        _merge_kernel,
        grid=(D_MODEL // COL_TILE, m // rows),
        in_specs=[
            pl.BlockSpec((rows, FOUR_WIDTH), lambda j, i: (i, 0)),
            pl.BlockSpec((rows, D_MODEL), lambda j, i: (i, 0)),
            pl.BlockSpec((FOUR_WIDTH, COL_TILE), lambda j, i: (0, j)),
            pl.BlockSpec((D_MODEL, COL_TILE), lambda j, i: (0, j)),
            pl.BlockSpec((rows, COL_TILE), lambda j, i: (i, mf0 + j)),
            pl.BlockSpec((rows, COL_TILE), lambda j, i: (i, mr0 + j)),
        ],
        out_specs=pl.BlockSpec((rows, COL_TILE), lambda j, i: (i, j)),
        out_shape=jax.ShapeDtypeStruct((m, D_MODEL), BF16),
        scratch_shapes=[pltpu.VMEM((FOUR_WIDTH, COL_TILE), BF16), pltpu.VMEM((D_MODEL, COL_TILE), BF16)],
        compiler_params=_params("arbitrary", "arbitrary"),
        name="merge",
    )(fm2, ret2, w_four_out, w_ret_out, p2, p2)


def _out_proj_kernel(a_ref, w_ref, x_ref, g_ref, o_ref, wb_ref):
    @pl.when(pl.program_id(1) == 0)
    def _():
        wb_ref[...] = w_ref[...].astype(BF16)

    o_ref[...] = x_ref[...] + g_ref[0] * _dot(a_ref[...], wb_ref[...])


def _out_proj(merged, w_out, x2, mod3, gate_blk0, rows=1024):
    m = merged.shape[0]
    return pl.pallas_call(
        _out_proj_kernel,
        grid=(D_MODEL // COL_TILE, m // rows),
        in_specs=[
            pl.BlockSpec((rows, D_MODEL), lambda j, i: (i, 0)),
            pl.BlockSpec((D_MODEL, COL_TILE), lambda j, i: (0, j)),
            pl.BlockSpec((rows, COL_TILE), lambda j, i: (i, j)),
            pl.BlockSpec((1, 1, COL_TILE), lambda j, i: (i * rows // SEQ, 0, gate_blk0 + j)),
        ],
        out_specs=pl.BlockSpec((rows, COL_TILE), lambda j, i: (i, j)),
        out_shape=jax.ShapeDtypeStruct((m, D_MODEL), F32),
        scratch_shapes=[pltpu.VMEM((D_MODEL, COL_TILE), BF16)],
        compiler_params=_params("arbitrary", "arbitrary"),
        name="out_proj",
    )(merged, w_out, x2, mod3)


def _route_kernel(x_ref, g_ref, sh_ref, sc_ref, whi_ref, wlo_ref, b_ref, r_ref):
    h = _rms_modulate(x_ref[...], g_ref[...], sh_ref[0], sc_ref[0])
    hi = h.astype(BF16)
    lo = (h - hi.astype(F32)).astype(BF16)
    logits = _dot(hi, whi_ref[...]) + _dot(lo, whi_ref[...]) + _dot(hi, wlo_ref[...]) + b_ref[...]

    lane = lax.broadcasted_iota(jnp.int32, logits.shape, 1).astype(F32)
    neg = -jnp.inf
    first = lambda hit: jnp.min(jnp.where(hit, lane, float(LANES)), axis=1, keepdims=True)
    is_grp = lane < float(N_GROUPS)
    gl = jnp.where(is_grp, logits, neg)
    gmax = jnp.max(gl, axis=1, keepdims=True)
    grp = first(gl == gmax)
    g_w = 1.0 / jnp.sum(jnp.where(is_grp, jnp.exp(logits - gmax), 0.0), axis=1, keepdims=True)
    lo_lane = float(N_GROUPS) + grp * float(EXPERTS_PER_GROUP)
    el = jnp.where(lane >= lo_lane, jnp.where(lane < lo_lane + float(EXPERTS_PER_GROUP), logits, neg), neg)
    v1 = jnp.max(el, axis=1, keepdims=True)
    i1 = first(el == v1)
    el2 = jnp.where(lane == i1, neg, el)
    v2 = jnp.max(el2, axis=1, keepdims=True)
    i2 = first(el2 == v2)
    e = jnp.exp(v2 - v1)
    w1 = g_w / (1.0 + e)
    w2 = g_w * e / (1.0 + e)
    out = jnp.where(lane == 0.0, i1 - float(N_GROUPS),
                    jnp.where(lane == 1.0, i2 - float(N_GROUPS),
                              jnp.where(lane == 2.0, w1, jnp.where(lane == 3.0, w2, 0.0))))
    r_ref[...] = out


def _route(x1, g, mod3, shift_blk, scale_blk, w_hi, w_lo, bias, rows=512):
    m = x1.shape[0]
    return pl.pallas_call(
        _route_kernel,
        grid=(m // rows,),
        in_specs=[
            pl.BlockSpec((rows, D_MODEL), lambda i: (i, 0)),
            pl.BlockSpec((1, D_MODEL), lambda i: (0, 0)),
            pl.BlockSpec((1, 1, D_MODEL), lambda i: (i * rows // SEQ, 0, shift_blk)),
            pl.BlockSpec((1, 1, D_MODEL), lambda i: (i * rows // SEQ, 0, scale_blk)),
            pl.BlockSpec((D_MODEL, LANES), lambda i: (0, 0)),
            pl.BlockSpec((D_MODEL, LANES), lambda i: (0, 0)),
            pl.BlockSpec((1, LANES), lambda i: (0, 0)),
        ],
        out_specs=pl.BlockSpec((rows, LANES), lambda i: (i, 0)),
        out_shape=jax.ShapeDtypeStruct((m, LANES), F32),
        compiler_params=_params("arbitrary"),
        name="route",
    )(x1, g, mod3, mod3, w_hi, w_lo, bias)


def _dispatch_plan(route):
    rows = EXPERT_ROWS
    flat = route[:, :2].astype(jnp.int32).reshape(-1)
    n_tiles = flat.shape[0] // rows
    experts = jnp.arange(N_EXPERTS, dtype=jnp.int32)
    onehot = (flat[:, None] == experts[None, :]).astype(jnp.int32)
    csum = jnp.cumsum(onehot, axis=0)
    rank = jnp.sum(csum * onehot, axis=1) - 1
    counts = csum[-1]
    cend = jnp.cumsum(counts)
    cstart = cend - counts
    pos = jnp.sum(cstart[None, :] * onehot, axis=1) + rank
    first_tile = cstart // rows
    items = jnp.where(counts > 0, (cend - 1) // rows - first_tile + 1, 0)
    wend = jnp.cumsum(items)
    wstart = wend - items
    n_work = wend[-1]
    w = jnp.arange(N_WORK_ITEMS, dtype=jnp.int32)
    valid = w < n_work
    last_expert = jnp.max(jnp.where(counts > 0, experts, 0))
    e_w = jnp.minimum(jnp.sum((wend[None, :] <= w[:, None]).astype(jnp.int32), axis=1), N_EXPERTS - 1)
    e_w = jnp.where(valid, e_w, last_expert)
    sel = (e_w[:, None] == experts[None, :]).astype(jnp.int32)
    pick = lambda v: jnp.sum(sel * v[None, :], axis=1)
    tile_w = jnp.where(valid, pick(first_tile) + w - pick(wstart), n_tiles - 1)
    lo_w = jnp.where(valid, jnp.clip(pick(cstart) - tile_w * rows, 0, rows), 0)
    hi_w = jnp.where(valid, jnp.clip(pick(cend) - tile_w * rows, 0, rows), 0)
    prev = lambda v: jnp.concatenate([jnp.full((1,), -1, jnp.int32), v[:-1]])
    new_expert = (e_w != prev(e_w)).astype(jnp.int32)
    new_tile = (tile_w != prev(tile_w)).astype(jnp.int32)
    later = jnp.logical_and(counts[None, :] > 0, experts[None, :] > experts[:, None])
    next_of = jnp.min(jnp.where(later, experts[None, :], N_EXPERTS), axis=1)
    next_of = jnp.where(next_of < N_EXPERTS, next_of, -1)
    buf_w = (jnp.cumsum(new_expert) - 1) % 2
    plan = jnp.stack([tile_w, e_w, new_expert, new_tile, lo_w, hi_w,
                      jnp.broadcast_to(n_work, w.shape), pick(next_of), buf_w]).astype(jnp.int32)
    return pos, plan


def _dispatch_kernel(pos_ref, x_ref, g_ref, sh_ref, sc_ref, xs_hbm, hbuf, sem):
    rows = DISPATCH_ROWS
    i = pl.program_id(0)
    n = pl.num_programs(0)
    slot = i % 2

    def row_copy(s, r, dst_row):
        return pltpu.make_async_copy(hbuf.at[s, pl.ds(r, 1), :], xs_hbm.at[pl.ds(dst_row, 1), :], sem.at[s])

    def drain(s):
        def body(r, c):
            row_copy(s, 0, 0).wait()
            return c
        lax.fori_loop(0, 2 * rows, body, 0, unroll=8)

    @pl.when(i >= 2)
    def _():
        drain(slot)

    hbuf[slot] = _rms_modulate(x_ref[...], g_ref[...], sh_ref[0], sc_ref[0])

    def issue(r, c):
        row_copy(slot, r, pos_ref[0, 0, r]).start()
        row_copy(slot, r, pos_ref[0, 0, rows + r]).start()
        return c
    lax.fori_loop(0, rows, issue, 0, unroll=8)

    @pl.when(i == n - 1)
    def _():
        drain(slot)
        drain(1 - slot)


def _dispatch(x1, g, mod3, shift_blk, scale_blk, pos3, n_sorted):
    rows = DISPATCH_ROWS
    m = x1.shape[0]
    assert m // rows >= 2
    return pl.pallas_call(
        _dispatch_kernel,
        grid=(m // rows,),
        in_specs=[
            pl.BlockSpec((1, 1, 2 * rows), lambda i: (i, 0, 0), memory_space=pltpu.SMEM),
            pl.BlockSpec((rows, D_MODEL), lambda i: (i, 0)),
            pl.BlockSpec((1, D_MODEL), lambda i: (0, 0)),
            pl.BlockSpec((1, 1, D_MODEL), lambda i: (i * rows // SEQ, 0, shift_blk)),
            pl.BlockSpec((1, 1, D_MODEL), lambda i: (i * rows // SEQ, 0, scale_blk)),
        ],
        out_specs=pl.BlockSpec(memory_space=pl.ANY),
        out_shape=jax.ShapeDtypeStruct((n_sorted, D_MODEL), F32),
        scratch_shapes=[pltpu.VMEM((2, rows, D_MODEL), F32), pltpu.SemaphoreType.DMA((2,))],
        compiler_params=_params("arbitrary"),
        name="dispatch",
    )(pos3, x1, g, mod3, mod3)


def _experts_kernel(plan_ref, x_ref, wg_hbm, wu_hbm, wd_hbm, y_ref,
                    wg_buf, wu_buf, wd_buf, sem, wgb_ref, wub_ref, wdb_ref):
    w = pl.program_id(0)

    def weight_copies(e, s):
        return (pltpu.make_async_copy(wg_hbm.at[e], wg_buf.at[s], sem.at[s, 0]),
                pltpu.make_async_copy(wu_hbm.at[e], wu_buf.at[s], sem.at[s, 1]),
                pltpu.make_async_copy(wd_hbm.at[e], wd_buf.at[s], sem.at[s, 2]))

    @pl.when(w == 0)
    def _():
        for cp in weight_copies(plan_ref[1, 0], 0):
            cp.start()

    @pl.when(w < plan_ref[6, w])
    def _():
        @pl.when(plan_ref[2, w] == 1)
        def _():
            s = plan_ref[8, w]

            @pl.when(plan_ref[7, w] >= 0)
            def _():
                for cp in weight_copies(plan_ref[7, w], 1 - s):
                    cp.start()

            for cp in weight_copies(plan_ref[1, w], s):
                cp.wait()
            wgb_ref[...] = wg_buf[s].astype(BF16)
            wub_ref[...] = wu_buf[s].astype(BF16)
            wdb_ref[...] = wd_buf[s].astype(BF16)

        x = x_ref[...].astype(BF16)
        hg = _dot(x, wgb_ref[...])
        hu = _dot(x, wub_ref[...])
        hid = _silu(hg) * hu
        y = _dot(hid.astype(BF16), wdb_ref[...])
        row = lax.broadcasted_iota(jnp.int32, (EXPERT_ROWS, 1), 0)
        mine = jnp.logical_and(row >= plan_ref[4, w], row < plan_ref[5, w])

        @pl.when(plan_ref[3, w] == 1)
        def _():
            y_ref[...] = jnp.where(mine, y, 0.0)

        @pl.when(plan_ref[3, w] == 0)
        def _():
            y_ref[...] = jnp.where(mine, y, y_ref[...])


def _experts(xs, plan, w_gate, w_up, w_down):
    rows = EXPERT_ROWS
    grid_spec = pltpu.PrefetchScalarGridSpec(
        num_scalar_prefetch=1,
        grid=(N_WORK_ITEMS,),
        in_specs=[
            pl.BlockSpec((rows, D_MODEL), lambda w, p: (p[0, w], 0)),
            pl.BlockSpec(memory_space=pl.ANY),
            pl.BlockSpec(memory_space=pl.ANY),
            pl.BlockSpec(memory_space=pl.ANY),
        ],
        out_specs=pl.BlockSpec((rows, D_MODEL), lambda w, p: (p[0, w], 0)),
        scratch_shapes=[
            pltpu.VMEM((2, D_MODEL, EXPERT_FF), F32),
            pltpu.VMEM((2, D_MODEL, EXPERT_FF), F32),
            pltpu.VMEM((2, EXPERT_FF, D_MODEL), F32),
            pltpu.SemaphoreType.DMA((2, 3)),
            pltpu.VMEM((D_MODEL, EXPERT_FF), BF16),
            pltpu.VMEM((D_MODEL, EXPERT_FF), BF16),
            pltpu.VMEM((EXPERT_FF, D_MODEL), BF16),
        ],
    )
    return pl.pallas_call(
        _experts_kernel,
        grid_spec=grid_spec,
        out_shape=jax.ShapeDtypeStruct(xs.shape, F32),
        compiler_params=_params("arbitrary"),
        name="experts",
    )(plan, xs, w_gate, w_up, w_down)


def _final_kernel(pos_ref, posn_ref, x_ref, r_ref, g2_ref, fg_ref, y_hbm, o_ref, ybuf, sem):
    rows = FINAL_ROWS
    i = pl.program_id(0)
    n = pl.num_programs(0)
    slot = i % 2

    def row_copy(src_row, s, r):
        return pltpu.make_async_copy(y_hbm.at[pl.ds(src_row, 1), :], ybuf.at[s, pl.ds(r, 1), :], sem.at[s])

    def issue(idx_ref, s):
        def body(r, c):
            row_copy(idx_ref[0, 0, r], s, r).start()
            return c
        lax.fori_loop(0, 2 * rows, body, 0, unroll=8)

    @pl.when(i == 0)
    def _():
        issue(pos_ref, 0)

    @pl.when(i + 1 < n)
    def _():
        issue(posn_ref, 1 - slot)

    def wait_body(r, c):
        row_copy(0, slot, r).wait()
        return c
    lax.fori_loop(0, 2 * rows, wait_body, 0, unroll=8)

    route = r_ref[...]
    w1 = route[:, 2:3]
    w2 = route[:, 3:4]
    moe = w1 * ybuf[slot, 0:rows, :] + w2 * ybuf[slot, rows:2 * rows, :]
    x = x_ref[...] + g2_ref[0] * moe
    o_ref[...] = x * lax.rsqrt(jnp.mean(x * x, axis=-1, keepdims=True) + EPS) * fg_ref[...]


def _final(x1, route, mod3, gate_blk, final_g, y, pos3):
    rows = FINAL_ROWS
    m = x1.shape[0]
    nt = m // rows
    return pl.pallas_call(
        _final_kernel,
        grid=(nt,),
        in_specs=[
            pl.BlockSpec((1, 1, 2 * rows), lambda i: (i, 0, 0), memory_space=pltpu.SMEM),
            pl.BlockSpec((1, 1, 2 * rows), lambda i: (jnp.minimum(i + 1, nt - 1), 0, 0), memory_space=pltpu.SMEM),
            pl.BlockSpec((rows, D_MODEL), lambda i: (i, 0)),
            pl.BlockSpec((rows, LANES), lambda i: (i, 0)),
            pl.BlockSpec((1, 1, D_MODEL), lambda i: (i * rows // SEQ, 0, gate_blk)),
            pl.BlockSpec((1, D_MODEL), lambda i: (0, 0)),
            pl.BlockSpec(memory_space=pl.ANY),
        ],
        out_specs=pl.BlockSpec((rows, D_MODEL), lambda i: (i, 0)),
        out_shape=jax.ShapeDtypeStruct((m, D_MODEL), F32),
        scratch_shapes=[pltpu.VMEM((2, 2 * rows, D_MODEL), F32), pltpu.SemaphoreType.DMA((2,))],
        compiler_params=_params("arbitrary"),
        name="final",
    )(pos3, pos3, x1, route, mod3, final_g, y)


def _rope_tables():
    rows = SEQ // GRID_W
    row = jnp.repeat(jnp.arange(rows, dtype=F32), GRID_W)
    col = jnp.tile(jnp.arange(GRID_W, dtype=F32), rows)
    n_freq = RET_DK // 4
    inv = ROPE_BASE ** (-jnp.arange(n_freq, dtype=F32) / n_freq)
    ang = jnp.concatenate([row[:, None] * inv, col[:, None] * inv], axis=-1)
    cos, sin = jnp.cos(ang), jnp.sin(ang)
    return jnp.concatenate([cos, cos], axis=-1), jnp.concatenate([-sin, sin], axis=-1)


def kernel(x, c, ctx, c_ctx, w_mod, b_mod, norm1_g, norm2_g, w_in, w_four_out, w_ret_out, w_out,
           ret_decay_f, ret_decay_b, w_group_router, b_group_router, w_expert_router, b_expert_router,
           w_gate, w_up, w_down, final_norm_g):
    b, n, d = x.shape
    assert (n, d) == (SEQ, D_MODEL) and ctx.shape[1] == CTX_LEN and w_mod.shape[0] == 1
    t = b * n

    c8 = jnp.zeros((8, d), F32).at[:b].set(c).at[b].set(c_ctx)
    mod3 = _modulation(c8, w_mod[0], b_mod).reshape(8, 1, N_MOD * d)

    hx = _prenorm(x, norm1_g, mod3, lambda i: i, 0, 1, rows=1024)
    hc = _prenorm(ctx, norm1_g, mod3, lambda i: b, 0, 1, rows=CTX_LEN)

    cos2, sin2 = _rope_tables()
    p2 = _in_proj(hx.reshape(t, d), w_in[0], cos2, sin2)
    p3 = p2.reshape(b, n, IN_WIDTH)
    kvc = _project(hc.reshape(b * CTX_LEN, d), w_in[0], K_OFF // COL_TILE, (GF_OFF - K_OFF) // COL_TILE,
                   b * CTX_LEN, "ctx_proj")
    kvc3 = kvc.reshape(b, CTX_LEN, GF_OFF - K_OFF)

    fm = _fourier(p3)

    lg_f = jax.nn.log_sigmoid(ret_decay_f[0].astype(F32))
    lg_b = jax.nn.log_sigmoid(ret_decay_b[0].astype(F32))
    lg = jnp.stack([lg_f, lg_b, jnp.exp(RET_BLOCK * lg_f), jnp.exp(RET_BLOCK * lg_b)])
    ret = _retention(p3, kvc3, lg)

    merged = _merge(fm.reshape(t, FOUR_WIDTH), ret.reshape(t, d), w_four_out[0], w_ret_out[0], p2)
    x1 = _out_proj(merged, w_out[0], x.reshape(t, d), mod3, 2 * d // COL_TILE)

    w_router = jnp.concatenate(
        [w_group_router[0], w_expert_router[0].transpose(1, 0, 2).reshape(d, N_EXPERTS)], axis=1).astype(F32)
    w_router = jnp.pad(w_router, ((0, 0), (0, LANES - w_router.shape[1])))
    w_hi = w_router.astype(BF16)
    w_lo = (w_router - w_hi.astype(F32)).astype(BF16)
    bias = jnp.pad(jnp.concatenate([b_group_router[0], b_expert_router[0].reshape(-1)]).astype(F32),
                   (0, LANES - N_GROUPS - N_EXPERTS)).reshape(1, LANES)
    route = _route(x1, norm2_g, mod3, 3, 4, w_hi, w_lo, bias)

    pos, plan = _dispatch_plan(route)
    assert DISPATCH_ROWS == FINAL_ROWS
    pos3 = pos.reshape(t // FINAL_ROWS, FINAL_ROWS, 2).transpose(0, 2, 1).reshape(t // FINAL_ROWS, 1, 2 * FINAL_ROWS)
    xs = _dispatch(x1, norm2_g, mod3, 3, 4, pos3, 2 * t)
    y = _experts(xs, plan, w_gate[0], w_up[0], w_down[0])
    out = _final(x1, route, mod3, 5, final_norm_g.reshape(1, d), y, pos3)
    return out.reshape(b, n, d)
```

```python
import functools

import jax
import jax.numpy as jnp
import numpy as np
from jax import lax
from jax.experimental import pallas as pl
from jax.experimental.pallas import tpu as pltpu

F32 = jnp.float32
BF16 = jnp.bfloat16

D_MODEL = 2048
SEQ = 2048
CTX_LEN = 256
GRID_W = 64
N_MOD = 6
FOUR_GROUPS = 8
FOUR_GROUP_DIM = 128
FOUR_WIDTH = FOUR_GROUPS * FOUR_GROUP_DIM
RET_HEADS = 8
RET_DK = 128
RET_DV = 256
ROPE_BASE = 10000.0
N_GROUPS = 4
EXPERTS_PER_GROUP = 8
N_EXPERTS = N_GROUPS * EXPERTS_PER_GROUP
EXPERT_FF = 512
EPS = 1e-6
Q_OFF = FOUR_WIDTH
K_OFF = Q_OFF + RET_HEADS * RET_DK
V_OFF = K_OFF + RET_HEADS * RET_DK
GF_OFF = V_OFF + RET_HEADS * RET_DV
GB_OFF = GF_OFF + RET_HEADS * RET_DV
MF_OFF = GB_OFF + RET_HEADS * RET_DV
MR_OFF = MF_OFF + D_MODEL
IN_WIDTH = MR_OFF + D_MODEL

V7X_VMEM_LIMIT_BYTES = 56 * 1024 * 1024
LANES = 128
RET_BLOCK = 256
COL_TILE = 1024
IN_COL_TILE = 1024
IN_ROWS = 1024
EXPERT_ROWS = 256
DISPATCH_ROWS = 256
FINAL_ROWS = 256
N_WORK_ITEMS = 2 * 4 * SEQ // EXPERT_ROWS + N_EXPERTS


def _params(*sem):
    return pltpu.CompilerParams(dimension_semantics=sem, vmem_limit_bytes=V7X_VMEM_LIMIT_BYTES)


def _dot(a, b):
    return jnp.dot(a, b, preferred_element_type=F32)


def _mod_kernel(c_ref, w_ref, b_ref, o_ref):
    c = c_ref[...]
    sc = c * jax.nn.sigmoid(c)
    o_ref[...] = _dot(sc.astype(BF16), w_ref[...].astype(BF16)) + b_ref[...]


def _modulation(c8, w_mod, b_mod):
    n_out = w_mod.shape[1]
    return pl.pallas_call(
        _mod_kernel,
        grid=(n_out // COL_TILE,),
        in_specs=[
            pl.BlockSpec((8, D_MODEL), lambda j: (0, 0)),
            pl.BlockSpec((D_MODEL, COL_TILE), lambda j: (0, j)),
            pl.BlockSpec((1, COL_TILE), lambda j: (0, j)),
        ],
        out_specs=pl.BlockSpec((8, COL_TILE), lambda j: (0, j)),
        out_shape=jax.ShapeDtypeStruct((8, n_out), F32),
        compiler_params=_params("arbitrary"),
        name="mod",
    )(c8, w_mod, b_mod)


def _rms_modulate(x, g, shift, scale):
    y = x * lax.rsqrt(jnp.mean(x * x, axis=-1, keepdims=True) + EPS) * g
    return y * (1.0 + scale) + shift


def _prenorm_kernel(x_ref, g_ref, sh_ref, sc_ref, o_ref):
    o_ref[0] = _rms_modulate(x_ref[0], g_ref[...], sh_ref[0], sc_ref[0]).astype(o_ref.dtype)


def _prenorm(x, g, mod3, row_of_batch, shift_blk, scale_blk, rows):
    b, n, d = x.shape
    return pl.pallas_call(
        _prenorm_kernel,
        grid=(b, n // rows),
        in_specs=[
            pl.BlockSpec((1, rows, d), lambda i, j: (i, j, 0)),
            pl.BlockSpec((1, d), lambda i, j: (0, 0)),
            pl.BlockSpec((1, 1, d), lambda i, j: (row_of_batch(i), 0, shift_blk)),
            pl.BlockSpec((1, 1, d), lambda i, j: (row_of_batch(i), 0, scale_blk)),
        ],
        out_specs=pl.BlockSpec((1, rows, d), lambda i, j: (i, j, 0)),
        out_shape=jax.ShapeDtypeStruct((b, n, d), BF16),
        compiler_params=_params("arbitrary", "arbitrary"),
        name="prenorm",
    )(x, g, mod3, mod3)


def _proj_kernel(a_ref, w_ref, o_ref, wb_ref):
    @pl.when(pl.program_id(1) == 0)
    def _():
        wb_ref[...] = w_ref[...].astype(BF16)

    o_ref[...] = _dot(a_ref[...], wb_ref[...]).astype(o_ref.dtype)


def _project(a, w, col_tile0, n_col_tiles, rows, name):
    m, k = a.shape
    return pl.pallas_call(
        _proj_kernel,
        grid=(n_col_tiles, m // rows),
        in_specs=[
            pl.BlockSpec((rows, k), lambda j, i: (i, 0)),
            pl.BlockSpec((k, COL_TILE), lambda j, i: (0, j + col_tile0)),
        ],
        out_specs=pl.BlockSpec((rows, COL_TILE), lambda j, i: (i, j)),
        out_shape=jax.ShapeDtypeStruct((m, n_col_tiles * COL_TILE), BF16),
        scratch_shapes=[pltpu.VMEM((k, COL_TILE), BF16)],
        compiler_params=_params("arbitrary", "arbitrary"),
        name=name,
    )(a, w)


def _rope(t, cos2, sin2):
    return t * cos2 + pltpu.roll(t, RET_DK // 2, axis=1) * sin2


def _in_proj_kernel(a_ref, w_ref, cos_ref, sin_ref, o_ref, wb_ref):
    j = pl.program_id(0)

    @pl.when(pl.program_id(1) == 0)
    def _():
        wb_ref[...] = w_ref[...].astype(BF16)

    q_tile, k_tile, v_tile = Q_OFF // IN_COL_TILE, K_OFF // IN_COL_TILE, V_OFF // IN_COL_TILE
    gate_tile, merge_tile = GF_OFF // IN_COL_TILE, MF_OFF // IN_COL_TILE
    is_qk = jnp.logical_and(j >= q_tile, j < v_tile)
    is_gate = jnp.logical_and(j >= gate_tile, j < merge_tile)

    def store_chunks(val):
        for c in range(IN_COL_TILE // LANES):
            o_ref[c] = val[:, c * LANES:(c + 1) * LANES].astype(o_ref.dtype)

    @pl.when(jnp.logical_not(jnp.logical_or(is_qk, is_gate)))
    def _():
        store_chunks(_dot(a_ref[...], wb_ref[...]))

    @pl.when(is_qk)
    def _():
        acc = _dot(a_ref[...], wb_ref[...])
        scale = jnp.where(j < k_tile, RET_DK ** -0.5, 1.0).astype(F32)
        cos2, sin2 = cos_ref[...], sin_ref[...]
        assert RET_DK == LANES
        for h in range(IN_COL_TILE // RET_DK):
            cols = slice(h * RET_DK, (h + 1) * RET_DK)
            o_ref[h] = (_rope(acc[:, cols], cos2, sin2) * scale).astype(o_ref.dtype)

    @pl.when(is_gate)
    def _():
        store_chunks(_silu(_dot(a_ref[...], wb_ref[...])))


def _in_proj(a, w, cos2, sin2, rows=IN_ROWS):
    m, k = a.shape
    n_col_tiles = IN_WIDTH // IN_COL_TILE
    pos_blocks = SEQ // rows
    return pl.pallas_call(
        _in_proj_kernel,
        grid=(n_col_tiles, m // rows),
        in_specs=[
            pl.BlockSpec((rows, k), lambda j, i: (i, 0)),
            pl.BlockSpec((k, IN_COL_TILE), lambda j, i: (0, j)),
            pl.BlockSpec((rows, RET_DK), lambda j, i: (i % pos_blocks, 0)),
            pl.BlockSpec((rows, RET_DK), lambda j, i: (i % pos_blocks, 0)),
        ],
        out_specs=pl.BlockSpec((IN_COL_TILE // LANES, rows, LANES), lambda j, i: (j, i, 0)),
        out_shape=jax.ShapeDtypeStruct((IN_WIDTH // LANES, m, LANES), BF16),
        scratch_shapes=[pltpu.VMEM((k, IN_COL_TILE), BF16)],
        compiler_params=_params("arbitrary", "arbitrary"),
        name="in_proj",
    )(a, w, cos2, sin2)


def _dft_tables():
    n = np.arange(SEQ, dtype=np.int64)
    ang_n = 2.0 * np.pi * ((n[:, None] * n[None, :]) % SEQ) / SEQ
    pos = np.concatenate([np.cos(ang_n), -np.sin(ang_n)], axis=1).astype(np.float32)
    c = np.arange(FOUR_GROUP_DIM, dtype=np.int64)
    ang_c = 2.0 * np.pi * ((c[:, None] * c[None, :]) % FOUR_GROUP_DIM) / FOUR_GROUP_DIM
    norm = 1.0 / np.sqrt(float(SEQ * FOUR_GROUP_DIM))
    chan = (np.concatenate([np.cos(ang_c), np.sin(ang_c)], axis=1) * norm).astype(np.float32)
    return pos, chan


def _fourier_kernel(u_ref, chan_ref, pos_ref, o_ref, y_ref):
    @pl.when(pl.program_id(1) == 0)
    def _():
        chan = chan_ref[...]
        for g in range(FOUR_GROUPS):
            cols = slice(g * FOUR_GROUP_DIM, (g + 1) * FOUR_GROUP_DIM)
            y = _dot(u_ref[g], chan)
            y_ref[0:SEQ, cols] = y[:, :FOUR_GROUP_DIM].astype(BF16)
            y_ref[SEQ:2 * SEQ, cols] = y[:, FOUR_GROUP_DIM:].astype(BF16)

    o_ref[0] = _dot(pos_ref[...], y_ref[...]).astype(o_ref.dtype)


def _fourier(pc, b, rows=1024):
    assert FOUR_GROUP_DIM == LANES
    pos_np, chan_np = _dft_tables()
    pos = jnp.asarray(pos_np).astype(BF16)
    chan = jnp.asarray(chan_np).astype(BF16)
    return pl.pallas_call(
        _fourier_kernel,
        grid=(b, SEQ // rows),
        in_specs=[
            pl.BlockSpec((FOUR_GROUPS, SEQ, LANES), lambda i, j: (0, i, 0)),
            pl.BlockSpec((FOUR_GROUP_DIM, 2 * FOUR_GROUP_DIM), lambda i, j: (0, 0)),
            pl.BlockSpec((rows, 2 * SEQ), lambda i, j: (j, 0)),
        ],
        out_specs=pl.BlockSpec((1, rows, FOUR_WIDTH), lambda i, j: (i, j, 0)),
        out_shape=jax.ShapeDtypeStruct((b, SEQ, FOUR_WIDTH), BF16),
        scratch_shapes=[pltpu.VMEM((2 * SEQ, FOUR_WIDTH), BF16)],
        compiler_params=_params("arbitrary", "arbitrary"),
        name="fourier",
    )(pc, chan, pos)


def _head_norm(o):
    mu = jnp.mean(o, axis=-1, keepdims=True)
    d = o - mu
    var = jnp.mean(d * d, axis=-1, keepdims=True)
    return d * lax.rsqrt(var + EPS)


def _silu(x):
    return x * jax.nn.sigmoid(x)


def _retention_kernel(lg_ref, q_ref, k_ref, v_ref, gf_ref, gb_ref, kc_ref, vc_ref,
                      o_ref, acc_ref, df_ref, db_ref, xif_ref, xib_ref, zf_ref, zb_ref):
    blk = RET_BLOCK
    n_blk = SEQ // blk
    h = pl.program_id(0)
    lgf, lgb, gchf, gchb = lg_ref[0, h], lg_ref[1, h], lg_ref[2, h], lg_ref[3, h]

    @pl.when(pl.program_id(1) == 0)
    def _():
        row = lax.broadcasted_iota(jnp.int32, (blk, blk), 0)
        col = lax.broadcasted_iota(jnp.int32, (blk, blk), 1)
        diff = (row - col).astype(F32)
        df_ref[...] = jnp.where(diff >= 0, jnp.exp(jnp.maximum(diff, 0.0) * lgf), 0.0)
        db_ref[...] = jnp.where(diff <= 0, jnp.exp(jnp.maximum(-diff, 0.0) * lgb), 0.0)
        pos_v = lax.broadcasted_iota(jnp.int32, (blk, RET_DV), 0).astype(F32)
        xif_ref[...] = jnp.exp((pos_v + 1.0) * lgf)
        xib_ref[...] = jnp.exp((blk - pos_v) * lgb)
        pos_k = lax.broadcasted_iota(jnp.int32, (blk, RET_DK), 0).astype(F32)
        zf_ref[...] = jnp.exp((blk - 1.0 - pos_k) * lgf)
        zb_ref[...] = jnp.exp(pos_k * lgb)

    def state_update(r, k, z_ref, v, gch):
        kz = (k.astype(F32) * z_ref[...]).astype(BF16)
        return gch * r + lax.dot_general(kz, v, (((0,), (0,)), ((), ())), preferred_element_type=F32)

    def block_out(qb, kb, vb, r, d_ref, xi_ref):
        s = lax.dot_general(qb, kb, (((1,), (1,)), ((), ())), preferred_element_type=F32)
        inner = _dot((s * d_ref[...]).astype(BF16), vb)
        cross = _dot(qb, r.astype(BF16)) * xi_ref[...]
        return inner + cross

    def wide(ref, rows):
        return jnp.concatenate([ref[0, rows, :], ref[1, rows, :]], axis=1)

    zero = jnp.zeros((RET_DK, RET_DV), F32)
    r_f = state_update(zero, kc_ref[0], zf_ref, vc_ref[0], gchf)
    r_b = state_update(zero, kc_ref[0], zb_ref, vc_ref[0], gchb)

    def emit(i, part, reached_first):
        rows = slice(i * blk, (i + 1) * blk)
        if reached_first:
            acc_ref[rows, :] = part
        else:
            o_ref[0, rows, :] = (acc_ref[rows, :] + part).astype(o_ref.dtype)

    for j in range(n_blk):
        i_f, i_b = j, n_blk - 1 - j
        rows_f = slice(i_f * blk, (i_f + 1) * blk)
        rows_b = slice(i_b * blk, (i_b + 1) * blk)
        qf, kf, vf = q_ref[0, rows_f, :], k_ref[0, rows_f, :], wide(v_ref, rows_f)
        qk, kk, vk = q_ref[0, rows_b, :], k_ref[0, rows_b, :], wide(v_ref, rows_b)
        o_f = block_out(qf, kf, vf, r_f, df_ref, xif_ref)
        o_b = block_out(qk, kk, vk, r_b, db_ref, xib_ref)
        r_f = state_update(r_f, kf, zf_ref, vf, gchf)
        r_b = state_update(r_b, kk, zb_ref, vk, gchb)
        first_visit = j < n_blk - 1 - j
        emit(i_f, wide(gf_ref, rows_f).astype(F32) * _head_norm(o_f), first_visit)
        emit(i_b, wide(gb_ref, rows_b).astype(F32) * _head_norm(o_b), first_visit)


def _retention(pc, kvc3, lg):
    b = kvc3.shape[0]
    assert CTX_LEN == RET_BLOCK and (SEQ // RET_BLOCK) % 2 == 0 and RET_DK == LANES and RET_DV == 2 * LANES
    qb0, kb0 = Q_OFF // RET_DK, K_OFF // RET_DK
    vb0, gfb0, gbb0 = V_OFF // RET_DV, GF_OFF // RET_DV, GB_OFF // RET_DV
    kcv0 = (RET_HEADS * RET_DK) // RET_DV
    return pl.pallas_call(
        _retention_kernel,
        grid=(RET_HEADS, b),
        in_specs=[
            pl.BlockSpec(memory_space=pltpu.SMEM),
            pl.BlockSpec((1, SEQ, LANES), lambda h, i: (qb0 + h, i, 0)),
            pl.BlockSpec((1, SEQ, LANES), lambda h, i: (kb0 + h, i, 0)),
            pl.BlockSpec((RET_DV // LANES, SEQ, LANES), lambda h, i: (vb0 + h, i, 0)),
            pl.BlockSpec((RET_DV // LANES, SEQ, LANES), lambda h, i: (gfb0 + h, i, 0)),
            pl.BlockSpec((RET_DV // LANES, SEQ, LANES), lambda h, i: (gbb0 + h, i, 0)),
            pl.BlockSpec((1, CTX_LEN, RET_DK), lambda h, i: (i, 0, h)),
            pl.BlockSpec((1, CTX_LEN, RET_DV), lambda h, i: (i, 0, kcv0 + h)),
        ],
        out_specs=pl.BlockSpec((1, SEQ, RET_DV), lambda h, i: (i, 0, h)),
        out_shape=jax.ShapeDtypeStruct((b, SEQ, RET_HEADS * RET_DV), BF16),
        scratch_shapes=[
            pltpu.VMEM((SEQ, RET_DV), F32),
            pltpu.VMEM((RET_BLOCK, RET_BLOCK), F32),
            pltpu.VMEM((RET_BLOCK, RET_BLOCK), F32),
            pltpu.VMEM((RET_BLOCK, RET_DV), F32),
            pltpu.VMEM((RET_BLOCK, RET_DV), F32),
            pltpu.VMEM((RET_BLOCK, RET_DK), F32),
            pltpu.VMEM((RET_BLOCK, RET_DK), F32),
        ],
        compiler_params=_params("arbitrary", "arbitrary"),
        name="retention",
    )(lg, pc, pc, pc, pc, pc, kvc3, kvc3)


def _merge_kernel(fm_ref, ret_ref, wf_ref, wr_ref, mf_ref, mr_ref, o_ref, wfb_ref, wrb_ref):
    @pl.when(pl.program_id(1) == 0)
    def _():
        wfb_ref[...] = wf_ref[...].astype(BF16)
        wrb_ref[...] = wr_ref[...].astype(BF16)

    four = _dot(fm_ref[...], wfb_ref[...])
    ret = _dot(ret_ref[...], wrb_ref[...])
    for c in range(COL_TILE // LANES):
        cols = slice(c * LANES, (c + 1) * LANES)
        merged = (jax.nn.sigmoid(mf_ref[c].astype(F32)) * four[:, cols]
                  + jax.nn.sigmoid(mr_ref[c].astype(F32)) * ret[:, cols])
        o_ref[:, cols] = merged.astype(o_ref.dtype)


def _merge(fm2, ret2, w_four_out, w_ret_out, pc, rows=512):
    m = fm2.shape[0]
    mf0, mr0 = MF_OFF // COL_TILE, MR_OFF // COL_TILE
    chunks = COL_TILE // LANES
    return pl.pallas_call(
        _merge_kernel,
        grid=(D_MODEL // COL_TILE, m // rows),
        in_specs=[
            pl.BlockSpec((rows, FOUR_WIDTH), lambda j, i: (i, 0)),
            pl.BlockSpec((rows, D_MODEL), lambda j, i: (i, 0)),
            pl.BlockSpec((FOUR_WIDTH, COL_TILE), lambda j, i: (0, j)),
            pl.BlockSpec((D_MODEL, COL_TILE), lambda j, i: (0, j)),
            pl.BlockSpec((chunks, rows, LANES), lambda j, i: (mf0 + j, i, 0)),
            pl.BlockSpec((chunks, rows, LANES), lambda j, i: (mr0 + j, i, 0)),
        ],
        out_specs=pl.BlockSpec((rows, COL_TILE), lambda j, i: (i, j)),
        out_shape=jax.ShapeDtypeStruct((m, D_MODEL), BF16),
        scratch_shapes=[pltpu.VMEM((FOUR_WIDTH, COL_TILE), BF16), pltpu.VMEM((D_MODEL, COL_TILE), BF16)],
        compiler_params=_params("arbitrary", "arbitrary"),
        name="merge",
    )(fm2, ret2, w_four_out, w_ret_out, pc, pc)


def _out_proj_kernel(a_ref, w_ref, x_ref, g_ref, o_ref, wb_ref):
    @pl.when(pl.program_id(1) == 0)
    def _():
        wb_ref[...] = w_ref[...].astype(BF16)

    o_ref[...] = x_ref[...] + g_ref[0] * _dot(a_ref[...], wb_ref[...])


def _out_proj(merged, w_out, x2, mod3, gate_blk0, rows=1024):
    m = merged.shape[0]
    return pl.pallas_call(
        _out_proj_kernel,
        grid=(D_MODEL // COL_TILE, m // rows),
        in_specs=[
            pl.BlockSpec((rows, D_MODEL), lambda j, i: (i, 0)),
            pl.BlockSpec((D_MODEL, COL_TILE), lambda j, i: (0, j)),
            pl.BlockSpec((rows, COL_TILE), lambda j, i: (i, j)),
            pl.BlockSpec((1, 1, COL_TILE), lambda j, i: (i * rows // SEQ, 0, gate_blk0 + j)),
        ],
        out_specs=pl.BlockSpec((rows, COL_TILE), lambda j, i: (i, j)),
        out_shape=jax.ShapeDtypeStruct((m, D_MODEL), F32),
        scratch_shapes=[pltpu.VMEM((D_MODEL, COL_TILE), BF16)],
        compiler_params=_params("arbitrary", "arbitrary"),
        name="out_proj",
    )(merged, w_out, x2, mod3)


def _route_kernel(x_ref, g_ref, sh_ref, sc_ref, whi_ref, wlo_ref, b_ref, r_ref):
    h = _rms_modulate(x_ref[...], g_ref[...], sh_ref[0], sc_ref[0])
    hi = h.astype(BF16)
    lo = (h - hi.astype(F32)).astype(BF16)
    logits = _dot(hi, whi_ref[...]) + _dot(lo, whi_ref[...]) + _dot(hi, wlo_ref[...]) + b_ref[...]

    lane = lax.broadcasted_iota(jnp.int32, logits.shape, 1).astype(F32)
    neg = -jnp.inf
    first = lambda hit: jnp.min(jnp.where(hit, lane, float(LANES)), axis=1, keepdims=True)
    is_grp = lane < float(N_GROUPS)
    gl = jnp.where(is_grp, logits, neg)
    gmax = jnp.max(gl, axis=1, keepdims=True)
    grp = first(gl == gmax)
    g_w = 1.0 / jnp.sum(jnp.where(is_grp, jnp.exp(logits - gmax), 0.0), axis=1, keepdims=True)
    lo_lane = float(N_GROUPS) + grp * float(EXPERTS_PER_GROUP)
    el = jnp.where(lane >= lo_lane, jnp.where(lane < lo_lane + float(EXPERTS_PER_GROUP), logits, neg), neg)
    v1 = jnp.max(el, axis=1, keepdims=True)
    i1 = first(el == v1)
    el2 = jnp.where(lane == i1, neg, el)
    v2 = jnp.max(el2, axis=1, keepdims=True)
    i2 = first(el2 == v2)
    e = jnp.exp(v2 - v1)
    w1 = g_w / (1.0 + e)
    w2 = g_w * e / (1.0 + e)
    out = jnp.where(lane == 0.0, i1 - float(N_GROUPS),
                    jnp.where(lane == 1.0, i2 - float(N_GROUPS),
                              jnp.where(lane == 2.0, w1, jnp.where(lane == 3.0, w2, 0.0))))
    r_ref[...] = out


def _route(x1, g, mod3, shift_blk, scale_blk, w_hi, w_lo, bias, rows=512):
    m = x1.shape[0]
    return pl.pallas_call(
        _route_kernel,
        grid=(m // rows,),
        in_specs=[
            pl.BlockSpec((rows, D_MODEL), lambda i: (i, 0)),
            pl.BlockSpec((1, D_MODEL), lambda i: (0, 0)),
            pl.BlockSpec((1, 1, D_MODEL), lambda i: (i * rows // SEQ, 0, shift_blk)),
            pl.BlockSpec((1, 1, D_MODEL), lambda i: (i * rows // SEQ, 0, scale_blk)),
            pl.BlockSpec((D_MODEL, LANES), lambda i: (0, 0)),
            pl.BlockSpec((D_MODEL, LANES), lambda i: (0, 0)),
            pl.BlockSpec((1, LANES), lambda i: (0, 0)),
        ],
        out_specs=pl.BlockSpec((rows, LANES), lambda i: (i, 0)),
        out_shape=jax.ShapeDtypeStruct((m, LANES), F32),
        compiler_params=_params("arbitrary"),
        name="route",
    )(x1, g, mod3, mod3, w_hi, w_lo, bias)


def _dispatch_plan(route):
    rows = EXPERT_ROWS
    flat = route[:, :2].astype(jnp.int32).reshape(-1)
    n_tiles = flat.shape[0] // rows
    experts = jnp.arange(N_EXPERTS, dtype=jnp.int32)
    onehot = (flat[:, None] == experts[None, :]).astype(jnp.int32)
    csum = jnp.cumsum(onehot, axis=0)
    rank = jnp.sum(csum * onehot, axis=1) - 1
    counts = csum[-1]
    cend = jnp.cumsum(counts)
    cstart = cend - counts
    pos = jnp.sum(cstart[None, :] * onehot, axis=1) + rank
    first_tile = cstart // rows
    items = jnp.where(counts > 0, (cend - 1) // rows - first_tile + 1, 0)
    wend = jnp.cumsum(items)
    wstart = wend - items
    n_work = wend[-1]
    w = jnp.arange(N_WORK_ITEMS, dtype=jnp.int32)
    valid = w < n_work
    last_expert = jnp.max(jnp.where(counts > 0, experts, 0))
    e_w = jnp.minimum(jnp.sum((wend[None, :] <= w[:, None]).astype(jnp.int32), axis=1), N_EXPERTS - 1)
    e_w = jnp.where(valid, e_w, last_expert)
    sel = (e_w[:, None] == experts[None, :]).astype(jnp.int32)
    pick = lambda v: jnp.sum(sel * v[None, :], axis=1)
    tile_w = jnp.where(valid, pick(first_tile) + w - pick(wstart), n_tiles - 1)
    lo_w = jnp.where(valid, jnp.clip(pick(cstart) - tile_w * rows, 0, rows), 0)
    hi_w = jnp.where(valid, jnp.clip(pick(cend) - tile_w * rows, 0, rows), 0)
    prev = lambda v: jnp.concatenate([jnp.full((1,), -1, jnp.int32), v[:-1]])
    new_expert = (e_w != prev(e_w)).astype(jnp.int32)
    new_tile = (tile_w != prev(tile_w)).astype(jnp.int32)
    later = jnp.logical_and(counts[None, :] > 0, experts[None, :] > experts[:, None])
    next_of = jnp.min(jnp.where(later, experts[None, :], N_EXPERTS), axis=1)
    next_of = jnp.where(next_of < N_EXPERTS, next_of, -1)
    buf_w = (jnp.cumsum(new_expert) - 1) % 2
    plan = jnp.stack([tile_w, e_w, new_expert, new_tile, lo_w, hi_w,
                      jnp.broadcast_to(n_work, w.shape), pick(next_of), buf_w]).astype(jnp.int32)
    return pos, plan


def _dispatch_kernel(pos_ref, x_ref, g_ref, sh_ref, sc_ref, xs_hbm, hbuf, sem):
    rows = DISPATCH_ROWS
    i = pl.program_id(0)
    n = pl.num_programs(0)
    slot = i % 2

    def row_copy(s, r, dst_row):
        return pltpu.make_async_copy(hbuf.at[s, pl.ds(r, 1), :], xs_hbm.at[pl.ds(dst_row, 1), :], sem.at[s])

    def drain(s):
        def body(r, c):
            row_copy(s, 0, 0).wait()
            return c
        lax.fori_loop(0, 2 * rows, body, 0, unroll=8)

    @pl.when(i >= 2)
    def _():
        drain(slot)

    hbuf[slot] = _rms_modulate(x_ref[...], g_ref[...], sh_ref[0], sc_ref[0])

    def issue(r, c):
        row_copy(slot, r, pos_ref[0, 0, r]).start()
        row_copy(slot, r, pos_ref[0, 0, rows + r]).start()
        return c
    lax.fori_loop(0, rows, issue, 0, unroll=8)

    @pl.when(i == n - 1)
    def _():
        drain(slot)
        drain(1 - slot)


def _dispatch(x1, g, mod3, shift_blk, scale_blk, pos3, n_sorted):
    rows = DISPATCH_ROWS
    m = x1.shape[0]
    assert m // rows >= 2
    return pl.pallas_call(
        _dispatch_kernel,
        grid=(m // rows,),
        in_specs=[
            pl.BlockSpec((1, 1, 2 * rows), lambda i: (i, 0, 0), memory_space=pltpu.SMEM),
            pl.BlockSpec((rows, D_MODEL), lambda i: (i, 0)),
            pl.BlockSpec((1, D_MODEL), lambda i: (0, 0)),
            pl.BlockSpec((1, 1, D_MODEL), lambda i: (i * rows // SEQ, 0, shift_blk)),
            pl.BlockSpec((1, 1, D_MODEL), lambda i: (i * rows // SEQ, 0, scale_blk)),
        ],
        out_specs=pl.BlockSpec(memory_space=pl.ANY),
        out_shape=jax.ShapeDtypeStruct((n_sorted, D_MODEL), F32),
        scratch_shapes=[pltpu.VMEM((2, rows, D_MODEL), F32), pltpu.SemaphoreType.DMA((2,))],
        compiler_params=_params("arbitrary"),
        name="dispatch",
    )(pos3, x1, g, mod3, mod3)


def _experts_kernel(plan_ref, x_ref, wg_hbm, wu_hbm, wd_hbm, y_ref,
                    wg_buf, wu_buf, wd_buf, sem, wgb_ref, wub_ref, wdb_ref):
    w = pl.program_id(0)

    def weight_copies(e, s):
        return (pltpu.make_async_copy(wg_hbm.at[e], wg_buf.at[s], sem.at[s, 0]),
                pltpu.make_async_copy(wu_hbm.at[e], wu_buf.at[s], sem.at[s, 1]),
                pltpu.make_async_copy(wd_hbm.at[e], wd_buf.at[s], sem.at[s, 2]))

    @pl.when(w == 0)
    def _():
        for cp in weight_copies(plan_ref[1, 0], 0):
            cp.start()

    @pl.when(w < plan_ref[6, w])
    def _():
        @pl.when(plan_ref[2, w] == 1)
        def _():
            s = plan_ref[8, w]

            @pl.when(plan_ref[7, w] >= 0)
            def _():
                for cp in weight_copies(plan_ref[7, w], 1 - s):
                    cp.start()

            for cp in weight_copies(plan_ref[1, w], s):
                cp.wait()
            wgb_ref[...] = wg_buf[s].astype(BF16)
            wub_ref[...] = wu_buf[s].astype(BF16)
            wdb_ref[...] = wd_buf[s].astype(BF16)

        x = x_ref[...].astype(BF16)
        hg = _dot(x, wgb_ref[...])
        hu = _dot(x, wub_ref[...])
        hid = _silu(hg) * hu
        y = _dot(hid.astype(BF16), wdb_ref[...])
        row = lax.broadcasted_iota(jnp.int32, (EXPERT_ROWS, 1), 0)
        mine = jnp.logical_and(row >= plan_ref[4, w], row < plan_ref[5, w])

        @pl.when(plan_ref[3, w] == 1)
        def _():
            y_ref[...] = jnp.where(mine, y, 0.0)

        @pl.when(plan_ref[3, w] == 0)
        def _():
            y_ref[...] = jnp.where(mine, y, y_ref[...])


def _experts(xs, plan, w_gate, w_up, w_down):
    rows = EXPERT_ROWS
    grid_spec = pltpu.PrefetchScalarGridSpec(
        num_scalar_prefetch=1,
        grid=(N_WORK_ITEMS,),
        in_specs=[
            pl.BlockSpec((rows, D_MODEL), lambda w, p: (p[0, w], 0)),
            pl.BlockSpec(memory_space=pl.ANY),
            pl.BlockSpec(memory_space=pl.ANY),
            pl.BlockSpec(memory_space=pl.ANY),
        ],
        out_specs=pl.BlockSpec((rows, D_MODEL), lambda w, p: (p[0, w], 0)),
        scratch_shapes=[
            pltpu.VMEM((2, D_MODEL, EXPERT_FF), F32),
            pltpu.VMEM((2, D_MODEL, EXPERT_FF), F32),
            pltpu.VMEM((2, EXPERT_FF, D_MODEL), F32),
            pltpu.SemaphoreType.DMA((2, 3)),
            pltpu.VMEM((D_MODEL, EXPERT_FF), BF16),
            pltpu.VMEM((D_MODEL, EXPERT_FF), BF16),
            pltpu.VMEM((EXPERT_FF, D_MODEL), BF16),
        ],
    )
    return pl.pallas_call(
        _experts_kernel,
        grid_spec=grid_spec,
        out_shape=jax.ShapeDtypeStruct(xs.shape, F32),
        compiler_params=_params("arbitrary"),
        name="experts",
    )(plan, xs, w_gate, w_up, w_down)


def _final_kernel(pos_ref, posn_ref, x_ref, r_ref, g2_ref, fg_ref, y_hbm, o_ref, ybuf, sem):
    rows = FINAL_ROWS
    i = pl.program_id(0)
    n = pl.num_programs(0)
    slot = i % 2

    def row_copy(src_row, s, r):
        return pltpu.make_async_copy(y_hbm.at[pl.ds(src_row, 1), :], ybuf.at[s, pl.ds(r, 1), :], sem.at[s])

    def issue(idx_ref, s):
        def body(r, c):
            row_copy(idx_ref[0, 0, r], s, r).start()
            return c
        lax.fori_loop(0, 2 * rows, body, 0, unroll=8)

    @pl.when(i == 0)
    def _():
        issue(pos_ref, 0)

    @pl.when(i + 1 < n)
    def _():
        issue(posn_ref, 1 - slot)

    def wait_body(r, c):
        row_copy(0, slot, r).wait()
        return c
    lax.fori_loop(0, 2 * rows, wait_body, 0, unroll=8)

    route = r_ref[...]
    w1 = route[:, 2:3]
    w2 = route[:, 3:4]
    moe = w1 * ybuf[slot, 0:rows, :] + w2 * ybuf[slot, rows:2 * rows, :]
    x = x_ref[...] + g2_ref[0] * moe
    o_ref[...] = x * lax.rsqrt(jnp.mean(x * x, axis=-1, keepdims=True) + EPS) * fg_ref[...]


def _final(x1, route, mod3, gate_blk, final_g, y, pos3):
    rows = FINAL_ROWS
    m = x1.shape[0]
    nt = m // rows
    return pl.pallas_call(
        _final_kernel,
        grid=(nt,),
        in_specs=[
            pl.BlockSpec((1, 1, 2 * rows), lambda i: (i, 0, 0), memory_space=pltpu.SMEM),
            pl.BlockSpec((1, 1, 2 * rows), lambda i: (jnp.minimum(i + 1, nt - 1), 0, 0), memory_space=pltpu.SMEM),
            pl.BlockSpec((rows, D_MODEL), lambda i: (i, 0)),
            pl.BlockSpec((rows, LANES), lambda i: (i, 0)),
            pl.BlockSpec((1, 1, D_MODEL), lambda i: (i * rows // SEQ, 0, gate_blk)),
            pl.BlockSpec((1, D_MODEL), lambda i: (0, 0)),
            pl.BlockSpec(memory_space=pl.ANY),
        ],
        out_specs=pl.BlockSpec((rows, D_MODEL), lambda i: (i, 0)),
        out_shape=jax.ShapeDtypeStruct((m, D_MODEL), F32),
        scratch_shapes=[pltpu.VMEM((2, 2 * rows, D_MODEL), F32), pltpu.SemaphoreType.DMA((2,))],
        compiler_params=_params("arbitrary"),
        name="final",
    )(pos3, pos3, x1, route, mod3, final_g, y)


def _rope_tables():
    rows = SEQ // GRID_W
    row = jnp.repeat(jnp.arange(rows, dtype=F32), GRID_W)
    col = jnp.tile(jnp.arange(GRID_W, dtype=F32), rows)
    n_freq = RET_DK // 4
    inv = ROPE_BASE ** (-jnp.arange(n_freq, dtype=F32) / n_freq)
    ang = jnp.concatenate([row[:, None] * inv, col[:, None] * inv], axis=-1)
    cos, sin = jnp.cos(ang), jnp.sin(ang)
    return jnp.concatenate([cos, cos], axis=-1), jnp.concatenate([-sin, sin], axis=-1)


def kernel(x, c, ctx, c_ctx, w_mod, b_mod, norm1_g, norm2_g, w_in, w_four_out, w_ret_out, w_out,
           ret_decay_f, ret_decay_b, w_group_router, b_group_router, w_expert_router, b_expert_router,
           w_gate, w_up, w_down, final_norm_g):
    b, n, d = x.shape
    assert (n, d) == (SEQ, D_MODEL) and ctx.shape[1] == CTX_LEN and w_mod.shape[0] == 1
    t = b * n

    c8 = jnp.zeros((8, d), F32).at[:b].set(c).at[b].set(c_ctx)
    mod3 = _modulation(c8, w_mod[0], b_mod).reshape(8, 1, N_MOD * d)

    hx = _prenorm(x, norm1_g, mod3, lambda i: i, 0, 1, rows=1024)
    hc = _prenorm(ctx, norm1_g, mod3, lambda i: b, 0, 1, rows=CTX_LEN)

    cos2, sin2 = _rope_tables()
    pc = _in_proj(hx.reshape(t, d), w_in[0], cos2, sin2)
    kvc = _project(hc.reshape(b * CTX_LEN, d), w_in[0], K_OFF // COL_TILE, (GF_OFF - K_OFF) // COL_TILE,
                   b * CTX_LEN, "ctx_proj")
    kvc3 = kvc.reshape(b, CTX_LEN, GF_OFF - K_OFF)

    fm = _fourier(pc, b)

    lg_f = jax.nn.log_sigmoid(ret_decay_f[0].astype(F32))
    lg_b = jax.nn.log_sigmoid(ret_decay_b[0].astype(F32))
    lg = jnp.stack([lg_f, lg_b, jnp.exp(RET_BLOCK * lg_f), jnp.exp(RET_BLOCK * lg_b)])
    ret = _retention(pc, kvc3, lg)

    merged = _merge(fm.reshape(t, FOUR_WIDTH), ret.reshape(t, d), w_four_out[0], w_ret_out[0], pc)
    x1 = _out_proj(merged, w_out[0], x.reshape(t, d), mod3, 2 * d // COL_TILE)

    w_router = jnp.concatenate(
        [w_group_router[0], w_expert_router[0].transpose(1, 0, 2).reshape(d, N_EXPERTS)], axis=1).astype(F32)
    w_router = jnp.pad(w_router, ((0, 0), (0, LANES - w_router.shape[1])))
    w_hi = w_router.astype(BF16)
    w_lo = (w_router - w_hi.astype(F32)).astype(BF16)
    bias = jnp.pad(jnp.concatenate([b_group_router[0], b_expert_router[0].reshape(-1)]).astype(F32),
                   (0, LANES - N_GROUPS - N_EXPERTS)).reshape(1, LANES)
    route = _route(x1, norm2_g, mod3, 3, 4, w_hi, w_lo, bias)

    pos, plan = _dispatch_plan(route)
    assert DISPATCH_ROWS == FINAL_ROWS
    pos3 = pos.reshape(t // FINAL_ROWS, FINAL_ROWS, 2).transpose(0, 2, 1).reshape(t // FINAL_ROWS, 1, 2 * FINAL_ROWS)
    xs = _dispatch(x1, norm2_g, mod3, 3, 4, pos3, 2 * t)
    y = _experts(xs, plan, w_gate[0], w_up[0], w_down[0])
    out = _final(x1, route, mod3, 5, final_norm_g.reshape(1, d), y, pos3)
    return out.reshape(b, n, d)
```

```python
import functools

import jax
import jax.numpy as jnp
import numpy as np
from jax import lax
from jax.experimental import pallas as pl
from jax.experimental.pallas import tpu as pltpu

F32 = jnp.float32
BF16 = jnp.bfloat16

D_MODEL = 2048
SEQ = 2048
CTX_LEN = 256
GRID_W = 64
N_MOD = 6
FOUR_GROUPS = 8
FOUR_GROUP_DIM = 128
FOUR_WIDTH = FOUR_GROUPS * FOUR_GROUP_DIM
RET_HEADS = 8
RET_DK = 128
RET_DV = 256
ROPE_BASE = 10000.0
N_GROUPS = 4
EXPERTS_PER_GROUP = 8
N_EXPERTS = N_GROUPS * EXPERTS_PER_GROUP
EXPERT_FF = 512
EPS = 1e-6
Q_OFF = FOUR_WIDTH
K_OFF = Q_OFF + RET_HEADS * RET_DK
V_OFF = K_OFF + RET_HEADS * RET_DK
GF_OFF = V_OFF + RET_HEADS * RET_DV
GB_OFF = GF_OFF + RET_HEADS * RET_DV
MF_OFF = GB_OFF + RET_HEADS * RET_DV
MR_OFF = MF_OFF + D_MODEL
IN_WIDTH = MR_OFF + D_MODEL

V7X_VMEM_LIMIT_BYTES = 56 * 1024 * 1024
LANES = 128
RET_BLOCK = 256
COL_TILE = 1024
MERGE_COL_TILE = 512
IN_COL_TILE = 1024
IN_ROWS = 1024
EXPERT_ROWS = 256
DISPATCH_ROWS = 256
FINAL_ROWS = 256
N_WORK_ITEMS = 2 * 4 * SEQ // EXPERT_ROWS + N_EXPERTS


def _params(*sem):
    return pltpu.CompilerParams(dimension_semantics=sem, vmem_limit_bytes=V7X_VMEM_LIMIT_BYTES)


def _dot(a, b):
    return jnp.dot(a, b, preferred_element_type=F32)


def _mod_kernel(c_ref, w_ref, b_ref, o_ref):
    c = c_ref[...]
    sc = c * jax.nn.sigmoid(c)
    o_ref[...] = _dot(sc.astype(BF16), w_ref[...].astype(BF16)) + b_ref[...]


def _modulation(c8, w_mod, b_mod):
    n_out = w_mod.shape[1]
    return pl.pallas_call(
        _mod_kernel,
        grid=(n_out // COL_TILE,),
        in_specs=[
            pl.BlockSpec((8, D_MODEL), lambda j: (0, 0)),
            pl.BlockSpec((D_MODEL, COL_TILE), lambda j: (0, j)),
            pl.BlockSpec((1, COL_TILE), lambda j: (0, j)),
        ],
        out_specs=pl.BlockSpec((8, COL_TILE), lambda j: (0, j)),
        out_shape=jax.ShapeDtypeStruct((8, n_out), F32),
        compiler_params=_params("arbitrary"),
        name="mod",
    )(c8, w_mod, b_mod)


def _rms_modulate(x, g, shift, scale):
    y = x * lax.rsqrt(jnp.mean(x * x, axis=-1, keepdims=True) + EPS) * g
    return y * (1.0 + scale) + shift


def _prenorm_kernel(x_ref, g_ref, sh_ref, sc_ref, o_ref):
    o_ref[0] = _rms_modulate(x_ref[0], g_ref[...], sh_ref[0], sc_ref[0]).astype(o_ref.dtype)


def _prenorm(x, g, mod3, row_of_batch, shift_blk, scale_blk, rows):
    b, n, d = x.shape
    return pl.pallas_call(
        _prenorm_kernel,
        grid=(b, n // rows),
        in_specs=[
            pl.BlockSpec((1, rows, d), lambda i, j: (i, j, 0)),
            pl.BlockSpec((1, d), lambda i, j: (0, 0)),
            pl.BlockSpec((1, 1, d), lambda i, j: (row_of_batch(i), 0, shift_blk)),
            pl.BlockSpec((1, 1, d), lambda i, j: (row_of_batch(i), 0, scale_blk)),
        ],
        out_specs=pl.BlockSpec((1, rows, d), lambda i, j: (i, j, 0)),
        out_shape=jax.ShapeDtypeStruct((b, n, d), BF16),
        compiler_params=_params("arbitrary", "arbitrary"),
        name="prenorm",
    )(x, g, mod3, mod3)


def _proj_kernel(a_ref, w_ref, o_ref, wb_ref):
    @pl.when(pl.program_id(1) == 0)
    def _():
        wb_ref[...] = w_ref[...].astype(BF16)

    o_ref[...] = _dot(a_ref[...], wb_ref[...]).astype(o_ref.dtype)


def _project(a, w, col_tile0, n_col_tiles, rows, name):
    m, k = a.shape
    return pl.pallas_call(
        _proj_kernel,
        grid=(n_col_tiles, m // rows),
        in_specs=[
            pl.BlockSpec((rows, k), lambda j, i: (i, 0)),
            pl.BlockSpec((k, COL_TILE), lambda j, i: (0, j + col_tile0)),
        ],
        out_specs=pl.BlockSpec((rows, COL_TILE), lambda j, i: (i, j)),
        out_shape=jax.ShapeDtypeStruct((m, n_col_tiles * COL_TILE), BF16),
        scratch_shapes=[pltpu.VMEM((k, COL_TILE), BF16)],
        compiler_params=_params("arbitrary", "arbitrary"),
        name=name,
    )(a, w)


def _rope(t, cos2, sin2):
    return t * cos2 + pltpu.roll(t, RET_DK // 2, axis=1) * sin2


def _in_proj_kernel(a_ref, w_ref, cos_ref, sin_ref, o_ref, wb_ref, acc_ref):
    j = pl.program_id(0)

    @pl.when(pl.program_id(1) == 0)
    def _():
        wb_ref[...] = w_ref[...].astype(BF16)

    q_tile, k_tile, v_tile = Q_OFF // IN_COL_TILE, K_OFF // IN_COL_TILE, V_OFF // IN_COL_TILE
    gate_tile, merge_tile = GF_OFF // IN_COL_TILE, MF_OFF // IN_COL_TILE
    is_qk = jnp.logical_and(j >= q_tile, j < v_tile)
    is_gate = jnp.logical_and(j >= gate_tile, j < merge_tile)

    def store_chunks(val):
        for c in range(IN_COL_TILE // LANES):
            o_ref[c] = val[:, c * LANES:(c + 1) * LANES].astype(o_ref.dtype)

    is_four = j < q_tile

    @pl.when(jnp.logical_not(jnp.logical_or(jnp.logical_or(is_qk, is_gate), is_four)))
    def _():
        store_chunks(_dot(a_ref[...], wb_ref[...]))

    @pl.when(is_four)
    def _():
        acc = _dot(a_ref[...], wb_ref[...])
        half = acc_ref.shape[1] // 2
        for c in range(IN_COL_TILE // LANES):
            acc_ref[c] = acc[:, c * LANES:(c + 1) * LANES]
            for parity in (0, 1):
                val = acc_ref[c, pl.ds(parity, half, stride=2), :]
                o_ref[c, parity * half:(parity + 1) * half, :] = val.astype(o_ref.dtype)

    @pl.when(is_qk)
    def _():
        acc = _dot(a_ref[...], wb_ref[...])
        scale = jnp.where(j < k_tile, RET_DK ** -0.5, 1.0).astype(F32)
        cos2, sin2 = cos_ref[...], sin_ref[...]
        assert RET_DK == LANES
        for h in range(IN_COL_TILE // RET_DK):
            cols = slice(h * RET_DK, (h + 1) * RET_DK)
            o_ref[h] = (_rope(acc[:, cols], cos2, sin2) * scale).astype(o_ref.dtype)

    @pl.when(is_gate)
    def _():
        store_chunks(_silu(_dot(a_ref[...], wb_ref[...])))


def _in_proj(a, w, cos2, sin2, rows=IN_ROWS):
    m, k = a.shape
    n_col_tiles = IN_WIDTH // IN_COL_TILE
    pos_blocks = SEQ // rows
    return pl.pallas_call(
        _in_proj_kernel,
        grid=(n_col_tiles, m // rows),
        in_specs=[
            pl.BlockSpec((rows, k), lambda j, i: (i, 0)),
            pl.BlockSpec((k, IN_COL_TILE), lambda j, i: (0, j)),
            pl.BlockSpec((rows, RET_DK), lambda j, i: (i % pos_blocks, 0)),
            pl.BlockSpec((rows, RET_DK), lambda j, i: (i % pos_blocks, 0)),
        ],
        out_specs=pl.BlockSpec((IN_COL_TILE // LANES, rows, LANES), lambda j, i: (j, i, 0)),
        out_shape=jax.ShapeDtypeStruct((IN_WIDTH // LANES, m, LANES), BF16),
        scratch_shapes=[pltpu.VMEM((k, IN_COL_TILE), BF16),
                        pltpu.VMEM((IN_COL_TILE // LANES, rows, LANES), F32)],
        compiler_params=_params("arbitrary", "arbitrary"),
        name="in_proj",
    )(a, w, cos2, sin2)


def _dft_tables():
    half = SEQ // 2
    k = np.arange(half, dtype=np.int64)[:, None]
    n2 = np.arange(half, dtype=np.int64)[None, :]
    tables = []
    for parity in (0, 1):
        ang = 2.0 * np.pi * ((k * (2 * n2 + parity)) % SEQ) / SEQ
        tables.append(np.concatenate([np.cos(ang), -np.sin(ang)], axis=1).astype(np.float32))
    c = np.arange(FOUR_GROUP_DIM, dtype=np.int64)
    ang_c = 2.0 * np.pi * ((c[:, None] * c[None, :]) % FOUR_GROUP_DIM) / FOUR_GROUP_DIM
    norm = 1.0 / np.sqrt(float(SEQ * FOUR_GROUP_DIM))
    chan = (np.concatenate([np.cos(ang_c), np.sin(ang_c)], axis=1) * norm).astype(np.float32)
    return tables[0], tables[1], chan


def _fourier_kernel(u_ref, chan_ref, even_ref, odd_ref, o_ref, y_ref):
    half = SEQ // 2
    tile_half = IN_ROWS // 2
    chan = chan_ref[...]
    for g in range(FOUR_GROUPS):
        cols = slice(g * FOUR_GROUP_DIM, (g + 1) * FOUR_GROUP_DIM)
        y = _dot(u_ref[g], chan).astype(BF16)
        for t in range(SEQ // IN_ROWS):
            for parity in (0, 1):
                src = slice(t * IN_ROWS + parity * tile_half, t * IN_ROWS + (parity + 1) * tile_half)
                dst = 2 * parity * half + t * tile_half
                y_ref[dst:dst + tile_half, cols] = y[src, :FOUR_GROUP_DIM]
                y_ref[half + dst:half + dst + tile_half, cols] = y[src, FOUR_GROUP_DIM:]
    e = _dot(even_ref[...], y_ref[0:SEQ, :])
    o = _dot(odd_ref[...], y_ref[SEQ:2 * SEQ, :])
    o_ref[0, 0:half, :] = (e + o).astype(o_ref.dtype)
    o_ref[0, half:SEQ, :] = (e - o).astype(o_ref.dtype)


def _fourier(pc, b):
    assert FOUR_GROUP_DIM == LANES and FOUR_WIDTH == IN_COL_TILE and SEQ % IN_ROWS == 0
    even_np, odd_np, chan_np = _dft_tables()
    even = jnp.asarray(even_np).astype(BF16)
    odd = jnp.asarray(odd_np).astype(BF16)
    chan = jnp.asarray(chan_np).astype(BF16)
    return pl.pallas_call(
        _fourier_kernel,
        grid=(b,),
        in_specs=[
            pl.BlockSpec((FOUR_GROUPS, SEQ, LANES), lambda i: (0, i, 0)),
            pl.BlockSpec((FOUR_GROUP_DIM, 2 * FOUR_GROUP_DIM), lambda i: (0, 0)),
            pl.BlockSpec((SEQ // 2, SEQ), lambda i: (0, 0)),
            pl.BlockSpec((SEQ // 2, SEQ), lambda i: (0, 0)),
        ],
        out_specs=pl.BlockSpec((1, SEQ, FOUR_WIDTH), lambda i: (i, 0, 0)),
        out_shape=jax.ShapeDtypeStruct((b, SEQ, FOUR_WIDTH), BF16),
        scratch_shapes=[pltpu.VMEM((2 * SEQ, FOUR_WIDTH), BF16)],
        compiler_params=_params("arbitrary"),
        name="fourier",
    )(pc, chan, even, odd)


def _head_norm(o):
    mu = jnp.mean(o, axis=-1, keepdims=True)
    d = o - mu
    var = jnp.mean(d * d, axis=-1, keepdims=True)
    return d * lax.rsqrt(var + EPS)


def _silu(x):
    return x * jax.nn.sigmoid(x)


def _retention_kernel(lg_ref, q_ref, k_ref, v_ref, gf_ref, gb_ref, kc_ref, vc_ref,
                      o_ref, acc_ref, df_ref, db_ref, xif_ref, xib_ref, zf_ref, zb_ref):
    blk = RET_BLOCK
    n_blk = SEQ // blk
    h = pl.program_id(0)
    lgf, lgb, gchf, gchb = lg_ref[0, h], lg_ref[1, h], lg_ref[2, h], lg_ref[3, h]

    @pl.when(pl.program_id(1) == 0)
    def _():
        row = lax.broadcasted_iota(jnp.int32, (blk, blk), 0)
        col = lax.broadcasted_iota(jnp.int32, (blk, blk), 1)
        diff = (row - col).astype(F32)
        df_ref[...] = jnp.where(diff >= 0, jnp.exp(jnp.maximum(diff, 0.0) * lgf), 0.0)
        db_ref[...] = jnp.where(diff <= 0, jnp.exp(jnp.maximum(-diff, 0.0) * lgb), 0.0)
        pos_v = lax.broadcasted_iota(jnp.int32, (blk, RET_DV), 0).astype(F32)
        xif_ref[...] = jnp.exp((pos_v + 1.0) * lgf)
        xib_ref[...] = jnp.exp((blk - pos_v) * lgb)
        pos_k = lax.broadcasted_iota(jnp.int32, (blk, RET_DK), 0).astype(F32)
        zf_ref[...] = jnp.exp((blk - 1.0 - pos_k) * lgf)
        zb_ref[...] = jnp.exp(pos_k * lgb)

    def state_update(r, k, z_ref, v, gch):
        kz = (k.astype(F32) * z_ref[...]).astype(BF16)
        return gch * r + lax.dot_general(kz, v, (((0,), (0,)), ((), ())), preferred_element_type=F32)

    def block_out(qb, kb, vb, r, d_ref, xi_ref):
        s = lax.dot_general(qb, kb, (((1,), (1,)), ((), ())), preferred_element_type=F32)
        inner = _dot((s * d_ref[...]).astype(BF16), vb)
        cross = _dot(qb, r.astype(BF16)) * xi_ref[...]
        return inner + cross

    def wide(ref, rows):
        return jnp.concatenate([ref[0, rows, :], ref[1, rows, :]], axis=1)

    zero = jnp.zeros((RET_DK, RET_DV), F32)
    r_f = state_update(zero, kc_ref[0], zf_ref, vc_ref[0], gchf)
    r_b = state_update(zero, kc_ref[0], zb_ref, vc_ref[0], gchb)

    def emit(i, part, reached_first):
        rows = slice(i * blk, (i + 1) * blk)
        if reached_first:
            acc_ref[rows, :] = part
        else:
            o_ref[0, rows, :] = (acc_ref[rows, :] + part).astype(o_ref.dtype)

    for j in range(n_blk):
        i_f, i_b = j, n_blk - 1 - j
        rows_f = slice(i_f * blk, (i_f + 1) * blk)
        rows_b = slice(i_b * blk, (i_b + 1) * blk)
        qf, kf, vf = q_ref[0, rows_f, :], k_ref[0, rows_f, :], wide(v_ref, rows_f)
        qk, kk, vk = q_ref[0, rows_b, :], k_ref[0, rows_b, :], wide(v_ref, rows_b)
        o_f = block_out(qf, kf, vf, r_f, df_ref, xif_ref)
        o_b = block_out(qk, kk, vk, r_b, db_ref, xib_ref)
        r_f = state_update(r_f, kf, zf_ref, vf, gchf)
        r_b = state_update(r_b, kk, zb_ref, vk, gchb)
        first_visit = j < n_blk - 1 - j
        emit(i_f, wide(gf_ref, rows_f).astype(F32) * _head_norm(o_f), first_visit)
        emit(i_b, wide(gb_ref, rows_b).astype(F32) * _head_norm(o_b), first_visit)


def _retention(pc, kvc3, lg):
    b = kvc3.shape[0]
    assert CTX_LEN == RET_BLOCK and (SEQ // RET_BLOCK) % 2 == 0 and RET_DK == LANES and RET_DV == 2 * LANES
    qb0, kb0 = Q_OFF // RET_DK, K_OFF // RET_DK
    vb0, gfb0, gbb0 = V_OFF // RET_DV, GF_OFF // RET_DV, GB_OFF // RET_DV
    kcv0 = (RET_HEADS * RET_DK) // RET_DV
    return pl.pallas_call(
        _retention_kernel,
        grid=(RET_HEADS, b),
        in_specs=[
            pl.BlockSpec(memory_space=pltpu.SMEM),
            pl.BlockSpec((1, SEQ, LANES), lambda h, i: (qb0 + h, i, 0)),
            pl.BlockSpec((1, SEQ, LANES), lambda h, i: (kb0 + h, i, 0)),
            pl.BlockSpec((RET_DV // LANES, SEQ, LANES), lambda h, i: (vb0 + h, i, 0)),
            pl.BlockSpec((RET_DV // LANES, SEQ, LANES), lambda h, i: (gfb0 + h, i, 0)),
            pl.BlockSpec((RET_DV // LANES, SEQ, LANES), lambda h, i: (gbb0 + h, i, 0)),
            pl.BlockSpec((1, CTX_LEN, RET_DK), lambda h, i: (i, 0, h)),
            pl.BlockSpec((1, CTX_LEN, RET_DV), lambda h, i: (i, 0, kcv0 + h)),
        ],
        out_specs=pl.BlockSpec((1, SEQ, RET_DV), lambda h, i: (i, 0, h)),
        out_shape=jax.ShapeDtypeStruct((b, SEQ, RET_HEADS * RET_DV), BF16),
        scratch_shapes=[
            pltpu.VMEM((SEQ, RET_DV), F32),
            pltpu.VMEM((RET_BLOCK, RET_BLOCK), F32),
            pltpu.VMEM((RET_BLOCK, RET_BLOCK), F32),
            pltpu.VMEM((RET_BLOCK, RET_DV), F32),
            pltpu.VMEM((RET_BLOCK, RET_DV), F32),
            pltpu.VMEM((RET_BLOCK, RET_DK), F32),
            pltpu.VMEM((RET_BLOCK, RET_DK), F32),
        ],
        compiler_params=_params("arbitrary", "arbitrary"),
        name="retention",
    )(lg, pc, pc, pc, pc, pc, kvc3, kvc3)


def _merge_kernel(fm_ref, ret_ref, wf_ref, wr_ref, mf_ref, mr_ref, o_ref, wfb_ref, wrb_ref):
    @pl.when(pl.program_id(1) == 0)
    def _():
        wfb_ref[...] = wf_ref[...].astype(BF16)
        wrb_ref[...] = wr_ref[...].astype(BF16)

    four = _dot(fm_ref[...], wfb_ref[...])
    ret = _dot(ret_ref[...], wrb_ref[...])
    for c in range(MERGE_COL_TILE // LANES):
        cols = slice(c * LANES, (c + 1) * LANES)
        merged = (jax.nn.sigmoid(mf_ref[c].astype(F32)) * four[:, cols]
                  + jax.nn.sigmoid(mr_ref[c].astype(F32)) * ret[:, cols])
        o_ref[:, cols] = merged.astype(o_ref.dtype)


def _merge(fm2, ret2, w_four_out, w_ret_out, pc, rows=1024):
    m = fm2.shape[0]
    tile = MERGE_COL_TILE
    mf0, mr0 = MF_OFF // tile, MR_OFF // tile
    chunks = tile // LANES
    return pl.pallas_call(
        _merge_kernel,
        grid=(D_MODEL // tile, m // rows),
        in_specs=[
            pl.BlockSpec((rows, FOUR_WIDTH), lambda j, i: (i, 0)),
            pl.BlockSpec((rows, D_MODEL), lambda j, i: (i, 0)),
            pl.BlockSpec((FOUR_WIDTH, tile), lambda j, i: (0, j)),
            pl.BlockSpec((D_MODEL, tile), lambda j, i: (0, j)),
            pl.BlockSpec((chunks, rows, LANES), lambda j, i: (mf0 + j, i, 0)),
            pl.BlockSpec((chunks, rows, LANES), lambda j, i: (mr0 + j, i, 0)),
        ],
        out_specs=pl.BlockSpec((rows, tile), lambda j, i: (i, j)),
        out_shape=jax.ShapeDtypeStruct((m, D_MODEL), BF16),
        scratch_shapes=[pltpu.VMEM((FOUR_WIDTH, tile), BF16), pltpu.VMEM((D_MODEL, tile), BF16)],
        compiler_params=_params("arbitrary", "arbitrary"),
        name="merge",
    )(fm2, ret2, w_four_out, w_ret_out, pc, pc)


def _out_proj_kernel(a_ref, w_ref, x_ref, g_ref, o_ref, wb_ref):
    @pl.when(pl.program_id(1) == 0)
    def _():
        wb_ref[...] = w_ref[...].astype(BF16)

    o_ref[...] = x_ref[...] + g_ref[0] * _dot(a_ref[...], wb_ref[...])


def _out_proj(merged, w_out, x2, mod3, gate_blk0, rows=1024):
    m = merged.shape[0]
    return pl.pallas_call(
        _out_proj_kernel,
        grid=(D_MODEL // COL_TILE, m // rows),
        in_specs=[
            pl.BlockSpec((rows, D_MODEL), lambda j, i: (i, 0)),
            pl.BlockSpec((D_MODEL, COL_TILE), lambda j, i: (0, j)),
            pl.BlockSpec((rows, COL_TILE), lambda j, i: (i, j)),
            pl.BlockSpec((1, 1, COL_TILE), lambda j, i: (i * rows // SEQ, 0, gate_blk0 + j)),
        ],
        out_specs=pl.BlockSpec((rows, COL_TILE), lambda j, i: (i, j)),
        out_shape=jax.ShapeDtypeStruct((m, D_MODEL), F32),
        scratch_shapes=[pltpu.VMEM((D_MODEL, COL_TILE), BF16)],
        compiler_params=_params("arbitrary", "arbitrary"),
        name="out_proj",
    )(merged, w_out, x2, mod3)


def _route_kernel(x_ref, g_ref, sh_ref, sc_ref, whi_ref, wlo_ref, b_ref, r_ref):
    h = _rms_modulate(x_ref[...], g_ref[...], sh_ref[0], sc_ref[0])
    hi = h.astype(BF16)
    lo = (h - hi.astype(F32)).astype(BF16)
    logits = _dot(hi, whi_ref[...]) + _dot(lo, whi_ref[...]) + _dot(hi, wlo_ref[...]) + b_ref[...]

    lane = lax.broadcasted_iota(jnp.int32, logits.shape, 1).astype(F32)
    neg = -jnp.inf
    first = lambda hit: jnp.min(jnp.where(hit, lane, float(LANES)), axis=1, keepdims=True)
    is_grp = lane < float(N_GROUPS)
    gl = jnp.where(is_grp, logits, neg)
    gmax = jnp.max(gl, axis=1, keepdims=True)
    grp = first(gl == gmax)
    g_w = 1.0 / jnp.sum(jnp.where(is_grp, jnp.exp(logits - gmax), 0.0), axis=1, keepdims=True)
    lo_lane = float(N_GROUPS) + grp * float(EXPERTS_PER_GROUP)
    el = jnp.where(lane >= lo_lane, jnp.where(lane < lo_lane + float(EXPERTS_PER_GROUP), logits, neg), neg)
    v1 = jnp.max(el, axis=1, keepdims=True)
    i1 = first(el == v1)
    el2 = jnp.where(lane == i1, neg, el)
    v2 = jnp.max(el2, axis=1, keepdims=True)
    i2 = first(el2 == v2)
    e = jnp.exp(v2 - v1)
    w1 = g_w / (1.0 + e)
    w2 = g_w * e / (1.0 + e)
    out = jnp.where(lane == 0.0, i1 - float(N_GROUPS),
                    jnp.where(lane == 1.0, i2 - float(N_GROUPS),
                              jnp.where(lane == 2.0, w1, jnp.where(lane == 3.0, w2, 0.0))))
    r_ref[...] = out


def _route(x1, g, mod3, shift_blk, scale_blk, w_hi, w_lo, bias, rows=512):
    m = x1.shape[0]
    return pl.pallas_call(
        _route_kernel,
        grid=(m // rows,),
        in_specs=[
            pl.BlockSpec((rows, D_MODEL), lambda i: (i, 0)),
            pl.BlockSpec((1, D_MODEL), lambda i: (0, 0)),
            pl.BlockSpec((1, 1, D_MODEL), lambda i: (i * rows // SEQ, 0, shift_blk)),
            pl.BlockSpec((1, 1, D_MODEL), lambda i: (i * rows // SEQ, 0, scale_blk)),
            pl.BlockSpec((D_MODEL, LANES), lambda i: (0, 0)),
            pl.BlockSpec((D_MODEL, LANES), lambda i: (0, 0)),
            pl.BlockSpec((1, LANES), lambda i: (0, 0)),
        ],
        out_specs=pl.BlockSpec((rows, LANES), lambda i: (i, 0)),
        out_shape=jax.ShapeDtypeStruct((m, LANES), F32),
        compiler_params=_params("arbitrary"),
        name="route",
    )(x1, g, mod3, mod3, w_hi, w_lo, bias)


def _dispatch_plan(route):
    rows = EXPERT_ROWS
    flat = route[:, :2].astype(jnp.int32).reshape(-1)
    n_tiles = flat.shape[0] // rows
    experts = jnp.arange(N_EXPERTS, dtype=jnp.int32)
    onehot = (flat[:, None] == experts[None, :]).astype(jnp.int32)
    csum = jnp.cumsum(onehot, axis=0)
    rank = jnp.sum(csum * onehot, axis=1) - 1
    counts = csum[-1]
    cend = jnp.cumsum(counts)
    cstart = cend - counts
    pos = jnp.sum(cstart[None, :] * onehot, axis=1) + rank
    first_tile = cstart // rows
    items = jnp.where(counts > 0, (cend - 1) // rows - first_tile + 1, 0)
    wend = jnp.cumsum(items)
    wstart = wend - items
    n_work = wend[-1]
    w = jnp.arange(N_WORK_ITEMS, dtype=jnp.int32)
    valid = w < n_work
    last_expert = jnp.max(jnp.where(counts > 0, experts, 0))
    e_w = jnp.minimum(jnp.sum((wend[None, :] <= w[:, None]).astype(jnp.int32), axis=1), N_EXPERTS - 1)
    e_w = jnp.where(valid, e_w, last_expert)
    sel = (e_w[:, None] == experts[None, :]).astype(jnp.int32)
    pick = lambda v: jnp.sum(sel * v[None, :], axis=1)
    tile_w = jnp.where(valid, pick(first_tile) + w - pick(wstart), n_tiles - 1)
    lo_w = jnp.where(valid, jnp.clip(pick(cstart) - tile_w * rows, 0, rows), 0)
    hi_w = jnp.where(valid, jnp.clip(pick(cend) - tile_w * rows, 0, rows), 0)
    prev = lambda v: jnp.concatenate([jnp.full((1,), -1, jnp.int32), v[:-1]])
    new_expert = (e_w != prev(e_w)).astype(jnp.int32)
    new_tile = (tile_w != prev(tile_w)).astype(jnp.int32)
    later = jnp.logical_and(counts[None, :] > 0, experts[None, :] > experts[:, None])
    next_of = jnp.min(jnp.where(later, experts[None, :], N_EXPERTS), axis=1)
    next_of = jnp.where(next_of < N_EXPERTS, next_of, -1)
    buf_w = (jnp.cumsum(new_expert) - 1) % 2
    plan = jnp.stack([tile_w, e_w, new_expert, new_tile, lo_w, hi_w,
                      jnp.broadcast_to(n_work, w.shape), pick(next_of), buf_w]).astype(jnp.int32)
    return pos, plan


def _dispatch_kernel(pos_ref, x_ref, g_ref, sh_ref, sc_ref, xs_hbm, hbuf, sem):
    rows = DISPATCH_ROWS
    i = pl.program_id(0)
    n = pl.num_programs(0)
    slot = i % 2

    def row_copy(s, r, dst_row):
        return pltpu.make_async_copy(hbuf.at[s, pl.ds(r, 1), :], xs_hbm.at[pl.ds(dst_row, 1), :], sem.at[s])

    def drain(s):
        def body(r, c):
            row_copy(s, 0, 0).wait()
            return c
        lax.fori_loop(0, 2 * rows, body, 0, unroll=8)

    @pl.when(i >= 2)
    def _():
        drain(slot)

    hbuf[slot] = _rms_modulate(x_ref[...], g_ref[...], sh_ref[0], sc_ref[0])

    def issue(r, c):
        row_copy(slot, r, pos_ref[0, 0, r]).start()
        row_copy(slot, r, pos_ref[0, 0, rows + r]).start()
        return c
    lax.fori_loop(0, rows, issue, 0, unroll=8)

    @pl.when(i == n - 1)
    def _():
        drain(slot)
        drain(1 - slot)


def _dispatch(x1, g, mod3, shift_blk, scale_blk, pos3, n_sorted):
    rows = DISPATCH_ROWS
    m = x1.shape[0]
    assert m // rows >= 2
    return pl.pallas_call(
        _dispatch_kernel,
        grid=(m // rows,),
        in_specs=[
            pl.BlockSpec((1, 1, 2 * rows), lambda i: (i, 0, 0), memory_space=pltpu.SMEM),
            pl.BlockSpec((rows, D_MODEL), lambda i: (i, 0)),
            pl.BlockSpec((1, D_MODEL), lambda i: (0, 0)),
            pl.BlockSpec((1, 1, D_MODEL), lambda i: (i * rows // SEQ, 0, shift_blk)),
            pl.BlockSpec((1, 1, D_MODEL), lambda i: (i * rows // SEQ, 0, scale_blk)),
        ],
        out_specs=pl.BlockSpec(memory_space=pl.ANY),
        out_shape=jax.ShapeDtypeStruct((n_sorted, D_MODEL), F32),
        scratch_shapes=[pltpu.VMEM((2, rows, D_MODEL), F32), pltpu.SemaphoreType.DMA((2,))],
        compiler_params=_params("arbitrary"),
        name="dispatch",
    )(pos3, x1, g, mod3, mod3)


def _experts_kernel(plan_ref, x_ref, wg_hbm, wu_hbm, wd_hbm, y_ref,
                    wg_buf, wu_buf, wd_buf, sem, wgb_ref, wub_ref, wdb_ref):
    w = pl.program_id(0)

    def weight_copies(e, s):
        return (pltpu.make_async_copy(wg_hbm.at[e], wg_buf.at[s], sem.at[s, 0]),
                pltpu.make_async_copy(wu_hbm.at[e], wu_buf.at[s], sem.at[s, 1]),
                pltpu.make_async_copy(wd_hbm.at[e], wd_buf.at[s], sem.at[s, 2]))

    @pl.when(w == 0)
    def _():
        for cp in weight_copies(plan_ref[1, 0], 0):
            cp.start()

    @pl.when(w < plan_ref[6, w])
    def _():
        @pl.when(plan_ref[2, w] == 1)
        def _():
            s = plan_ref[8, w]

            @pl.when(plan_ref[7, w] >= 0)
            def _():
                for cp in weight_copies(plan_ref[7, w], 1 - s):
                    cp.start()

            for cp in weight_copies(plan_ref[1, w], s):
                cp.wait()
            wgb_ref[...] = wg_buf[s].astype(BF16)
            wub_ref[...] = wu_buf[s].astype(BF16)
            wdb_ref[...] = wd_buf[s].astype(BF16)

        x = x_ref[...].astype(BF16)
        hg = _dot(x, wgb_ref[...])
        hu = _dot(x, wub_ref[...])
        hid = _silu(hg) * hu
        y = _dot(hid.astype(BF16), wdb_ref[...])
        row = lax.broadcasted_iota(jnp.int32, (EXPERT_ROWS, 1), 0)
        mine = jnp.logical_and(row >= plan_ref[4, w], row < plan_ref[5, w])

        @pl.when(plan_ref[3, w] == 1)
        def _():
            y_ref[...] = jnp.where(mine, y, 0.0)

        @pl.when(plan_ref[3, w] == 0)
        def _():
            y_ref[...] = jnp.where(mine, y, y_ref[...])


def _experts(xs, plan, w_gate, w_up, w_down):
    rows = EXPERT_ROWS
    grid_spec = pltpu.PrefetchScalarGridSpec(
        num_scalar_prefetch=1,
        grid=(N_WORK_ITEMS,),
        in_specs=[
            pl.BlockSpec((rows, D_MODEL), lambda w, p: (p[0, w], 0)),
            pl.BlockSpec(memory_space=pl.ANY),
            pl.BlockSpec(memory_space=pl.ANY),
            pl.BlockSpec(memory_space=pl.ANY),
        ],
        out_specs=pl.BlockSpec((rows, D_MODEL), lambda w, p: (p[0, w], 0)),
        scratch_shapes=[
            pltpu.VMEM((2, D_MODEL, EXPERT_FF), F32),
            pltpu.VMEM((2, D_MODEL, EXPERT_FF), F32),
            pltpu.VMEM((2, EXPERT_FF, D_MODEL), F32),
            pltpu.SemaphoreType.DMA((2, 3)),
            pltpu.VMEM((D_MODEL, EXPERT_FF), BF16),
            pltpu.VMEM((D_MODEL, EXPERT_FF), BF16),
            pltpu.VMEM((EXPERT_FF, D_MODEL), BF16),
        ],
    )
    return pl.pallas_call(
        _experts_kernel,
        grid_spec=grid_spec,
        out_shape=jax.ShapeDtypeStruct(xs.shape, F32),
        compiler_params=_params("arbitrary"),
        name="experts",
    )(plan, xs, w_gate, w_up, w_down)


def _final_kernel(pos_ref, posn_ref, x_ref, r_ref, g2_ref, fg_ref, y_hbm, o_ref, ybuf, sem):
    rows = FINAL_ROWS
    i = pl.program_id(0)
    n = pl.num_programs(0)
    slot = i % 2

    def row_copy(src_row, s, r):
        return pltpu.make_async_copy(y_hbm.at[pl.ds(src_row, 1), :], ybuf.at[s, pl.ds(r, 1), :], sem.at[s])

    def issue(idx_ref, s):
        def body(r, c):
            row_copy(idx_ref[0, 0, r], s, r).start()
            return c
        lax.fori_loop(0, 2 * rows, body, 0, unroll=8)

    @pl.when(i == 0)
    def _():
        issue(pos_ref, 0)

    @pl.when(i + 1 < n)
    def _():
        issue(posn_ref, 1 - slot)

    def wait_body(r, c):
        row_copy(0, slot, r).wait()
        return c
    lax.fori_loop(0, 2 * rows, wait_body, 0, unroll=8)

    route = r_ref[...]
    w1 = route[:, 2:3]
    w2 = route[:, 3:4]
    moe = w1 * ybuf[slot, 0:rows, :] + w2 * ybuf[slot, rows:2 * rows, :]
    x = x_ref[...] + g2_ref[0] * moe
    o_ref[...] = x * lax.rsqrt(jnp.mean(x * x, axis=-1, keepdims=True) + EPS) * fg_ref[...]


def _final(x1, route, mod3, gate_blk, final_g, y, pos3):
    rows = FINAL_ROWS
    m = x1.shape[0]
    nt = m // rows
    return pl.pallas_call(
        _final_kernel,
        grid=(nt,),
        in_specs=[
            pl.BlockSpec((1, 1, 2 * rows), lambda i: (i, 0, 0), memory_space=pltpu.SMEM),
            pl.BlockSpec((1, 1, 2 * rows), lambda i: (jnp.minimum(i + 1, nt - 1), 0, 0), memory_space=pltpu.SMEM),
            pl.BlockSpec((rows, D_MODEL), lambda i: (i, 0)),
            pl.BlockSpec((rows, LANES), lambda i: (i, 0)),
            pl.BlockSpec((1, 1, D_MODEL), lambda i: (i * rows // SEQ, 0, gate_blk)),
            pl.BlockSpec((1, D_MODEL), lambda i: (0, 0)),
            pl.BlockSpec(memory_space=pl.ANY),
        ],
        out_specs=pl.BlockSpec((rows, D_MODEL), lambda i: (i, 0)),
        out_shape=jax.ShapeDtypeStruct((m, D_MODEL), F32),
        scratch_shapes=[pltpu.VMEM((2, 2 * rows, D_MODEL), F32), pltpu.SemaphoreType.DMA((2,))],
        compiler_params=_params("arbitrary"),
        name="final",
    )(pos3, pos3, x1, route, mod3, final_g, y)


def _rope_tables():
    rows = SEQ // GRID_W
    row = jnp.repeat(jnp.arange(rows, dtype=F32), GRID_W)
    col = jnp.tile(jnp.arange(GRID_W, dtype=F32), rows)
    n_freq = RET_DK // 4
    inv = ROPE_BASE ** (-jnp.arange(n_freq, dtype=F32) / n_freq)
    ang = jnp.concatenate([row[:, None] * inv, col[:, None] * inv], axis=-1)
    cos, sin = jnp.cos(ang), jnp.sin(ang)
    return jnp.concatenate([cos, cos], axis=-1), jnp.concatenate([-sin, sin], axis=-1)


def kernel(x, c, ctx, c_ctx, w_mod, b_mod, norm1_g, norm2_g, w_in, w_four_out, w_ret_out, w_out,
           ret_decay_f, ret_decay_b, w_group_router, b_group_router, w_expert_router, b_expert_router,
           w_gate, w_up, w_down, final_norm_g):
    b, n, d = x.shape
    assert (n, d) == (SEQ, D_MODEL) and ctx.shape[1] == CTX_LEN and w_mod.shape[0] == 1
    t = b * n

    c8 = jnp.zeros((8, d), F32).at[:b].set(c).at[b].set(c_ctx)
    mod3 = _modulation(c8, w_mod[0], b_mod).reshape(8, 1, N_MOD * d)

    hx = _prenorm(x, norm1_g, mod3, lambda i: i, 0, 1, rows=1024)
    hc = _prenorm(ctx, norm1_g, mod3, lambda i: b, 0, 1, rows=CTX_LEN)

    cos2, sin2 = _rope_tables()
    pc = _in_proj(hx.reshape(t, d), w_in[0], cos2, sin2)
    kvc = _project(hc.reshape(b * CTX_LEN, d), w_in[0], K_OFF // COL_TILE, (GF_OFF - K_OFF) // COL_TILE,
                   b * CTX_LEN, "ctx_proj")
    kvc3 = kvc.reshape(b, CTX_LEN, GF_OFF - K_OFF)

    fm = _fourier(pc, b)

    lg_f = jax.nn.log_sigmoid(ret_decay_f[0].astype(F32))
    lg_b = jax.nn.log_sigmoid(ret_decay_b[0].astype(F32))
    lg = jnp.stack([lg_f, lg_b, jnp.exp(RET_BLOCK * lg_f), jnp.exp(RET_BLOCK * lg_b)])
    ret = _retention(pc, kvc3, lg)

    merged = _merge(fm.reshape(t, FOUR_WIDTH), ret.reshape(t, d), w_four_out[0], w_ret_out[0], pc)
    x1 = _out_proj(merged, w_out[0], x.reshape(t, d), mod3, 2 * d // COL_TILE)

    w_router = jnp.concatenate(
        [w_group_router[0], w_expert_router[0].transpose(1, 0, 2).reshape(d, N_EXPERTS)], axis=1).astype(F32)
    w_router = jnp.pad(w_router, ((0, 0), (0, LANES - w_router.shape[1])))
    w_hi = w_router.astype(BF16)
    w_lo = (w_router - w_hi.astype(F32)).astype(BF16)
    bias = jnp.pad(jnp.concatenate([b_group_router[0], b_expert_router[0].reshape(-1)]).astype(F32),
                   (0, LANES - N_GROUPS - N_EXPERTS)).reshape(1, LANES)
    route = _route(x1, norm2_g, mod3, 3, 4, w_hi, w_lo, bias)

    pos, plan = _dispatch_plan(route)
    assert DISPATCH_ROWS == FINAL_ROWS
    pos3 = pos.reshape(t // FINAL_ROWS, FINAL_ROWS, 2).transpose(0, 2, 1).reshape(t // FINAL_ROWS, 1, 2 * FINAL_ROWS)
    xs = _dispatch(x1, norm2_g, mod3, 3, 4, pos3, 2 * t)
    y = _experts(xs, plan, w_gate[0], w_up[0], w_down[0])
    out = _final(x1, route, mod3, 5, final_norm_g.reshape(1, d), y, pos3)
    return out.reshape(b, n, d)
```

```python
import functools

import jax
import jax.numpy as jnp
import numpy as np
from jax import lax
from jax.experimental import pallas as pl
from jax.experimental.pallas import tpu as pltpu

F32 = jnp.float32
BF16 = jnp.bfloat16

D_MODEL = 2048
SEQ = 2048
CTX_LEN = 256
GRID_W = 64
N_MOD = 6
FOUR_GROUPS = 8
FOUR_GROUP_DIM = 128
FOUR_WIDTH = FOUR_GROUPS * FOUR_GROUP_DIM
RET_HEADS = 8
RET_DK = 128
RET_DV = 256
ROPE_BASE = 10000.0
N_GROUPS = 4
EXPERTS_PER_GROUP = 8
N_EXPERTS = N_GROUPS * EXPERTS_PER_GROUP
EXPERT_FF = 512
EPS = 1e-6
Q_OFF = FOUR_WIDTH
K_OFF = Q_OFF + RET_HEADS * RET_DK
V_OFF = K_OFF + RET_HEADS * RET_DK
GF_OFF = V_OFF + RET_HEADS * RET_DV
GB_OFF = GF_OFF + RET_HEADS * RET_DV
MF_OFF = GB_OFF + RET_HEADS * RET_DV
MR_OFF = MF_OFF + D_MODEL
IN_WIDTH = MR_OFF + D_MODEL

V7X_VMEM_LIMIT_BYTES = 56 * 1024 * 1024
LANES = 128
RET_BLOCK = 256
COL_TILE = 1024
MERGE_COL_TILE = 1024
IN_COL_TILE = 1024
IN_ROWS = 1024
EXPERT_ROWS = 256
DISPATCH_ROWS = 256
FINAL_ROWS = 256
N_WORK_ITEMS = 2 * 4 * SEQ // EXPERT_ROWS + N_EXPERTS


def _params(*sem):
    return pltpu.CompilerParams(dimension_semantics=sem, vmem_limit_bytes=V7X_VMEM_LIMIT_BYTES)


def _dot(a, b):
    return jnp.dot(a, b, preferred_element_type=F32)


def _mod_kernel(c_ref, w_ref, b_ref, o_ref):
    c = c_ref[...]
    sc = c * jax.nn.sigmoid(c)
    o_ref[...] = _dot(sc.astype(BF16), w_ref[...].astype(BF16)) + b_ref[...]


def _modulation(c8, w_mod, b_mod):
    n_out = w_mod.shape[1]
    return pl.pallas_call(
        _mod_kernel,
        grid=(n_out // COL_TILE,),
        in_specs=[
            pl.BlockSpec((8, D_MODEL), lambda j: (0, 0)),
            pl.BlockSpec((D_MODEL, COL_TILE), lambda j: (0, j)),
            pl.BlockSpec((1, COL_TILE), lambda j: (0, j)),
        ],
        out_specs=pl.BlockSpec((8, COL_TILE), lambda j: (0, j)),
        out_shape=jax.ShapeDtypeStruct((8, n_out), F32),
        compiler_params=_params("arbitrary"),
        name="mod",
    )(c8, w_mod, b_mod)


def _rms_modulate(x, g, shift, scale):
    y = x * lax.rsqrt(jnp.mean(x * x, axis=-1, keepdims=True) + EPS) * g
    return y * (1.0 + scale) + shift


def _prenorm_kernel(x_ref, g_ref, sh_ref, sc_ref, o_ref):
    o_ref[0] = _rms_modulate(x_ref[0], g_ref[...], sh_ref[0], sc_ref[0]).astype(o_ref.dtype)


def _prenorm(x, g, mod3, row_of_batch, shift_blk, scale_blk, rows):
    b, n, d = x.shape
    return pl.pallas_call(
        _prenorm_kernel,
        grid=(b, n // rows),
        in_specs=[
            pl.BlockSpec((1, rows, d), lambda i, j: (i, j, 0)),
            pl.BlockSpec((1, d), lambda i, j: (0, 0)),
            pl.BlockSpec((1, 1, d), lambda i, j: (row_of_batch(i), 0, shift_blk)),
            pl.BlockSpec((1, 1, d), lambda i, j: (row_of_batch(i), 0, scale_blk)),
        ],
        out_specs=pl.BlockSpec((1, rows, d), lambda i, j: (i, j, 0)),
        out_shape=jax.ShapeDtypeStruct((b, n, d), BF16),
        compiler_params=_params("arbitrary", "arbitrary"),
        name="prenorm",
    )(x, g, mod3, mod3)


def _proj_kernel(a_ref, w_ref, o_ref, wb_ref):
    @pl.when(pl.program_id(1) == 0)
    def _():
        wb_ref[...] = w_ref[...].astype(BF16)

    o_ref[...] = _dot(a_ref[...], wb_ref[...]).astype(o_ref.dtype)


def _project(a, w, col_tile0, n_col_tiles, rows, name):
    m, k = a.shape
    return pl.pallas_call(
        _proj_kernel,
        grid=(n_col_tiles, m // rows),
        in_specs=[
            pl.BlockSpec((rows, k), lambda j, i: (i, 0)),
            pl.BlockSpec((k, COL_TILE), lambda j, i: (0, j + col_tile0)),
        ],
        out_specs=pl.BlockSpec((rows, COL_TILE), lambda j, i: (i, j)),
        out_shape=jax.ShapeDtypeStruct((m, n_col_tiles * COL_TILE), BF16),
        scratch_shapes=[pltpu.VMEM((k, COL_TILE), BF16)],
        compiler_params=_params("arbitrary", "arbitrary"),
        name=name,
    )(a, w)


def _rope(t, cos2, sin2):
    return t * cos2 + pltpu.roll(t, RET_DK // 2, axis=1) * sin2


def _in_proj_kernel(a_ref, w_ref, cos_ref, sin_ref, o_ref, wb_ref, acc_ref):
    j = pl.program_id(0)

    @pl.when(pl.program_id(1) == 0)
    def _():
        wb_ref[...] = w_ref[...].astype(BF16)

    q_tile, k_tile, v_tile = Q_OFF // IN_COL_TILE, K_OFF // IN_COL_TILE, V_OFF // IN_COL_TILE
    gate_tile, merge_tile = GF_OFF // IN_COL_TILE, MF_OFF // IN_COL_TILE
    is_qk = jnp.logical_and(j >= q_tile, j < v_tile)
    is_gate = jnp.logical_and(j >= gate_tile, j < merge_tile)

    def store_chunks(val):
        for c in range(IN_COL_TILE // LANES):
            o_ref[c] = val[:, c * LANES:(c + 1) * LANES].astype(o_ref.dtype)

    is_four = j < q_tile

    @pl.when(jnp.logical_not(jnp.logical_or(jnp.logical_or(is_qk, is_gate), is_four)))
    def _():
        store_chunks(_dot(a_ref[...], wb_ref[...]))

    @pl.when(is_four)
    def _():
        acc = _dot(a_ref[...], wb_ref[...])
        half = acc_ref.shape[1] // 2
        for c in range(IN_COL_TILE // LANES):
            acc_ref[c] = acc[:, c * LANES:(c + 1) * LANES]
            for parity in (0, 1):
                val = acc_ref[c, pl.ds(parity, half, stride=2), :]
                o_ref[c, parity * half:(parity + 1) * half, :] = val.astype(o_ref.dtype)

    @pl.when(is_qk)
    def _():
        acc = _dot(a_ref[...], wb_ref[...])
        scale = jnp.where(j < k_tile, RET_DK ** -0.5, 1.0).astype(F32)
        cos2, sin2 = cos_ref[...], sin_ref[...]
        assert RET_DK == LANES
        for h in range(IN_COL_TILE // RET_DK):
            cols = slice(h * RET_DK, (h + 1) * RET_DK)
            o_ref[h] = (_rope(acc[:, cols], cos2, sin2) * scale).astype(o_ref.dtype)

    @pl.when(is_gate)
    def _():
        store_chunks(_silu(_dot(a_ref[...], wb_ref[...])))


def _in_proj(a, w, cos2, sin2, rows=IN_ROWS):
    m, k = a.shape
    n_col_tiles = IN_WIDTH // IN_COL_TILE
    pos_blocks = SEQ // rows
    return pl.pallas_call(
        _in_proj_kernel,
        grid=(n_col_tiles, m // rows),
        in_specs=[
            pl.BlockSpec((rows, k), lambda j, i: (i, 0)),
            pl.BlockSpec((k, IN_COL_TILE), lambda j, i: (0, j)),
            pl.BlockSpec((rows, RET_DK), lambda j, i: (i % pos_blocks, 0)),
            pl.BlockSpec((rows, RET_DK), lambda j, i: (i % pos_blocks, 0)),
        ],
        out_specs=pl.BlockSpec((IN_COL_TILE // LANES, rows, LANES), lambda j, i: (j, i, 0)),
        out_shape=jax.ShapeDtypeStruct((IN_WIDTH // LANES, m, LANES), BF16),
        scratch_shapes=[pltpu.VMEM((k, IN_COL_TILE), BF16),
                        pltpu.VMEM((IN_COL_TILE // LANES, rows, LANES), F32)],
        compiler_params=_params("arbitrary", "arbitrary"),
        name="in_proj",
    )(a, w, cos2, sin2)


def _dft_tables():
    half = SEQ // 2
    k = np.arange(half, dtype=np.int64)[:, None]
    n2 = np.arange(half, dtype=np.int64)[None, :]
    tables = []
    for parity in (0, 1):
        ang = 2.0 * np.pi * ((k * (2 * n2 + parity)) % SEQ) / SEQ
        tables.append(np.concatenate([np.cos(ang), -np.sin(ang)], axis=1).astype(np.float32))
    c = np.arange(FOUR_GROUP_DIM, dtype=np.int64)
    ang_c = 2.0 * np.pi * ((c[:, None] * c[None, :]) % FOUR_GROUP_DIM) / FOUR_GROUP_DIM
    norm = 1.0 / np.sqrt(float(SEQ * FOUR_GROUP_DIM))
    chan = (np.concatenate([np.cos(ang_c), np.sin(ang_c)], axis=1) * norm).astype(np.float32)
    return tables[0], tables[1], chan


def _fourier_kernel(u_ref, chan_ref, even_ref, odd_ref, o_ref, y_ref):
    half = SEQ // 2
    tile_half = IN_ROWS // 2
    chan = chan_ref[...]
    for g in range(FOUR_GROUPS):
        cols = slice(g * FOUR_GROUP_DIM, (g + 1) * FOUR_GROUP_DIM)
        y = _dot(u_ref[g], chan).astype(BF16)
        for t in range(SEQ // IN_ROWS):
            for parity in (0, 1):
                src = slice(t * IN_ROWS + parity * tile_half, t * IN_ROWS + (parity + 1) * tile_half)
                dst = 2 * parity * half + t * tile_half
                y_ref[dst:dst + tile_half, cols] = y[src, :FOUR_GROUP_DIM]
                y_ref[half + dst:half + dst + tile_half, cols] = y[src, FOUR_GROUP_DIM:]
    e = _dot(even_ref[...], y_ref[0:SEQ, :])
    o = _dot(odd_ref[...], y_ref[SEQ:2 * SEQ, :])
    o_ref[0, 0:half, :] = (e + o).astype(o_ref.dtype)
    o_ref[0, half:SEQ, :] = (e - o).astype(o_ref.dtype)


def _fourier(pc, b):
    assert FOUR_GROUP_DIM == LANES and FOUR_WIDTH == IN_COL_TILE and SEQ % IN_ROWS == 0
    even_np, odd_np, chan_np = _dft_tables()
    even = jnp.asarray(even_np).astype(BF16)
    odd = jnp.asarray(odd_np).astype(BF16)
    chan = jnp.asarray(chan_np).astype(BF16)
    return pl.pallas_call(
        _fourier_kernel,
        grid=(b,),
        in_specs=[
            pl.BlockSpec((FOUR_GROUPS, SEQ, LANES), lambda i: (0, i, 0)),
            pl.BlockSpec((FOUR_GROUP_DIM, 2 * FOUR_GROUP_DIM), lambda i: (0, 0)),
            pl.BlockSpec((SEQ // 2, SEQ), lambda i: (0, 0)),
            pl.BlockSpec((SEQ // 2, SEQ), lambda i: (0, 0)),
        ],
        out_specs=pl.BlockSpec((1, SEQ, FOUR_WIDTH), lambda i: (i, 0, 0)),
        out_shape=jax.ShapeDtypeStruct((b, SEQ, FOUR_WIDTH), BF16),
        scratch_shapes=[pltpu.VMEM((2 * SEQ, FOUR_WIDTH), BF16)],
        compiler_params=_params("arbitrary"),
        name="fourier",
    )(pc, chan, even, odd)


def _head_norm(o):
    mu = jnp.mean(o, axis=-1, keepdims=True)
    d = o - mu
    var = jnp.mean(d * d, axis=-1, keepdims=True)
    return d * lax.rsqrt(var + EPS)


def _silu(x):
    return x * jax.nn.sigmoid(x)


def _retention_kernel(lg_ref, q_ref, k_ref, v_ref, gf_ref, gb_ref, kc_ref, vc_ref,
                      o_ref, acc_ref, df_ref, db_ref, xif_ref, xib_ref, zf_ref, zb_ref):
    blk = RET_BLOCK
    n_blk = SEQ // blk
    h = pl.program_id(0)
    lgf, lgb, gchf, gchb = lg_ref[0, h], lg_ref[1, h], lg_ref[2, h], lg_ref[3, h]

    @pl.when(pl.program_id(1) == 0)
    def _():
        row = lax.broadcasted_iota(jnp.int32, (blk, blk), 0)
        col = lax.broadcasted_iota(jnp.int32, (blk, blk), 1)
        diff = (row - col).astype(F32)
        df_ref[...] = jnp.where(diff >= 0, jnp.exp(jnp.maximum(diff, 0.0) * lgf), 0.0)
        db_ref[...] = jnp.where(diff <= 0, jnp.exp(jnp.maximum(-diff, 0.0) * lgb), 0.0)
        pos_v = lax.broadcasted_iota(jnp.int32, (blk, RET_DV), 0).astype(F32)
        xif_ref[...] = jnp.exp((pos_v + 1.0) * lgf)
        xib_ref[...] = jnp.exp((blk - pos_v) * lgb)
        pos_k = lax.broadcasted_iota(jnp.int32, (blk, RET_DK), 0).astype(F32)
        zf_ref[...] = jnp.exp((blk - 1.0 - pos_k) * lgf)
        zb_ref[...] = jnp.exp(pos_k * lgb)

    def state_update(r, k, z_ref, v, gch):
        kz = (k.astype(F32) * z_ref[...]).astype(BF16)
        return gch * r + lax.dot_general(kz, v, (((0,), (0,)), ((), ())), preferred_element_type=F32)

    def block_out(qb, kb, vb, r, d_ref, xi_ref):
        s = lax.dot_general(qb, kb, (((1,), (1,)), ((), ())), preferred_element_type=F32)
        inner = _dot((s * d_ref[...]).astype(BF16), vb)
        cross = _dot(qb, r.astype(BF16)) * xi_ref[...]
        return inner + cross

    def wide(ref, rows):
        return jnp.concatenate([ref[0, rows, :], ref[1, rows, :]], axis=1)

    zero = jnp.zeros((RET_DK, RET_DV), F32)
    r_f = state_update(zero, kc_ref[0], zf_ref, vc_ref[0], gchf)
    r_b = state_update(zero, kc_ref[0], zb_ref, vc_ref[0], gchb)

    def emit(i, part, reached_first):
        rows = slice(i * blk, (i + 1) * blk)
        if reached_first:
            acc_ref[rows, :] = part
        else:
            o_ref[0, rows, :] = (acc_ref[rows, :] + part).astype(o_ref.dtype)

    for j in range(n_blk):
        i_f, i_b = j, n_blk - 1 - j
        rows_f = slice(i_f * blk, (i_f + 1) * blk)
        rows_b = slice(i_b * blk, (i_b + 1) * blk)
        qf, kf, vf = q_ref[0, rows_f, :], k_ref[0, rows_f, :], wide(v_ref, rows_f)
        qk, kk, vk = q_ref[0, rows_b, :], k_ref[0, rows_b, :], wide(v_ref, rows_b)
        o_f = block_out(qf, kf, vf, r_f, df_ref, xif_ref)
        o_b = block_out(qk, kk, vk, r_b, db_ref, xib_ref)
        r_f = state_update(r_f, kf, zf_ref, vf, gchf)
        r_b = state_update(r_b, kk, zb_ref, vk, gchb)
        first_visit = j < n_blk - 1 - j
        emit(i_f, wide(gf_ref, rows_f).astype(F32) * _head_norm(o_f), first_visit)
        emit(i_b, wide(gb_ref, rows_b).astype(F32) * _head_norm(o_b), first_visit)


def _retention(pc, kvc3, lg):
    b = kvc3.shape[0]
    assert CTX_LEN == RET_BLOCK and (SEQ // RET_BLOCK) % 2 == 0 and RET_DK == LANES and RET_DV == 2 * LANES
    qb0, kb0 = Q_OFF // RET_DK, K_OFF // RET_DK
    vb0, gfb0, gbb0 = V_OFF // RET_DV, GF_OFF // RET_DV, GB_OFF // RET_DV
    kcv0 = (RET_HEADS * RET_DK) // RET_DV
    return pl.pallas_call(
        _retention_kernel,
        grid=(RET_HEADS, b),
        in_specs=[
            pl.BlockSpec(memory_space=pltpu.SMEM),
            pl.BlockSpec((1, SEQ, LANES), lambda h, i: (qb0 + h, i, 0)),
            pl.BlockSpec((1, SEQ, LANES), lambda h, i: (kb0 + h, i, 0)),
            pl.BlockSpec((RET_DV // LANES, SEQ, LANES), lambda h, i: (vb0 + h, i, 0)),
            pl.BlockSpec((RET_DV // LANES, SEQ, LANES), lambda h, i: (gfb0 + h, i, 0)),
            pl.BlockSpec((RET_DV // LANES, SEQ, LANES), lambda h, i: (gbb0 + h, i, 0)),
            pl.BlockSpec((1, CTX_LEN, RET_DK), lambda h, i: (i, 0, h)),
            pl.BlockSpec((1, CTX_LEN, RET_DV), lambda h, i: (i, 0, kcv0 + h)),
        ],
        out_specs=pl.BlockSpec((1, SEQ, RET_DV), lambda h, i: (i, 0, h)),
        out_shape=jax.ShapeDtypeStruct((b, SEQ, RET_HEADS * RET_DV), BF16),
        scratch_shapes=[
            pltpu.VMEM((SEQ, RET_DV), F32),
            pltpu.VMEM((RET_BLOCK, RET_BLOCK), F32),
            pltpu.VMEM((RET_BLOCK, RET_BLOCK), F32),
            pltpu.VMEM((RET_BLOCK, RET_DV), F32),
            pltpu.VMEM((RET_BLOCK, RET_DV), F32),
            pltpu.VMEM((RET_BLOCK, RET_DK), F32),
            pltpu.VMEM((RET_BLOCK, RET_DK), F32),
        ],
        compiler_params=_params("arbitrary", "arbitrary"),
        name="retention",
    )(lg, pc, pc, pc, pc, pc, kvc3, kvc3)


def _merge_kernel(fm_ref, ret_ref, wf_ref, wr_ref, mf_ref, mr_ref, o_ref, wfb_ref, wrb_ref):
    @pl.when(pl.program_id(1) == 0)
    def _():
        wfb_ref[...] = wf_ref[...].astype(BF16)
        wrb_ref[...] = wr_ref[...].astype(BF16)

    four = _dot(fm_ref[...], wfb_ref[...])
    ret = _dot(ret_ref[...], wrb_ref[...])
    for c in range(MERGE_COL_TILE // LANES):
        cols = slice(c * LANES, (c + 1) * LANES)
        merged = (jax.nn.sigmoid(mf_ref[c].astype(F32)) * four[:, cols]
                  + jax.nn.sigmoid(mr_ref[c].astype(F32)) * ret[:, cols])
        o_ref[:, cols] = merged.astype(o_ref.dtype)


def _merge(fm2, ret2, w_four_out, w_ret_out, pc, rows=512):
    m = fm2.shape[0]
    tile = MERGE_COL_TILE
    mf0, mr0 = MF_OFF // tile, MR_OFF // tile
    chunks = tile // LANES
    return pl.pallas_call(
        _merge_kernel,
        grid=(D_MODEL // tile, m // rows),
        in_specs=[
            pl.BlockSpec((rows, FOUR_WIDTH), lambda j, i: (i, 0)),
            pl.BlockSpec((rows, D_MODEL), lambda j, i: (i, 0)),
            pl.BlockSpec((FOUR_WIDTH, tile), lambda j, i: (0, j)),
            pl.BlockSpec((D_MODEL, tile), lambda j, i: (0, j)),
            pl.BlockSpec((chunks, rows, LANES), lambda j, i: (mf0 + j, i, 0)),
            pl.BlockSpec((chunks, rows, LANES), lambda j, i: (mr0 + j, i, 0)),
        ],
        out_specs=pl.BlockSpec((rows, tile), lambda j, i: (i, j)),
        out_shape=jax.ShapeDtypeStruct((m, D_MODEL), BF16),
        scratch_shapes=[pltpu.VMEM((FOUR_WIDTH, tile), BF16), pltpu.VMEM((D_MODEL, tile), BF16)],
        compiler_params=_params("arbitrary", "arbitrary"),
        name="merge",
    )(fm2, ret2, w_four_out, w_ret_out, pc, pc)


def _out_proj_kernel(a_ref, w_ref, x_ref, g_ref, o_ref, wb_ref):
    @pl.when(pl.program_id(1) == 0)
    def _():
        wb_ref[...] = w_ref[...].astype(BF16)

    o_ref[...] = x_ref[...] + g_ref[0] * _dot(a_ref[...], wb_ref[...])


def _out_proj(merged, w_out, x2, mod3, gate_blk0, rows=1024):
    m = merged.shape[0]
    return pl.pallas_call(
        _out_proj_kernel,
        grid=(D_MODEL // COL_TILE, m // rows),
        in_specs=[
            pl.BlockSpec((rows, D_MODEL), lambda j, i: (i, 0)),
            pl.BlockSpec((D_MODEL, COL_TILE), lambda j, i: (0, j)),
            pl.BlockSpec((rows, COL_TILE), lambda j, i: (i, j)),
            pl.BlockSpec((1, 1, COL_TILE), lambda j, i: (i * rows // SEQ, 0, gate_blk0 + j)),
        ],
        out_specs=pl.BlockSpec((rows, COL_TILE), lambda j, i: (i, j)),
        out_shape=jax.ShapeDtypeStruct((m, D_MODEL), F32),
        scratch_shapes=[pltpu.VMEM((D_MODEL, COL_TILE), BF16)],
        compiler_params=_params("arbitrary", "arbitrary"),
        name="out_proj",
    )(merged, w_out, x2, mod3)


def _route_kernel(x_ref, g_ref, sh_ref, sc_ref, whi_ref, wlo_ref, b_ref, r_ref):
    h = _rms_modulate(x_ref[...], g_ref[...], sh_ref[0], sc_ref[0])
    hi = h.astype(BF16)
    lo = (h - hi.astype(F32)).astype(BF16)
    logits = _dot(hi, whi_ref[...]) + _dot(lo, whi_ref[...]) + _dot(hi, wlo_ref[...]) + b_ref[...]

    lane = lax.broadcasted_iota(jnp.int32, logits.shape, 1).astype(F32)
    neg = -jnp.inf
    first = lambda hit: jnp.min(jnp.where(hit, lane, float(LANES)), axis=1, keepdims=True)
    is_grp = lane < float(N_GROUPS)
    gl = jnp.where(is_grp, logits, neg)
    gmax = jnp.max(gl, axis=1, keepdims=True)
    grp = first(gl == gmax)
    g_w = 1.0 / jnp.sum(jnp.where(is_grp, jnp.exp(logits - gmax), 0.0), axis=1, keepdims=True)
    lo_lane = float(N_GROUPS) + grp * float(EXPERTS_PER_GROUP)
    el = jnp.where(lane >= lo_lane, jnp.where(lane < lo_lane + float(EXPERTS_PER_GROUP), logits, neg), neg)
    v1 = jnp.max(el, axis=1, keepdims=True)
    i1 = first(el == v1)
    el2 = jnp.where(lane == i1, neg, el)
    v2 = jnp.max(el2, axis=1, keepdims=True)
    i2 = first(el2 == v2)
    e = jnp.exp(v2 - v1)
    w1 = g_w / (1.0 + e)
    w2 = g_w * e / (1.0 + e)
    out = jnp.where(lane == 0.0, i1 - float(N_GROUPS),
                    jnp.where(lane == 1.0, i2 - float(N_GROUPS),
                              jnp.where(lane == 2.0, w1, jnp.where(lane == 3.0, w2, 0.0))))
    r_ref[...] = out


def _route(x1, g, mod3, shift_blk, scale_blk, w_hi, w_lo, bias, rows=512):
    m = x1.shape[0]
    return pl.pallas_call(
        _route_kernel,
        grid=(m // rows,),
        in_specs=[
            pl.BlockSpec((rows, D_MODEL), lambda i: (i, 0)),
            pl.BlockSpec((1, D_MODEL), lambda i: (0, 0)),
            pl.BlockSpec((1, 1, D_MODEL), lambda i: (i * rows // SEQ, 0, shift_blk)),
            pl.BlockSpec((1, 1, D_MODEL), lambda i: (i * rows // SEQ, 0, scale_blk)),
            pl.BlockSpec((D_MODEL, LANES), lambda i: (0, 0)),
            pl.BlockSpec((D_MODEL, LANES), lambda i: (0, 0)),
            pl.BlockSpec((1, LANES), lambda i: (0, 0)),
        ],
        out_specs=pl.BlockSpec((rows, LANES), lambda i: (i, 0)),
        out_shape=jax.ShapeDtypeStruct((m, LANES), F32),
        compiler_params=_params("arbitrary"),
        name="route",
    )(x1, g, mod3, mod3, w_hi, w_lo, bias)


def _dispatch_plan(route):
    rows = EXPERT_ROWS
    flat = route[:, :2].astype(jnp.int32).reshape(-1)
    n_tiles = flat.shape[0] // rows
    experts = jnp.arange(N_EXPERTS, dtype=jnp.int32)
    onehot = (flat[:, None] == experts[None, :]).astype(jnp.int32)
    csum = jnp.cumsum(onehot, axis=0)
    rank = jnp.sum(csum * onehot, axis=1) - 1
    counts = csum[-1]
    cend = jnp.cumsum(counts)
    cstart = cend - counts
    pos = jnp.sum(cstart[None, :] * onehot, axis=1) + rank
    first_tile = cstart // rows
    items = jnp.where(counts > 0, (cend - 1) // rows - first_tile + 1, 0)
    wend = jnp.cumsum(items)
    wstart = wend - items
    n_work = wend[-1]
    w = jnp.arange(N_WORK_ITEMS, dtype=jnp.int32)
    valid = w < n_work
    last_expert = jnp.max(jnp.where(counts > 0, experts, 0))
    e_w = jnp.minimum(jnp.sum((wend[None, :] <= w[:, None]).astype(jnp.int32), axis=1), N_EXPERTS - 1)
    e_w = jnp.where(valid, e_w, last_expert)
    sel = (e_w[:, None] == experts[None, :]).astype(jnp.int32)
    pick = lambda v: jnp.sum(sel * v[None, :], axis=1)
    tile_w = jnp.where(valid, pick(first_tile) + w - pick(wstart), n_tiles - 1)
    lo_w = jnp.where(valid, jnp.clip(pick(cstart) - tile_w * rows, 0, rows), 0)
    hi_w = jnp.where(valid, jnp.clip(pick(cend) - tile_w * rows, 0, rows), 0)
    prev = lambda v: jnp.concatenate([jnp.full((1,), -1, jnp.int32), v[:-1]])
    new_expert = (e_w != prev(e_w)).astype(jnp.int32)
    new_tile = (tile_w != prev(tile_w)).astype(jnp.int32)
    later = jnp.logical_and(counts[None, :] > 0, experts[None, :] > experts[:, None])
    next_of = jnp.min(jnp.where(later, experts[None, :], N_EXPERTS), axis=1)
    next_of = jnp.where(next_of < N_EXPERTS, next_of, -1)
    buf_w = (jnp.cumsum(new_expert) - 1) % 2

    def follow(e):
        hit = (e[:, None] == experts[None, :]).astype(jnp.int32)
        return jnp.where(e >= 0, jnp.sum(hit * next_of[None, :], axis=1), -1)

    nxt1 = follow(e_w)
    nxt2 = follow(nxt1)
    nxt3 = follow(nxt2)
    plan = jnp.stack([tile_w, e_w, new_expert, new_tile, lo_w, hi_w,
                      jnp.broadcast_to(n_work, w.shape), nxt1, buf_w, nxt2, nxt3]).astype(jnp.int32)
    return pos, plan


def _dispatch_kernel(pos_ref, x_ref, g_ref, sh_ref, sc_ref, xs_hbm, hbuf, sem):
    rows = DISPATCH_ROWS
    i = pl.program_id(0)
    n = pl.num_programs(0)
    slot = i % 2

    def row_copy(s, r, dst_row):
        return pltpu.make_async_copy(hbuf.at[s, pl.ds(r, 1), :], xs_hbm.at[pl.ds(dst_row, 1), :], sem.at[s])

    def drain(s):
        def body(r, c):
            row_copy(s, 0, 0).wait()
            return c
        lax.fori_loop(0, 2 * rows, body, 0, unroll=8)

    @pl.when(i >= 2)
    def _():
        drain(slot)

    hbuf[slot] = _rms_modulate(x_ref[...], g_ref[...], sh_ref[0], sc_ref[0])

    def issue(r, c):
        row_copy(slot, r, pos_ref[0, 0, r]).start()
        row_copy(slot, r, pos_ref[0, 0, rows + r]).start()
        return c
    lax.fori_loop(0, rows, issue, 0, unroll=8)

    @pl.when(i == n - 1)
    def _():
        drain(slot)
        drain(1 - slot)


def _dispatch(x1, g, mod3, shift_blk, scale_blk, pos3, n_sorted):
    rows = DISPATCH_ROWS
    m = x1.shape[0]
    assert m // rows >= 2
    return pl.pallas_call(
        _dispatch_kernel,
        grid=(m // rows,),
        in_specs=[
            pl.BlockSpec((1, 1, 2 * rows), lambda i: (i, 0, 0), memory_space=pltpu.SMEM),
            pl.BlockSpec((rows, D_MODEL), lambda i: (i, 0)),
            pl.BlockSpec((1, D_MODEL), lambda i: (0, 0)),
            pl.BlockSpec((1, 1, D_MODEL), lambda i: (i * rows // SEQ, 0, shift_blk)),
            pl.BlockSpec((1, 1, D_MODEL), lambda i: (i * rows // SEQ, 0, scale_blk)),
        ],
        out_specs=pl.BlockSpec(memory_space=pl.ANY),
        out_shape=jax.ShapeDtypeStruct((n_sorted, D_MODEL), F32),
        scratch_shapes=[pltpu.VMEM((2, rows, D_MODEL), F32), pltpu.SemaphoreType.DMA((2,))],
        compiler_params=_params("arbitrary"),
        name="dispatch",
    )(pos3, x1, g, mod3, mod3)


def _experts_kernel(plan_ref, x_ref, wg_hbm, wu_hbm, wd_hbm, y_ref,
                    wg_buf, wu_buf, wd_buf, sem, wgb_ref, wub_ref, wdb_ref):
    w = pl.program_id(0)

    def weight_copies(e, s):
        return (pltpu.make_async_copy(wg_hbm.at[e], wg_buf.at[s], sem.at[s, 0]),
                pltpu.make_async_copy(wu_hbm.at[e], wu_buf.at[s], sem.at[s, 1]),
                pltpu.make_async_copy(wd_hbm.at[e], wd_buf.at[s], sem.at[s, 2]))

    def fetch(e, s):
        for cp in weight_copies(e, s):
            cp.start()

    def take(e, s):
        for cp in weight_copies(e, s):
            cp.wait()
        wgb_ref[s] = wg_buf[s].astype(BF16)
        wub_ref[s] = wu_buf[s].astype(BF16)
        wdb_ref[s] = wd_buf[s].astype(BF16)

    s = plan_ref[8, w]
    nxt1, nxt2, nxt3 = plan_ref[7, w], plan_ref[9, w], plan_ref[10, w]

    @pl.when(w == 0)
    def _():
        fetch(plan_ref[1, 0], 0)

        @pl.when(nxt1 >= 0)
        def _():
            fetch(nxt1, 1)

        take(plan_ref[1, 0], 0)

        @pl.when(nxt2 >= 0)
        def _():
            fetch(nxt2, 0)

    @pl.when(w < plan_ref[6, w])
    def _():
        @pl.when(jnp.logical_and(plan_ref[2, w] == 1, nxt1 >= 0))
        def _():
            take(nxt1, 1 - s)

            @pl.when(nxt3 >= 0)
            def _():
                fetch(nxt3, 1 - s)

        x = x_ref[...].astype(BF16)
        hg = _dot(x, wgb_ref[s])
        hu = _dot(x, wub_ref[s])
        hid = _silu(hg) * hu
        y = _dot(hid.astype(BF16), wdb_ref[s])
        row = lax.broadcasted_iota(jnp.int32, (EXPERT_ROWS, 1), 0)
        mine = jnp.logical_and(row >= plan_ref[4, w], row < plan_ref[5, w])

        @pl.when(plan_ref[3, w] == 1)
        def _():
            y_ref[...] = jnp.where(mine, y, 0.0)

        @pl.when(plan_ref[3, w] == 0)
        def _():
            y_ref[...] = jnp.where(mine, y, y_ref[...])


def _experts(xs, plan, w_gate, w_up, w_down):
    rows = EXPERT_ROWS
    grid_spec = pltpu.PrefetchScalarGridSpec(
        num_scalar_prefetch=1,
        grid=(N_WORK_ITEMS,),
        in_specs=[
            pl.BlockSpec((rows, D_MODEL), lambda w, p: (p[0, w], 0)),
            pl.BlockSpec(memory_space=pl.ANY),
            pl.BlockSpec(memory_space=pl.ANY),
            pl.BlockSpec(memory_space=pl.ANY),
        ],
        out_specs=pl.BlockSpec((rows, D_MODEL), lambda w, p: (p[0, w], 0)),
        scratch_shapes=[
            pltpu.VMEM((2, D_MODEL, EXPERT_FF), F32),
            pltpu.VMEM((2, D_MODEL, EXPERT_FF), F32),
            pltpu.VMEM((2, EXPERT_FF, D_MODEL), F32),
            pltpu.SemaphoreType.DMA((2, 3)),
            pltpu.VMEM((2, D_MODEL, EXPERT_FF), BF16),
            pltpu.VMEM((2, D_MODEL, EXPERT_FF), BF16),
            pltpu.VMEM((2, EXPERT_FF, D_MODEL), BF16),
        ],
    )
    return pl.pallas_call(
        _experts_kernel,
        grid_spec=grid_spec,
        out_shape=jax.ShapeDtypeStruct(xs.shape, F32),
        compiler_params=_params("arbitrary"),
        name="experts",
    )(plan, xs, w_gate, w_up, w_down)


def _final_kernel(pos_ref, posn_ref, x_ref, r_ref, g2_ref, fg_ref, y_hbm, o_ref, ybuf, sem):
    rows = FINAL_ROWS
    i = pl.program_id(0)
    n = pl.num_programs(0)
    slot = i % 2

    def row_copy(src_row, s, r):
        return pltpu.make_async_copy(y_hbm.at[pl.ds(src_row, 1), :], ybuf.at[s, pl.ds(r, 1), :], sem.at[s])

    def issue(idx_ref, s):
        def body(r, c):
            row_copy(idx_ref[0, 0, r], s, r).start()
            return c
        lax.fori_loop(0, 2 * rows, body, 0, unroll=8)

    @pl.when(i == 0)
    def _():
        issue(pos_ref, 0)

    @pl.when(i + 1 < n)
    def _():
        issue(posn_ref, 1 - slot)

    def wait_body(r, c):
        row_copy(0, slot, r).wait()
        return c
    lax.fori_loop(0, 2 * rows, wait_body, 0, unroll=8)

    route = r_ref[...]
    w1 = route[:, 2:3]
    w2 = route[:, 3:4]
    moe = w1 * ybuf[slot, 0:rows, :] + w2 * ybuf[slot, rows:2 * rows, :]
    x = x_ref[...] + g2_ref[0] * moe
    o_ref[...] = x * lax.rsqrt(jnp.mean(x * x, axis=-1, keepdims=True) + EPS) * fg_ref[...]


def _final(x1, route, mod3, gate_blk, final_g, y, pos3):
    rows = FINAL_ROWS
    m = x1.shape[0]
    nt = m // rows
    return pl.pallas_call(
        _final_kernel,
        grid=(nt,),
        in_specs=[
            pl.BlockSpec((1, 1, 2 * rows), lambda i: (i, 0, 0), memory_space=pltpu.SMEM),
            pl.BlockSpec((1, 1, 2 * rows), lambda i: (jnp.minimum(i + 1, nt - 1), 0, 0), memory_space=pltpu.SMEM),
            pl.BlockSpec((rows, D_MODEL), lambda i: (i, 0)),
            pl.BlockSpec((rows, LANES), lambda i: (i, 0)),
            pl.BlockSpec((1, 1, D_MODEL), lambda i: (i * rows // SEQ, 0, gate_blk)),
            pl.BlockSpec((1, D_MODEL), lambda i: (0, 0)),
            pl.BlockSpec(memory_space=pl.ANY),
        ],
        out_specs=pl.BlockSpec((rows, D_MODEL), lambda i: (i, 0)),
        out_shape=jax.ShapeDtypeStruct((m, D_MODEL), F32),
        scratch_shapes=[pltpu.VMEM((2, 2 * rows, D_MODEL), F32), pltpu.SemaphoreType.DMA((2,))],
        compiler_params=_params("arbitrary"),
        name="final",
    )(pos3, pos3, x1, route, mod3, final_g, y)


def _rope_tables():
    rows = SEQ // GRID_W
    row = jnp.repeat(jnp.arange(rows, dtype=F32), GRID_W)
    col = jnp.tile(jnp.arange(GRID_W, dtype=F32), rows)
    n_freq = RET_DK // 4
    inv = ROPE_BASE ** (-jnp.arange(n_freq, dtype=F32) / n_freq)
    ang = jnp.concatenate([row[:, None] * inv, col[:, None] * inv], axis=-1)
    cos, sin = jnp.cos(ang), jnp.sin(ang)
    return jnp.concatenate([cos, cos], axis=-1), jnp.concatenate([-sin, sin], axis=-1)


def kernel(x, c, ctx, c_ctx, w_mod, b_mod, norm1_g, norm2_g, w_in, w_four_out, w_ret_out, w_out,
           ret_decay_f, ret_decay_b, w_group_router, b_group_router, w_expert_router, b_expert_router,
           w_gate, w_up, w_down, final_norm_g):
    b, n, d = x.shape
    assert (n, d) == (SEQ, D_MODEL) and ctx.shape[1] == CTX_LEN and w_mod.shape[0] == 1
    t = b * n

    c8 = jnp.zeros((8, d), F32).at[:b].set(c).at[b].set(c_ctx)
    mod3 = _modulation(c8, w_mod[0], b_mod).reshape(8, 1, N_MOD * d)

    hx = _prenorm(x, norm1_g, mod3, lambda i: i, 0, 1, rows=1024)
    hc = _prenorm(ctx, norm1_g, mod3, lambda i: b, 0, 1, rows=CTX_LEN)

    cos2, sin2 = _rope_tables()
    pc = _in_proj(hx.reshape(t, d), w_in[0], cos2, sin2)
    kvc = _project(hc.reshape(b * CTX_LEN, d), w_in[0], K_OFF // COL_TILE, (GF_OFF - K_OFF) // COL_TILE,
                   b * CTX_LEN, "ctx_proj")
    kvc3 = kvc.reshape(b, CTX_LEN, GF_OFF - K_OFF)

    fm = _fourier(pc, b)

    lg_f = jax.nn.log_sigmoid(ret_decay_f[0].astype(F32))
    lg_b = jax.nn.log_sigmoid(ret_decay_b[0].astype(F32))
    lg = jnp.stack([lg_f, lg_b, jnp.exp(RET_BLOCK * lg_f), jnp.exp(RET_BLOCK * lg_b)])
    ret = _retention(pc, kvc3, lg)

    merged = _merge(fm.reshape(t, FOUR_WIDTH), ret.reshape(t, d), w_four_out[0], w_ret_out[0], pc)
    x1 = _out_proj(merged, w_out[0], x.reshape(t, d), mod3, 2 * d // COL_TILE)

    w_router = jnp.concatenate(
        [w_group_router[0], w_expert_router[0].transpose(1, 0, 2).reshape(d, N_EXPERTS)], axis=1).astype(F32)
    w_router = jnp.pad(w_router, ((0, 0), (0, LANES - w_router.shape[1])))
    w_hi = w_router.astype(BF16)
    w_lo = (w_router - w_hi.astype(F32)).astype(BF16)
    bias = jnp.pad(jnp.concatenate([b_group_router[0], b_expert_router[0].reshape(-1)]).astype(F32),
                   (0, LANES - N_GROUPS - N_EXPERTS)).reshape(1, LANES)
    route = _route(x1, norm2_g, mod3, 3, 4, w_hi, w_lo, bias)

    pos, plan = _dispatch_plan(route)
    assert DISPATCH_ROWS == FINAL_ROWS
    pos3 = pos.reshape(t // FINAL_ROWS, FINAL_ROWS, 2).transpose(0, 2, 1).reshape(t // FINAL_ROWS, 1, 2 * FINAL_ROWS)
    xs = _dispatch(x1, norm2_g, mod3, 3, 4, pos3, 2 * t)
    y = _experts(xs, plan, w_gate[0], w_up[0], w_down[0])
    out = _final(x1, route, mod3, 5, final_norm_g.reshape(1, d), y, pos3)
    return out.reshape(b, n, d)
```

```python
import functools

import jax
import jax.numpy as jnp
import numpy as np
from jax import lax
from jax.experimental import pallas as pl
from jax.experimental.pallas import tpu as pltpu

F32 = jnp.float32
BF16 = jnp.bfloat16

D_MODEL = 2048
SEQ = 2048
CTX_LEN = 256
GRID_W = 64
N_MOD = 6
FOUR_GROUPS = 8
FOUR_GROUP_DIM = 128
FOUR_WIDTH = FOUR_GROUPS * FOUR_GROUP_DIM
RET_HEADS = 8
RET_DK = 128
RET_DV = 256
ROPE_BASE = 10000.0
N_GROUPS = 4
EXPERTS_PER_GROUP = 8
N_EXPERTS = N_GROUPS * EXPERTS_PER_GROUP
EXPERT_FF = 512
EPS = 1e-6
Q_OFF = FOUR_WIDTH
K_OFF = Q_OFF + RET_HEADS * RET_DK
V_OFF = K_OFF + RET_HEADS * RET_DK
GF_OFF = V_OFF + RET_HEADS * RET_DV
GB_OFF = GF_OFF + RET_HEADS * RET_DV
MF_OFF = GB_OFF + RET_HEADS * RET_DV
MR_OFF = MF_OFF + D_MODEL
IN_WIDTH = MR_OFF + D_MODEL

V7X_VMEM_LIMIT_BYTES = 56 * 1024 * 1024
LANES = 128
RET_BLOCK = 256
COL_TILE = 1024
MERGE_COL_TILE = 1024
IN_COL_TILE = 1024
IN_ROWS = 1024
EXPERT_ROWS = 256
DISPATCH_ROWS = 256
FINAL_ROWS = 256
N_WORK_ITEMS = 2 * 4 * SEQ // EXPERT_ROWS + N_EXPERTS


def _params(*sem):
    return pltpu.CompilerParams(dimension_semantics=sem, vmem_limit_bytes=V7X_VMEM_LIMIT_BYTES)


def _dot(a, b):
    return jnp.dot(a, b, preferred_element_type=F32)


def _mod_kernel(c_ref, w_ref, b_ref, o_ref):
    c = c_ref[...]
    sc = c * jax.nn.sigmoid(c)
    o_ref[...] = _dot(sc.astype(BF16), w_ref[...].astype(BF16)) + b_ref[...]


def _modulation(c8, w_mod, b_mod):
    n_out = w_mod.shape[1]
    return pl.pallas_call(
        _mod_kernel,
        grid=(n_out // COL_TILE,),
        in_specs=[
            pl.BlockSpec((8, D_MODEL), lambda j: (0, 0)),
            pl.BlockSpec((D_MODEL, COL_TILE), lambda j: (0, j)),
            pl.BlockSpec((1, COL_TILE), lambda j: (0, j)),
        ],
        out_specs=pl.BlockSpec((8, COL_TILE), lambda j: (0, j)),
        out_shape=jax.ShapeDtypeStruct((8, n_out), F32),
        compiler_params=_params("arbitrary"),
        name="mod",
    )(c8, w_mod, b_mod)


def _rms_modulate(x, g, shift, scale):
    y = x * lax.rsqrt(jnp.mean(x * x, axis=-1, keepdims=True) + EPS) * g
    return y * (1.0 + scale) + shift


def _prenorm_kernel(x_ref, g_ref, sh_ref, sc_ref, o_ref):
    o_ref[0] = _rms_modulate(x_ref[0], g_ref[...], sh_ref[0], sc_ref[0]).astype(o_ref.dtype)


def _prenorm(x, g, mod3, row_of_batch, shift_blk, scale_blk, rows):
    b, n, d = x.shape
    return pl.pallas_call(
        _prenorm_kernel,
        grid=(b, n // rows),
        in_specs=[
            pl.BlockSpec((1, rows, d), lambda i, j: (i, j, 0)),
            pl.BlockSpec((1, d), lambda i, j: (0, 0)),
            pl.BlockSpec((1, 1, d), lambda i, j: (row_of_batch(i), 0, shift_blk)),
            pl.BlockSpec((1, 1, d), lambda i, j: (row_of_batch(i), 0, scale_blk)),
        ],
        out_specs=pl.BlockSpec((1, rows, d), lambda i, j: (i, j, 0)),
        out_shape=jax.ShapeDtypeStruct((b, n, d), BF16),
        compiler_params=_params("arbitrary", "arbitrary"),
        name="prenorm",
    )(x, g, mod3, mod3)


def _proj_kernel(a_ref, w_ref, o_ref, wb_ref):
    @pl.when(pl.program_id(1) == 0)
    def _():
        wb_ref[...] = w_ref[...].astype(BF16)

    o_ref[...] = _dot(a_ref[...], wb_ref[...]).astype(o_ref.dtype)


def _project(a, w, col_tile0, n_col_tiles, rows, name):
    m, k = a.shape
    return pl.pallas_call(
        _proj_kernel,
        grid=(n_col_tiles, m // rows),
        in_specs=[
            pl.BlockSpec((rows, k), lambda j, i: (i, 0)),
            pl.BlockSpec((k, COL_TILE), lambda j, i: (0, j + col_tile0)),
        ],
        out_specs=pl.BlockSpec((rows, COL_TILE), lambda j, i: (i, j)),
        out_shape=jax.ShapeDtypeStruct((m, n_col_tiles * COL_TILE), BF16),
        scratch_shapes=[pltpu.VMEM((k, COL_TILE), BF16)],
        compiler_params=_params("arbitrary", "arbitrary"),
        name=name,
    )(a, w)


def _rope(t, cos2, sin2):
    return t * cos2 + pltpu.roll(t, RET_DK // 2, axis=1) * sin2


def _in_proj_kernel(a_ref, w_ref, cos_ref, sin_ref, o_ref, wb_ref, acc_ref):
    j = pl.program_id(0)

    @pl.when(pl.program_id(1) == 0)
    def _():
        wb_ref[...] = w_ref[...].astype(BF16)

    q_tile, k_tile, v_tile = Q_OFF // IN_COL_TILE, K_OFF // IN_COL_TILE, V_OFF // IN_COL_TILE
    gate_tile, merge_tile = GF_OFF // IN_COL_TILE, MF_OFF // IN_COL_TILE
    is_qk = jnp.logical_and(j >= q_tile, j < v_tile)
    is_gate = jnp.logical_and(j >= gate_tile, j < merge_tile)

    def store_chunks(val):
        for c in range(IN_COL_TILE // LANES):
            o_ref[c] = val[:, c * LANES:(c + 1) * LANES].astype(o_ref.dtype)

    is_four = j < q_tile

    @pl.when(jnp.logical_not(jnp.logical_or(jnp.logical_or(is_qk, is_gate), is_four)))
    def _():
        store_chunks(_dot(a_ref[...], wb_ref[...]))

    @pl.when(is_four)
    def _():
        acc = _dot(a_ref[...], wb_ref[...])
        half = acc_ref.shape[1] // 2
        for c in range(IN_COL_TILE // LANES):
            acc_ref[c] = acc[:, c * LANES:(c + 1) * LANES]
            for parity in (0, 1):
                val = acc_ref[c, pl.ds(parity, half, stride=2), :]
                o_ref[c, parity * half:(parity + 1) * half, :] = val.astype(o_ref.dtype)

    @pl.when(is_qk)
    def _():
        acc = _dot(a_ref[...], wb_ref[...])
        scale = jnp.where(j < k_tile, RET_DK ** -0.5, 1.0).astype(F32)
        cos2, sin2 = cos_ref[...], sin_ref[...]
        assert RET_DK == LANES
        for h in range(IN_COL_TILE // RET_DK):
            cols = slice(h * RET_DK, (h + 1) * RET_DK)
            o_ref[h] = (_rope(acc[:, cols], cos2, sin2) * scale).astype(o_ref.dtype)

    @pl.when(is_gate)
    def _():
        store_chunks(_silu(_dot(a_ref[...], wb_ref[...])))


def _in_proj(a, w, cos2, sin2, rows=IN_ROWS):
    m, k = a.shape
    n_col_tiles = IN_WIDTH // IN_COL_TILE
    pos_blocks = SEQ // rows
    return pl.pallas_call(
        _in_proj_kernel,
        grid=(n_col_tiles, m // rows),
        in_specs=[
            pl.BlockSpec((rows, k), lambda j, i: (i, 0)),
            pl.BlockSpec((k, IN_COL_TILE), lambda j, i: (0, j)),
            pl.BlockSpec((rows, RET_DK), lambda j, i: (i % pos_blocks, 0)),
            pl.BlockSpec((rows, RET_DK), lambda j, i: (i % pos_blocks, 0)),
        ],
        out_specs=pl.BlockSpec((IN_COL_TILE // LANES, rows, LANES), lambda j, i: (j, i, 0)),
        out_shape=jax.ShapeDtypeStruct((IN_WIDTH // LANES, m, LANES), BF16),
        scratch_shapes=[pltpu.VMEM((k, IN_COL_TILE), BF16),
                        pltpu.VMEM((IN_COL_TILE // LANES, rows, LANES), F32)],
        compiler_params=_params("arbitrary", "arbitrary"),
        name="in_proj",
    )(a, w, cos2, sin2)


def _dft_tables():
    half = SEQ // 2
    k = np.arange(half, dtype=np.int64)[:, None]
    n2 = np.arange(half, dtype=np.int64)[None, :]
    tables = []
    for parity in (0, 1):
        ang = 2.0 * np.pi * ((k * (2 * n2 + parity)) % SEQ) / SEQ
        tables.append(np.concatenate([np.cos(ang), -np.sin(ang)], axis=1).astype(np.float32))
    c = np.arange(FOUR_GROUP_DIM, dtype=np.int64)
    ang_c = 2.0 * np.pi * ((c[:, None] * c[None, :]) % FOUR_GROUP_DIM) / FOUR_GROUP_DIM
    norm = 1.0 / np.sqrt(float(SEQ * FOUR_GROUP_DIM))
    chan = (np.concatenate([np.cos(ang_c), np.sin(ang_c)], axis=1) * norm).astype(np.float32)
    return tables[0], tables[1], chan


def _fourier_kernel(u_ref, chan_ref, even_ref, odd_ref, o_ref, y_ref):
    half = SEQ // 2
    tile_half = IN_ROWS // 2
    chan = chan_ref[...]
    for g in range(FOUR_GROUPS):
        cols = slice(g * FOUR_GROUP_DIM, (g + 1) * FOUR_GROUP_DIM)
        y = _dot(u_ref[g], chan).astype(BF16)
        for t in range(SEQ // IN_ROWS):
            for parity in (0, 1):
                src = slice(t * IN_ROWS + parity * tile_half, t * IN_ROWS + (parity + 1) * tile_half)
                dst = 2 * parity * half + t * tile_half
                y_ref[dst:dst + tile_half, cols] = y[src, :FOUR_GROUP_DIM]
                y_ref[half + dst:half + dst + tile_half, cols] = y[src, FOUR_GROUP_DIM:]
    e = _dot(even_ref[...], y_ref[0:SEQ, :])
    o = _dot(odd_ref[...], y_ref[SEQ:2 * SEQ, :])
    o_ref[0, 0:half, :] = (e + o).astype(o_ref.dtype)
    o_ref[0, half:SEQ, :] = (e - o).astype(o_ref.dtype)


def _fourier(pc, b):
    assert FOUR_GROUP_DIM == LANES and FOUR_WIDTH == IN_COL_TILE and SEQ % IN_ROWS == 0
    even_np, odd_np, chan_np = _dft_tables()
    even = jnp.asarray(even_np).astype(BF16)
    odd = jnp.asarray(odd_np).astype(BF16)
    chan = jnp.asarray(chan_np).astype(BF16)
    return pl.pallas_call(
        _fourier_kernel,
        grid=(b,),
        in_specs=[
            pl.BlockSpec((FOUR_GROUPS, SEQ, LANES), lambda i: (0, i, 0)),
            pl.BlockSpec((FOUR_GROUP_DIM, 2 * FOUR_GROUP_DIM), lambda i: (0, 0)),
            pl.BlockSpec((SEQ // 2, SEQ), lambda i: (0, 0)),
            pl.BlockSpec((SEQ // 2, SEQ), lambda i: (0, 0)),
        ],
        out_specs=pl.BlockSpec((1, SEQ, FOUR_WIDTH), lambda i: (i, 0, 0)),
        out_shape=jax.ShapeDtypeStruct((b, SEQ, FOUR_WIDTH), BF16),
        scratch_shapes=[pltpu.VMEM((2 * SEQ, FOUR_WIDTH), BF16)],
        compiler_params=_params("arbitrary"),
        name="fourier",
    )(pc, chan, even, odd)


def _head_norm(o):
    mu = jnp.mean(o, axis=-1, keepdims=True)
    d = o - mu
    var = jnp.mean(d * d, axis=-1, keepdims=True)
    return d * lax.rsqrt(var + EPS)


def _silu(x):
    return x * jax.nn.sigmoid(x)


def _retention_kernel(lg_ref, q_ref, k_ref, v_ref, gf_ref, gb_ref, kc_ref, vc_ref,
                      o_ref, acc_ref, df_ref, db_ref, xif_ref, xib_ref, zf_ref, zb_ref):
    blk = RET_BLOCK
    n_blk = SEQ // blk
    h = pl.program_id(0)
    lgf, lgb, gchf, gchb = lg_ref[0, h], lg_ref[1, h], lg_ref[2, h], lg_ref[3, h]

    @pl.when(pl.program_id(1) == 0)
    def _():
        row = lax.broadcasted_iota(jnp.int32, (blk, blk), 0)
        col = lax.broadcasted_iota(jnp.int32, (blk, blk), 1)
        diff = (row - col).astype(F32)
        df_ref[...] = jnp.where(diff >= 0, jnp.exp(jnp.maximum(diff, 0.0) * lgf), 0.0)
        db_ref[...] = jnp.where(diff <= 0, jnp.exp(jnp.maximum(-diff, 0.0) * lgb), 0.0)
        pos_v = lax.broadcasted_iota(jnp.int32, (blk, RET_DV), 0).astype(F32)
        xif_ref[...] = jnp.exp((pos_v + 1.0) * lgf)
        xib_ref[...] = jnp.exp((blk - pos_v) * lgb)
        pos_k = lax.broadcasted_iota(jnp.int32, (blk, RET_DK), 0).astype(F32)
        zf_ref[...] = jnp.exp((blk - 1.0 - pos_k) * lgf)
        zb_ref[...] = jnp.exp(pos_k * lgb)

    def state_update(r, k, z_ref, v, gch):
        kz = (k.astype(F32) * z_ref[...]).astype(BF16)
        return gch * r + lax.dot_general(kz, v, (((0,), (0,)), ((), ())), preferred_element_type=F32)

    def block_out(qb, kb, vb, r, d_ref, xi_ref):
        s = lax.dot_general(qb, kb, (((1,), (1,)), ((), ())), preferred_element_type=F32)
        inner = _dot((s * d_ref[...]).astype(BF16), vb)
        cross = _dot(qb, r.astype(BF16)) * xi_ref[...]
        return inner + cross

    def wide(ref, rows):
        return jnp.concatenate([ref[0, rows, :], ref[1, rows, :]], axis=1)

    zero = jnp.zeros((RET_DK, RET_DV), F32)
    r_f = state_update(zero, kc_ref[0], zf_ref, vc_ref[0], gchf)
    r_b = state_update(zero, kc_ref[0], zb_ref, vc_ref[0], gchb)

    def emit(i, gate_ref, normed, reached_first):
        rows = slice(i * blk, (i + 1) * blk)
        for c in range(RET_DV // LANES):
            cols = slice(c * LANES, (c + 1) * LANES)
            part = gate_ref[c, rows, :].astype(F32) * normed[:, cols]
            if reached_first:
                acc_ref[rows, cols] = part
            else:
                o_ref[0, rows, cols] = (acc_ref[rows, cols] + part).astype(o_ref.dtype)

    for j in range(n_blk):
        i_f, i_b = j, n_blk - 1 - j
        rows_f = slice(i_f * blk, (i_f + 1) * blk)
        rows_b = slice(i_b * blk, (i_b + 1) * blk)
        qf, kf, vf = q_ref[0, rows_f, :], k_ref[0, rows_f, :], wide(v_ref, rows_f)
        qk, kk, vk = q_ref[0, rows_b, :], k_ref[0, rows_b, :], wide(v_ref, rows_b)
        o_f = block_out(qf, kf, vf, r_f, df_ref, xif_ref)
        o_b = block_out(qk, kk, vk, r_b, db_ref, xib_ref)
        r_f = state_update(r_f, kf, zf_ref, vf, gchf)
        r_b = state_update(r_b, kk, zb_ref, vk, gchb)
        first_visit = j < n_blk - 1 - j
        emit(i_f, gf_ref, _head_norm(o_f), first_visit)
        emit(i_b, gb_ref, _head_norm(o_b), first_visit)


def _retention(pc, kvc3, lg):
    b = kvc3.shape[0]
    assert CTX_LEN == RET_BLOCK and (SEQ // RET_BLOCK) % 2 == 0 and RET_DK == LANES and RET_DV == 2 * LANES
    qb0, kb0 = Q_OFF // RET_DK, K_OFF // RET_DK
    vb0, gfb0, gbb0 = V_OFF // RET_DV, GF_OFF // RET_DV, GB_OFF // RET_DV
    kcv0 = (RET_HEADS * RET_DK) // RET_DV
    return pl.pallas_call(
        _retention_kernel,
        grid=(RET_HEADS, b),
        in_specs=[
            pl.BlockSpec(memory_space=pltpu.SMEM),
            pl.BlockSpec((1, SEQ, LANES), lambda h, i: (qb0 + h, i, 0)),
            pl.BlockSpec((1, SEQ, LANES), lambda h, i: (kb0 + h, i, 0)),
            pl.BlockSpec((RET_DV // LANES, SEQ, LANES), lambda h, i: (vb0 + h, i, 0)),
            pl.BlockSpec((RET_DV // LANES, SEQ, LANES), lambda h, i: (gfb0 + h, i, 0)),
            pl.BlockSpec((RET_DV // LANES, SEQ, LANES), lambda h, i: (gbb0 + h, i, 0)),
            pl.BlockSpec((1, CTX_LEN, RET_DK), lambda h, i: (i, 0, h)),
            pl.BlockSpec((1, CTX_LEN, RET_DV), lambda h, i: (i, 0, kcv0 + h)),
        ],
        out_specs=pl.BlockSpec((1, SEQ, RET_DV), lambda h, i: (i, 0, h)),
        out_shape=jax.ShapeDtypeStruct((b, SEQ, RET_HEADS * RET_DV), BF16),
        scratch_shapes=[
            pltpu.VMEM((SEQ, RET_DV), F32),
            pltpu.VMEM((RET_BLOCK, RET_BLOCK), F32),
            pltpu.VMEM((RET_BLOCK, RET_BLOCK), F32),
            pltpu.VMEM((RET_BLOCK, RET_DV), F32),
            pltpu.VMEM((RET_BLOCK, RET_DV), F32),
            pltpu.VMEM((RET_BLOCK, RET_DK), F32),
            pltpu.VMEM((RET_BLOCK, RET_DK), F32),
        ],
        compiler_params=_params("arbitrary", "arbitrary"),
        name="retention",
    )(lg, pc, pc, pc, pc, pc, kvc3, kvc3)


def _merge_kernel(fm_ref, ret_ref, wf_ref, wr_ref, mf_ref, mr_ref, o_ref, wfb_ref, wrb_ref):
    @pl.when(pl.program_id(1) == 0)
    def _():
        wfb_ref[...] = wf_ref[...].astype(BF16)
        wrb_ref[...] = wr_ref[...].astype(BF16)

    four = _dot(fm_ref[...], wfb_ref[...])
    ret = _dot(ret_ref[...], wrb_ref[...])
    for c in range(MERGE_COL_TILE // LANES):
        cols = slice(c * LANES, (c + 1) * LANES)
        merged = (jax.nn.sigmoid(mf_ref[c].astype(F32)) * four[:, cols]
                  + jax.nn.sigmoid(mr_ref[c].astype(F32)) * ret[:, cols])
        o_ref[:, cols] = merged.astype(o_ref.dtype)


def _merge(fm2, ret2, w_four_out, w_ret_out, pc, rows=512):
    m = fm2.shape[0]
    tile = MERGE_COL_TILE
    mf0, mr0 = MF_OFF // tile, MR_OFF // tile
    chunks = tile // LANES
    return pl.pallas_call(
        _merge_kernel,
        grid=(D_MODEL // tile, m // rows),
        in_specs=[
            pl.BlockSpec((rows, FOUR_WIDTH), lambda j, i: (i, 0)),
            pl.BlockSpec((rows, D_MODEL), lambda j, i: (i, 0)),
            pl.BlockSpec((FOUR_WIDTH, tile), lambda j, i: (0, j)),
            pl.BlockSpec((D_MODEL, tile), lambda j, i: (0, j)),
            pl.BlockSpec((chunks, rows, LANES), lambda j, i: (mf0 + j, i, 0)),
            pl.BlockSpec((chunks, rows, LANES), lambda j, i: (mr0 + j, i, 0)),
        ],
        out_specs=pl.BlockSpec((rows, tile), lambda j, i: (i, j)),
        out_shape=jax.ShapeDtypeStruct((m, D_MODEL), BF16),
        scratch_shapes=[pltpu.VMEM((FOUR_WIDTH, tile), BF16), pltpu.VMEM((D_MODEL, tile), BF16)],
        compiler_params=_params("arbitrary", "arbitrary"),
        name="merge",
    )(fm2, ret2, w_four_out, w_ret_out, pc, pc)


def _out_proj_kernel(a_ref, w_ref, x_ref, g_ref, o_ref, wb_ref):
    @pl.when(pl.program_id(1) == 0)
    def _():
        wb_ref[...] = w_ref[...].astype(BF16)

    o_ref[...] = x_ref[...] + g_ref[0] * _dot(a_ref[...], wb_ref[...])


def _out_proj(merged, w_out, x2, mod3, gate_blk0, rows=1024):
    m = merged.shape[0]
    return pl.pallas_call(
        _out_proj_kernel,
        grid=(D_MODEL // COL_TILE, m // rows),
        in_specs=[
            pl.BlockSpec((rows, D_MODEL), lambda j, i: (i, 0)),
            pl.BlockSpec((D_MODEL, COL_TILE), lambda j, i: (0, j)),
            pl.BlockSpec((rows, COL_TILE), lambda j, i: (i, j)),
            pl.BlockSpec((1, 1, COL_TILE), lambda j, i: (i * rows // SEQ, 0, gate_blk0 + j)),
        ],
        out_specs=pl.BlockSpec((rows, COL_TILE), lambda j, i: (i, j)),
        out_shape=jax.ShapeDtypeStruct((m, D_MODEL), F32),
        scratch_shapes=[pltpu.VMEM((D_MODEL, COL_TILE), BF16)],
        compiler_params=_params("arbitrary", "arbitrary"),
        name="out_proj",
    )(merged, w_out, x2, mod3)


def _route_kernel(x_ref, g_ref, sh_ref, sc_ref, whi_ref, wlo_ref, b_ref, r_ref):
    h = _rms_modulate(x_ref[...], g_ref[...], sh_ref[0], sc_ref[0])
    hi = h.astype(BF16)
    lo = (h - hi.astype(F32)).astype(BF16)
    logits = _dot(hi, whi_ref[...]) + _dot(lo, whi_ref[...]) + _dot(hi, wlo_ref[...]) + b_ref[...]

    lane = lax.broadcasted_iota(jnp.int32, logits.shape, 1).astype(F32)
    neg = -jnp.inf
    first = lambda hit: jnp.min(jnp.where(hit, lane, float(LANES)), axis=1, keepdims=True)
    is_grp = lane < float(N_GROUPS)
    gl = jnp.where(is_grp, logits, neg)
    gmax = jnp.max(gl, axis=1, keepdims=True)
    grp = first(gl == gmax)
    g_w = 1.0 / jnp.sum(jnp.where(is_grp, jnp.exp(logits - gmax), 0.0), axis=1, keepdims=True)
    lo_lane = float(N_GROUPS) + grp * float(EXPERTS_PER_GROUP)
    el = jnp.where(lane >= lo_lane, jnp.where(lane < lo_lane + float(EXPERTS_PER_GROUP), logits, neg), neg)
    v1 = jnp.max(el, axis=1, keepdims=True)
    i1 = first(el == v1)
    el2 = jnp.where(lane == i1, neg, el)
    v2 = jnp.max(el2, axis=1, keepdims=True)
    i2 = first(el2 == v2)
    e = jnp.exp(v2 - v1)
    w1 = g_w / (1.0 + e)
    w2 = g_w * e / (1.0 + e)
    out = jnp.where(lane == 0.0, i1 - float(N_GROUPS),
                    jnp.where(lane == 1.0, i2 - float(N_GROUPS),
                              jnp.where(lane == 2.0, w1, jnp.where(lane == 3.0, w2, 0.0))))
    r_ref[...] = out


def _route(x1, g, mod3, shift_blk, scale_blk, w_hi, w_lo, bias, rows=512):
    m = x1.shape[0]
    return pl.pallas_call(
        _route_kernel,
        grid=(m // rows,),
        in_specs=[
            pl.BlockSpec((rows, D_MODEL), lambda i: (i, 0)),
            pl.BlockSpec((1, D_MODEL), lambda i: (0, 0)),
            pl.BlockSpec((1, 1, D_MODEL), lambda i: (i * rows // SEQ, 0, shift_blk)),
            pl.BlockSpec((1, 1, D_MODEL), lambda i: (i * rows // SEQ, 0, scale_blk)),
            pl.BlockSpec((D_MODEL, LANES), lambda i: (0, 0)),
            pl.BlockSpec((D_MODEL, LANES), lambda i: (0, 0)),
            pl.BlockSpec((1, LANES), lambda i: (0, 0)),
        ],
        out_specs=pl.BlockSpec((rows, LANES), lambda i: (i, 0)),
        out_shape=jax.ShapeDtypeStruct((m, LANES), F32),
        compiler_params=_params("arbitrary"),
        name="route",
    )(x1, g, mod3, mod3, w_hi, w_lo, bias)


def _dispatch_plan(route):
    rows = EXPERT_ROWS
    flat = route[:, :2].astype(jnp.int32).reshape(-1)
    n_tiles = flat.shape[0] // rows
    experts = jnp.arange(N_EXPERTS, dtype=jnp.int32)
    onehot = (flat[:, None] == experts[None, :]).astype(jnp.int32)
    csum = jnp.cumsum(onehot, axis=0)
    rank = jnp.sum(csum * onehot, axis=1) - 1
    counts = csum[-1]
    cend = jnp.cumsum(counts)
    cstart = cend - counts
    pos = jnp.sum(cstart[None, :] * onehot, axis=1) + rank
    first_tile = cstart // rows
    items = jnp.where(counts > 0, (cend - 1) // rows - first_tile + 1, 0)
    wend = jnp.cumsum(items)
    wstart = wend - items
    n_work = wend[-1]
    w = jnp.arange(N_WORK_ITEMS, dtype=jnp.int32)
    valid = w < n_work
    last_expert = jnp.max(jnp.where(counts > 0, experts, 0))
    e_w = jnp.minimum(jnp.sum((wend[None, :] <= w[:, None]).astype(jnp.int32), axis=1), N_EXPERTS - 1)
    e_w = jnp.where(valid, e_w, last_expert)
    sel = (e_w[:, None] == experts[None, :]).astype(jnp.int32)
    pick = lambda v: jnp.sum(sel * v[None, :], axis=1)
    tile_w = jnp.where(valid, pick(first_tile) + w - pick(wstart), n_tiles - 1)
    lo_w = jnp.where(valid, jnp.clip(pick(cstart) - tile_w * rows, 0, rows), 0)
    hi_w = jnp.where(valid, jnp.clip(pick(cend) - tile_w * rows, 0, rows), 0)
    prev = lambda v: jnp.concatenate([jnp.full((1,), -1, jnp.int32), v[:-1]])
    new_expert = (e_w != prev(e_w)).astype(jnp.int32)
    new_tile = (tile_w != prev(tile_w)).astype(jnp.int32)
    later = jnp.logical_and(counts[None, :] > 0, experts[None, :] > experts[:, None])
    next_of = jnp.min(jnp.where(later, experts[None, :], N_EXPERTS), axis=1)
    next_of = jnp.where(next_of < N_EXPERTS, next_of, -1)
    buf_w = (jnp.cumsum(new_expert) - 1) % 2

    def follow(e):
        hit = (e[:, None] == experts[None, :]).astype(jnp.int32)
        return jnp.where(e >= 0, jnp.sum(hit * next_of[None, :], axis=1), -1)

    nxt1 = follow(e_w)
    nxt2 = follow(nxt1)
    nxt3 = follow(nxt2)
    plan = jnp.stack([tile_w, e_w, new_expert, new_tile, lo_w, hi_w,
                      jnp.broadcast_to(n_work, w.shape), nxt1, buf_w, nxt2, nxt3]).astype(jnp.int32)
    return pos, plan


def _dispatch_kernel(pos_ref, x_ref, g_ref, sh_ref, sc_ref, xs_hbm, hbuf, sem):
    rows = DISPATCH_ROWS
    i = pl.program_id(0)
    n = pl.num_programs(0)
    slot = i % 2

    def row_copy(s, r, dst_row):
        return pltpu.make_async_copy(hbuf.at[s, pl.ds(r, 1), :], xs_hbm.at[pl.ds(dst_row, 1), :], sem.at[s])

    def drain(s):
        def body(r, c):
            row_copy(s, 0, 0).wait()
            return c
        lax.fori_loop(0, 2 * rows, body, 0, unroll=8)

    @pl.when(i >= 2)
    def _():
        drain(slot)

    hbuf[slot] = _rms_modulate(x_ref[...], g_ref[...], sh_ref[0], sc_ref[0])

    def issue(r, c):
        row_copy(slot, r, pos_ref[0, 0, r]).start(priority=0)
        row_copy(slot, r, pos_ref[0, 0, rows + r]).start(priority=1)
        return c
    lax.fori_loop(0, rows, issue, 0, unroll=8)

    @pl.when(i == n - 1)
    def _():
        drain(slot)
        drain(1 - slot)


def _dispatch(x1, g, mod3, shift_blk, scale_blk, pos3, n_sorted):
    rows = DISPATCH_ROWS
    m = x1.shape[0]
    assert m // rows >= 2
    return pl.pallas_call(
        _dispatch_kernel,
        grid=(m // rows,),
        in_specs=[
            pl.BlockSpec((1, 1, 2 * rows), lambda i: (i, 0, 0), memory_space=pltpu.SMEM),
            pl.BlockSpec((rows, D_MODEL), lambda i: (i, 0)),
            pl.BlockSpec((1, D_MODEL), lambda i: (0, 0)),
            pl.BlockSpec((1, 1, D_MODEL), lambda i: (i * rows // SEQ, 0, shift_blk)),
            pl.BlockSpec((1, 1, D_MODEL), lambda i: (i * rows // SEQ, 0, scale_blk)),
        ],
        out_specs=pl.BlockSpec(memory_space=pl.ANY),
        out_shape=jax.ShapeDtypeStruct((n_sorted, D_MODEL), F32),
        scratch_shapes=[pltpu.VMEM((2, rows, D_MODEL), F32), pltpu.SemaphoreType.DMA((2,))],
        compiler_params=_params("arbitrary"),
        name="dispatch",
    )(pos3, x1, g, mod3, mod3)


def _experts_kernel(plan_ref, x_ref, wg_hbm, wu_hbm, wd_hbm, y_ref,
                    wg_buf, wu_buf, wd_buf, sem, wgb_ref, wub_ref, wdb_ref):
    w = pl.program_id(0)

    def weight_copies(e, s):
        return (pltpu.make_async_copy(wg_hbm.at[e], wg_buf.at[s], sem.at[s, 0]),
                pltpu.make_async_copy(wu_hbm.at[e], wu_buf.at[s], sem.at[s, 1]),
                pltpu.make_async_copy(wd_hbm.at[e], wd_buf.at[s], sem.at[s, 2]))

    def fetch(e, s):
        for cp in weight_copies(e, s):
            cp.start()

    def take(e, s):
        for cp in weight_copies(e, s):
            cp.wait()
        wgb_ref[s] = wg_buf[s].astype(BF16)
        wub_ref[s] = wu_buf[s].astype(BF16)
        wdb_ref[s] = wd_buf[s].astype(BF16)

    s = plan_ref[8, w]
    nxt1, nxt2, nxt3 = plan_ref[7, w], plan_ref[9, w], plan_ref[10, w]

    @pl.when(w == 0)
    def _():
        fetch(plan_ref[1, 0], 0)

        @pl.when(nxt1 >= 0)
        def _():
            fetch(nxt1, 1)

        take(plan_ref[1, 0], 0)

        @pl.when(nxt2 >= 0)
        def _():
            fetch(nxt2, 0)

    @pl.when(w < plan_ref[6, w])
    def _():
        @pl.when(jnp.logical_and(plan_ref[2, w] == 1, nxt1 >= 0))
        def _():
            take(nxt1, 1 - s)

            @pl.when(nxt3 >= 0)
            def _():
                fetch(nxt3, 1 - s)

        x = x_ref[...].astype(BF16)
        hg = _dot(x, wgb_ref[s])
        hu = _dot(x, wub_ref[s])
        hid = _silu(hg) * hu
        y = _dot(hid.astype(BF16), wdb_ref[s])
        row = lax.broadcasted_iota(jnp.int32, (EXPERT_ROWS, 1), 0)
        mine = jnp.logical_and(row >= plan_ref[4, w], row < plan_ref[5, w])

        @pl.when(plan_ref[3, w] == 1)
        def _():
            y_ref[...] = jnp.where(mine, y, 0.0)

        @pl.when(plan_ref[3, w] == 0)
        def _():
            y_ref[...] = jnp.where(mine, y, y_ref[...])


def _experts(xs, plan, w_gate, w_up, w_down):
    rows = EXPERT_ROWS
    grid_spec = pltpu.PrefetchScalarGridSpec(
        num_scalar_prefetch=1,
        grid=(N_WORK_ITEMS,),
        in_specs=[
            pl.BlockSpec((rows, D_MODEL), lambda w, p: (p[0, w], 0)),
            pl.BlockSpec(memory_space=pl.ANY),
            pl.BlockSpec(memory_space=pl.ANY),
            pl.BlockSpec(memory_space=pl.ANY),
        ],
        out_specs=pl.BlockSpec((rows, D_MODEL), lambda w, p: (p[0, w], 0)),
        scratch_shapes=[
            pltpu.VMEM((2, D_MODEL, EXPERT_FF), F32),
            pltpu.VMEM((2, D_MODEL, EXPERT_FF), F32),
            pltpu.VMEM((2, EXPERT_FF, D_MODEL), F32),
            pltpu.SemaphoreType.DMA((2, 3)),
            pltpu.VMEM((2, D_MODEL, EXPERT_FF), BF16),
            pltpu.VMEM((2, D_MODEL, EXPERT_FF), BF16),
            pltpu.VMEM((2, EXPERT_FF, D_MODEL), BF16),
        ],
    )
    return pl.pallas_call(
        _experts_kernel,
        grid_spec=grid_spec,
        out_shape=jax.ShapeDtypeStruct(xs.shape, F32),
        compiler_params=_params("arbitrary"),
        name="experts",
    )(plan, xs, w_gate, w_up, w_down)


def _final_kernel(pos_ref, posn_ref, x_ref, r_ref, g2_ref, fg_ref, y_hbm, o_ref, ybuf, sem):
    rows = FINAL_ROWS
    i = pl.program_id(0)
    n = pl.num_programs(0)
    slot = i % 2

    def row_copy(src_row, s, r):
        return pltpu.make_async_copy(y_hbm.at[pl.ds(src_row, 1), :], ybuf.at[s, pl.ds(r, 1), :], sem.at[s])

    def issue(idx_ref, s):
        def body(r, c):
            row_copy(idx_ref[0, 0, r], s, r).start(priority=0)
            row_copy(idx_ref[0, 0, rows + r], s, rows + r).start(priority=1)
            return c
        lax.fori_loop(0, rows, body, 0, unroll=8)

    @pl.when(i == 0)
    def _():
        issue(pos_ref, 0)

    @pl.when(i + 1 < n)
    def _():
        issue(posn_ref, 1 - slot)

    def wait_body(r, c):
        row_copy(0, slot, r).wait()
        return c
    lax.fori_loop(0, 2 * rows, wait_body, 0, unroll=8)

    route = r_ref[...]
    w1 = route[:, 2:3]
    w2 = route[:, 3:4]
    moe = w1 * ybuf[slot, 0:rows, :] + w2 * ybuf[slot, rows:2 * rows, :]
    x = x_ref[...] + g2_ref[0] * moe
    o_ref[...] = x * lax.rsqrt(jnp.mean(x * x, axis=-1, keepdims=True) + EPS) * fg_ref[...]


def _final(x1, route, mod3, gate_blk, final_g, y, pos3):
    rows = FINAL_ROWS
    m = x1.shape[0]
    nt = m // rows
    return pl.pallas_call(
        _final_kernel,
        grid=(nt,),
        in_specs=[
            pl.BlockSpec((1, 1, 2 * rows), lambda i: (i, 0, 0), memory_space=pltpu.SMEM),
            pl.BlockSpec((1, 1, 2 * rows), lambda i: (jnp.minimum(i + 1, nt - 1), 0, 0), memory_space=pltpu.SMEM),
            pl.BlockSpec((rows, D_MODEL), lambda i: (i, 0)),
            pl.BlockSpec((rows, LANES), lambda i: (i, 0)),
            pl.BlockSpec((1, 1, D_MODEL), lambda i: (i * rows // SEQ, 0, gate_blk)),
            pl.BlockSpec((1, D_MODEL), lambda i: (0, 0)),
            pl.BlockSpec(memory_space=pl.ANY),
        ],
        out_specs=pl.BlockSpec((rows, D_MODEL), lambda i: (i, 0)),
        out_shape=jax.ShapeDtypeStruct((m, D_MODEL), F32),
        scratch_shapes=[pltpu.VMEM((2, 2 * rows, D_MODEL), F32), pltpu.SemaphoreType.DMA((2,))],
        compiler_params=_params("arbitrary"),
        name="final",
    )(pos3, pos3, x1, route, mod3, final_g, y)


def _rope_tables():
    rows = SEQ // GRID_W
    row = jnp.repeat(jnp.arange(rows, dtype=F32), GRID_W)
    col = jnp.tile(jnp.arange(GRID_W, dtype=F32), rows)
    n_freq = RET_DK // 4
    inv = ROPE_BASE ** (-jnp.arange(n_freq, dtype=F32) / n_freq)
    ang = jnp.concatenate([row[:, None] * inv, col[:, None] * inv], axis=-1)
    cos, sin = jnp.cos(ang), jnp.sin(ang)
    return jnp.concatenate([cos, cos], axis=-1), jnp.concatenate([-sin, sin], axis=-1)


def kernel(x, c, ctx, c_ctx, w_mod, b_mod, norm1_g, norm2_g, w_in, w_four_out, w_ret_out, w_out,
           ret_decay_f, ret_decay_b, w_group_router, b_group_router, w_expert_router, b_expert_router,
           w_gate, w_up, w_down, final_norm_g):
    b, n, d = x.shape
    assert (n, d) == (SEQ, D_MODEL) and ctx.shape[1] == CTX_LEN and w_mod.shape[0] == 1
    t = b * n

    c8 = jnp.zeros((8, d), F32).at[:b].set(c).at[b].set(c_ctx)
    mod3 = _modulation(c8, w_mod[0], b_mod).reshape(8, 1, N_MOD * d)

    hx = _prenorm(x, norm1_g, mod3, lambda i: i, 0, 1, rows=1024)
    hc = _prenorm(ctx, norm1_g, mod3, lambda i: b, 0, 1, rows=CTX_LEN)

    cos2, sin2 = _rope_tables()
    pc = _in_proj(hx.reshape(t, d), w_in[0], cos2, sin2)
    kvc = _project(hc.reshape(b * CTX_LEN, d), w_in[0], K_OFF // COL_TILE, (GF_OFF - K_OFF) // COL_TILE,
                   b * CTX_LEN, "ctx_proj")
    kvc3 = kvc.reshape(b, CTX_LEN, GF_OFF - K_OFF)

    fm = _fourier(pc, b)

    lg_f = jax.nn.log_sigmoid(ret_decay_f[0].astype(F32))
    lg_b = jax.nn.log_sigmoid(ret_decay_b[0].astype(F32))
    lg = jnp.stack([lg_f, lg_b, jnp.exp(RET_BLOCK * lg_f), jnp.exp(RET_BLOCK * lg_b)])
    ret = _retention(pc, kvc3, lg)

    merged = _merge(fm.reshape(t, FOUR_WIDTH), ret.reshape(t, d), w_four_out[0], w_ret_out[0], pc)
    x1 = _out_proj(merged, w_out[0], x.reshape(t, d), mod3, 2 * d // COL_TILE)

    w_router = jnp.concatenate(
        [w_group_router[0], w_expert_router[0].transpose(1, 0, 2).reshape(d, N_EXPERTS)], axis=1).astype(F32)
    w_router = jnp.pad(w_router, ((0, 0), (0, LANES - w_router.shape[1])))
    w_hi = w_router.astype(BF16)
    w_lo = (w_router - w_hi.astype(F32)).astype(BF16)
    bias = jnp.pad(jnp.concatenate([b_group_router[0], b_expert_router[0].reshape(-1)]).astype(F32),
                   (0, LANES - N_GROUPS - N_EXPERTS)).reshape(1, LANES)
    route = _route(x1, norm2_g, mod3, 3, 4, w_hi, w_lo, bias)

    pos, plan = _dispatch_plan(route)
    assert DISPATCH_ROWS == FINAL_ROWS
    pos3 = pos.reshape(t // FINAL_ROWS, FINAL_ROWS, 2).transpose(0, 2, 1).reshape(t // FINAL_ROWS, 1, 2 * FINAL_ROWS)
    xs = _dispatch(x1, norm2_g, mod3, 3, 4, pos3, 2 * t)
    y = _experts(xs, plan, w_gate[0], w_up[0], w_down[0])
    out = _final(x1, route, mod3, 5, final_norm_g.reshape(1, d), y, pos3)
    return out.reshape(b, n, d)
```

```python
import jax
import jax.numpy as jnp
import numpy as np
from jax import lax
from jax.experimental import pallas as pl
from jax.experimental.pallas import tpu as pltpu

F32 = jnp.float32
BF16 = jnp.bfloat16

D_MODEL = 2048
SEQ = 2048
CTX_LEN = 256
GRID_W = 64
N_MOD = 6
FOUR_GROUPS = 8
FOUR_GROUP_DIM = 128
FOUR_WIDTH = FOUR_GROUPS * FOUR_GROUP_DIM
RET_HEADS = 8
RET_DK = 128
RET_DV = 256
ROPE_BASE = 10000.0
N_GROUPS = 4
EXPERTS_PER_GROUP = 8
N_EXPERTS = N_GROUPS * EXPERTS_PER_GROUP
EXPERT_FF = 512
EPS = 1e-6
Q_OFF = FOUR_WIDTH
K_OFF = Q_OFF + RET_HEADS * RET_DK
V_OFF = K_OFF + RET_HEADS * RET_DK
GF_OFF = V_OFF + RET_HEADS * RET_DV
GB_OFF = GF_OFF + RET_HEADS * RET_DV
MF_OFF = GB_OFF + RET_HEADS * RET_DV
MR_OFF = MF_OFF + D_MODEL
IN_WIDTH = MR_OFF + D_MODEL

V7X_VMEM_LIMIT_BYTES = 56 * 1024 * 1024
LANES = 128
RET_BLOCK = 256
COL_TILE = 1024
MERGE_COL_TILE = 1024
IN_COL_TILE = 1024
IN_ROWS = 1024
EXPERT_ROWS = 256
DISPATCH_ROWS = 1024
FINAL_ROWS = 256


def _params(*sem):
    return pltpu.CompilerParams(dimension_semantics=sem, vmem_limit_bytes=V7X_VMEM_LIMIT_BYTES)


def _dot(a, b):
    return jnp.dot(a, b, preferred_element_type=F32)


def _mod_kernel(c_ref, w_ref, b_ref, o_ref):
    c = c_ref[...]
    sc = c * jax.nn.sigmoid(c)
    o_ref[...] = _dot(sc.astype(BF16), w_ref[...].astype(BF16)) + b_ref[...]


def _modulation(c8, w_mod, b_mod):
    n_out = w_mod.shape[1]
    return pl.pallas_call(
        _mod_kernel,
        grid=(n_out // COL_TILE,),
        in_specs=[
            pl.BlockSpec((8, D_MODEL), lambda j: (0, 0)),
            pl.BlockSpec((D_MODEL, COL_TILE), lambda j: (0, j)),
            pl.BlockSpec((1, COL_TILE), lambda j: (0, j)),
        ],
        out_specs=pl.BlockSpec((8, COL_TILE), lambda j: (0, j)),
        out_shape=jax.ShapeDtypeStruct((8, n_out), F32),
        compiler_params=_params("arbitrary"),
        name="mod",
    )(c8, w_mod, b_mod)


def _rms_modulate(x, g, shift, scale):
    y = x * lax.rsqrt(jnp.mean(x * x, axis=-1, keepdims=True) + EPS) * g
    return y * (1.0 + scale) + shift


def _prenorm_kernel(x_ref, g_ref, sh_ref, sc_ref, o_ref):
    o_ref[0] = _rms_modulate(x_ref[0], g_ref[...], sh_ref[0], sc_ref[0]).astype(o_ref.dtype)


def _prenorm(x, g, mod3, shift_blk, scale_blk, rows):
    b, n, d = x.shape
    return pl.pallas_call(
        _prenorm_kernel,
        grid=(b, n // rows),
        in_specs=[
            pl.BlockSpec((1, rows, d), lambda i, j: (i, j, 0)),
            pl.BlockSpec((1, d), lambda i, j: (0, 0)),
            pl.BlockSpec((1, 1, d), lambda i, j: (i, 0, shift_blk)),
            pl.BlockSpec((1, 1, d), lambda i, j: (i, 0, scale_blk)),
        ],
        out_specs=pl.BlockSpec((1, rows, d), lambda i, j: (i, j, 0)),
        out_shape=jax.ShapeDtypeStruct((b, n, d), BF16),
        compiler_params=_params("arbitrary", "arbitrary"),
        name="prenorm",
    )(x, g, mod3, mod3)


def _ctx_proj_kernel(x_ref, g_ref, sh_ref, sc_ref, w_ref, o_ref, a_ref, wb_ref):
    @pl.when(pl.program_id(0) == 0)
    def _():
        a_ref[...] = _rms_modulate(x_ref[...], g_ref[...], sh_ref[0], sc_ref[0]).astype(BF16)

    wb_ref[...] = w_ref[...].astype(BF16)
    o_ref[...] = _dot(a_ref[...], wb_ref[...]).astype(o_ref.dtype)


def _ctx_proj(ctx2, g, mod3, mod_row, w, col_tile0, n_col_tiles):
    m, k = ctx2.shape
    return pl.pallas_call(
        _ctx_proj_kernel,
        grid=(n_col_tiles,),
        in_specs=[
            pl.BlockSpec((m, k), lambda j: (0, 0)),
            pl.BlockSpec((1, k), lambda j: (0, 0)),
            pl.BlockSpec((1, 1, k), lambda j: (mod_row, 0, 0)),
            pl.BlockSpec((1, 1, k), lambda j: (mod_row, 0, 1)),
            pl.BlockSpec((k, COL_TILE), lambda j: (0, j + col_tile0)),
        ],
        out_specs=pl.BlockSpec((m, COL_TILE), lambda j: (0, j)),
        out_shape=jax.ShapeDtypeStruct((m, n_col_tiles * COL_TILE), BF16),
        scratch_shapes=[pltpu.VMEM((m, k), BF16), pltpu.VMEM((k, COL_TILE), BF16)],
        compiler_params=_params("arbitrary"),
        name="ctx_proj",
    )(ctx2, g, mod3, mod3, w)


def _rope(t, cos2, sin2):
    return t * cos2 + pltpu.roll(t, RET_DK // 2, axis=1) * sin2


def _in_proj_kernel(a_ref, w_ref, cos_ref, sin_ref, o_ref, wb_ref, acc_ref):
    j = pl.program_id(0)

    @pl.when(pl.program_id(1) == 0)
    def _():
        wb_ref[...] = w_ref[...].astype(BF16)

    q_tile, k_tile, v_tile = Q_OFF // IN_COL_TILE, K_OFF // IN_COL_TILE, V_OFF // IN_COL_TILE
    gate_tile, merge_tile = GF_OFF // IN_COL_TILE, MF_OFF // IN_COL_TILE
    is_qk = jnp.logical_and(j >= q_tile, j < v_tile)
    is_gate = jnp.logical_and(j >= gate_tile, j < merge_tile)

    def store_chunks(val):
        for c in range(IN_COL_TILE // LANES):
            o_ref[c] = val[:, c * LANES:(c + 1) * LANES].astype(o_ref.dtype)

    is_four = j < q_tile

    @pl.when(jnp.logical_not(jnp.logical_or(jnp.logical_or(is_qk, is_gate), is_four)))
    def _():
        store_chunks(_dot(a_ref[...], wb_ref[...]))

    @pl.when(is_four)
    def _():
        acc = _dot(a_ref[...], wb_ref[...])
        half = acc_ref.shape[1] // 2
        for c in range(IN_COL_TILE // LANES):
            acc_ref[c] = acc[:, c * LANES:(c + 1) * LANES]
            for parity in (0, 1):
                val = acc_ref[c, pl.ds(parity, half, stride=2), :]
                o_ref[c, parity * half:(parity + 1) * half, :] = val.astype(o_ref.dtype)

    @pl.when(is_qk)
    def _():
        acc = _dot(a_ref[...], wb_ref[...])
        scale = jnp.where(j < k_tile, RET_DK ** -0.5, 1.0).astype(F32)
        cos2, sin2 = cos_ref[...], sin_ref[...]
        assert RET_DK == LANES
        for h in range(IN_COL_TILE // RET_DK):
            cols = slice(h * RET_DK, (h + 1) * RET_DK)
            o_ref[h] = (_rope(acc[:, cols], cos2, sin2) * scale).astype(o_ref.dtype)

    @pl.when(is_gate)
    def _():
        store_chunks(_silu(_dot(a_ref[...], wb_ref[...])))


def _in_proj(a, w, cos2, sin2, rows=IN_ROWS):
    m, k = a.shape
    n_col_tiles = IN_WIDTH // IN_COL_TILE
    pos_blocks = SEQ // rows
    return pl.pallas_call(
        _in_proj_kernel,
        grid=(n_col_tiles, m // rows),
        in_specs=[
            pl.BlockSpec((rows, k), lambda j, i: (i, 0)),
            pl.BlockSpec((k, IN_COL_TILE), lambda j, i: (0, j)),
            pl.BlockSpec((rows, RET_DK), lambda j, i: (i % pos_blocks, 0)),
            pl.BlockSpec((rows, RET_DK), lambda j, i: (i % pos_blocks, 0)),
        ],
        out_specs=pl.BlockSpec((IN_COL_TILE // LANES, rows, LANES), lambda j, i: (j, i, 0)),
        out_shape=jax.ShapeDtypeStruct((IN_WIDTH // LANES, m, LANES), BF16),
        scratch_shapes=[pltpu.VMEM((k, IN_COL_TILE), BF16),
                        pltpu.VMEM((IN_COL_TILE // LANES, rows, LANES), F32)],
        compiler_params=_params("arbitrary", "arbitrary"),
        name="in_proj",
    )(a, w, cos2, sin2)


def _dft_tables():
    half = SEQ // 2
    k = np.arange(half, dtype=np.int64)[:, None]
    n2 = np.arange(half, dtype=np.int64)[None, :]
    tables = []
    for parity in (0, 1):
        ang = 2.0 * np.pi * ((k * (2 * n2 + parity)) % SEQ) / SEQ
        tables.append(np.concatenate([np.cos(ang), -np.sin(ang)], axis=1).astype(np.float32))
    c = np.arange(FOUR_GROUP_DIM, dtype=np.int64)
    ang_c = 2.0 * np.pi * ((c[:, None] * c[None, :]) % FOUR_GROUP_DIM) / FOUR_GROUP_DIM
    norm = 1.0 / np.sqrt(float(SEQ * FOUR_GROUP_DIM))
    chan = (np.concatenate([np.cos(ang_c), np.sin(ang_c)], axis=1) * norm).astype(np.float32)
    return tables[0], tables[1], chan


def _fourier_kernel(u_ref, chan_ref, even_ref, odd_ref, o_ref, y_ref):
    half = SEQ // 2
    tile_half = IN_ROWS // 2
    chan = chan_ref[...]
    for g in range(FOUR_GROUPS):
        cols = slice(g * FOUR_GROUP_DIM, (g + 1) * FOUR_GROUP_DIM)
        y = _dot(u_ref[g], chan).astype(BF16)
        for t in range(SEQ // IN_ROWS):
            for parity in (0, 1):
                src = slice(t * IN_ROWS + parity * tile_half, t * IN_ROWS + (parity + 1) * tile_half)
                dst = 2 * parity * half + t * tile_half
                y_ref[dst:dst + tile_half, cols] = y[src, :FOUR_GROUP_DIM]
                y_ref[half + dst:half + dst + tile_half, cols] = y[src, FOUR_GROUP_DIM:]
    e = _dot(even_ref[...], y_ref[0:SEQ, :])
    o = _dot(odd_ref[...], y_ref[SEQ:2 * SEQ, :])
    o_ref[0, 0:half, :] = (e + o).astype(o_ref.dtype)
    o_ref[0, half:SEQ, :] = (e - o).astype(o_ref.dtype)


def _fourier(pc, b):
    assert FOUR_GROUP_DIM == LANES and FOUR_WIDTH == IN_COL_TILE and SEQ % IN_ROWS == 0
    even_np, odd_np, chan_np = _dft_tables()
    even = jnp.asarray(even_np).astype(BF16)
    odd = jnp.asarray(odd_np).astype(BF16)
    chan = jnp.asarray(chan_np).astype(BF16)
    return pl.pallas_call(
        _fourier_kernel,
        grid=(b,),
        in_specs=[
            pl.BlockSpec((FOUR_GROUPS, SEQ, LANES), lambda i: (0, i, 0)),
            pl.BlockSpec((FOUR_GROUP_DIM, 2 * FOUR_GROUP_DIM), lambda i: (0, 0)),
            pl.BlockSpec((SEQ // 2, SEQ), lambda i: (0, 0)),
            pl.BlockSpec((SEQ // 2, SEQ), lambda i: (0, 0)),
        ],
        out_specs=pl.BlockSpec((1, SEQ, FOUR_WIDTH), lambda i: (i, 0, 0)),
        out_shape=jax.ShapeDtypeStruct((b, SEQ, FOUR_WIDTH), BF16),
        scratch_shapes=[pltpu.VMEM((2 * SEQ, FOUR_WIDTH), BF16)],
        compiler_params=_params("arbitrary"),
        name="fourier",
    )(pc, chan, even, odd)


def _head_norm(o):
    mu = jnp.mean(o, axis=-1, keepdims=True)
    d = o - mu
    var = jnp.mean(d * d, axis=-1, keepdims=True)
    return d * lax.rsqrt(var + EPS)


def _silu(x):
    return x * jax.nn.sigmoid(x)


def _retention_kernel(lg_ref, q_ref, k_ref, v_ref, gf_ref, gb_ref, kc_ref, vc_ref,
                      o_ref, acc_ref, df_ref, db_ref, xif_ref, xib_ref, zf_ref, zb_ref):
    blk = RET_BLOCK
    n_blk = SEQ // blk
    h = pl.program_id(0)
    lgf, lgb, gchf, gchb = lg_ref[0, h], lg_ref[1, h], lg_ref[2, h], lg_ref[3, h]

    @pl.when(pl.program_id(1) == 0)
    def _():
        row = lax.broadcasted_iota(jnp.int32, (blk, blk), 0)
        col = lax.broadcasted_iota(jnp.int32, (blk, blk), 1)
        diff = (row - col).astype(F32)
        df_ref[...] = jnp.where(diff >= 0, jnp.exp(jnp.maximum(diff, 0.0) * lgf), 0.0)
        db_ref[...] = jnp.where(diff <= 0, jnp.exp(jnp.maximum(-diff, 0.0) * lgb), 0.0)
        pos_v = lax.broadcasted_iota(jnp.int32, (blk, RET_DV), 0).astype(F32)
        xif_ref[...] = jnp.exp((pos_v + 1.0) * lgf)
        xib_ref[...] = jnp.exp((blk - pos_v) * lgb)
        pos_k = lax.broadcasted_iota(jnp.int32, (blk, RET_DK), 0).astype(F32)
        zf_ref[...] = jnp.exp((blk - 1.0 - pos_k) * lgf)
        zb_ref[...] = jnp.exp(pos_k * lgb)

    def state_update(r, k, z_ref, v, gch):
        kz = (k.astype(F32) * z_ref[...]).astype(BF16)
        return gch * r + lax.dot_general(kz, v, (((0,), (0,)), ((), ())), preferred_element_type=F32)

    def block_out(qb, kb, vb, r, d_ref, xi_ref):
        s = lax.dot_general(qb, kb, (((1,), (1,)), ((), ())), preferred_element_type=F32)
        inner = _dot((s * d_ref[...]).astype(BF16), vb)
        cross = _dot(qb, r.astype(BF16)) * xi_ref[...]
        return inner + cross

    def wide(ref, rows):
        return jnp.concatenate([ref[0, rows, :], ref[1, rows, :]], axis=1)

    zero = jnp.zeros((RET_DK, RET_DV), F32)
    r_f = state_update(zero, kc_ref[0], zf_ref, vc_ref[0], gchf)
    r_b = state_update(zero, kc_ref[0], zb_ref, vc_ref[0], gchb)

    def emit(i, gate_ref, normed, reached_first):
        rows = slice(i * blk, (i + 1) * blk)
        for c in range(RET_DV // LANES):
            cols = slice(c * LANES, (c + 1) * LANES)
            part = gate_ref[c, rows, :].astype(F32) * normed[:, cols]
            if reached_first:
                acc_ref[rows, cols] = part
            else:
                o_ref[0, rows, cols] = (acc_ref[rows, cols] + part).astype(o_ref.dtype)

    for j in range(n_blk):
        i_f, i_b = j, n_blk - 1 - j
        rows_f = slice(i_f * blk, (i_f + 1) * blk)
        rows_b = slice(i_b * blk, (i_b + 1) * blk)
        qf, kf, vf = q_ref[0, rows_f, :], k_ref[0, rows_f, :], wide(v_ref, rows_f)
        qk, kk, vk = q_ref[0, rows_b, :], k_ref[0, rows_b, :], wide(v_ref, rows_b)
        o_f = block_out(qf, kf, vf, r_f, df_ref, xif_ref)
        o_b = block_out(qk, kk, vk, r_b, db_ref, xib_ref)
        r_f = state_update(r_f, kf, zf_ref, vf, gchf)
        r_b = state_update(r_b, kk, zb_ref, vk, gchb)
        first_visit = j < n_blk - 1 - j
        emit(i_f, gf_ref, _head_norm(o_f), first_visit)
        emit(i_b, gb_ref, _head_norm(o_b), first_visit)


def _retention(pc, kvc3, lg):
    b = kvc3.shape[0]
    assert CTX_LEN == RET_BLOCK and (SEQ // RET_BLOCK) % 2 == 0 and RET_DK == LANES and RET_DV == 2 * LANES
    qb0, kb0 = Q_OFF // RET_DK, K_OFF // RET_DK
    vb0, gfb0, gbb0 = V_OFF // RET_DV, GF_OFF // RET_DV, GB_OFF // RET_DV
    kcv0 = (RET_HEADS * RET_DK) // RET_DV
    return pl.pallas_call(
        _retention_kernel,
        grid=(RET_HEADS, b),
        in_specs=[
            pl.BlockSpec(memory_space=pltpu.SMEM),
            pl.BlockSpec((1, SEQ, LANES), lambda h, i: (qb0 + h, i, 0)),
            pl.BlockSpec((1, SEQ, LANES), lambda h, i: (kb0 + h, i, 0)),
            pl.BlockSpec((RET_DV // LANES, SEQ, LANES), lambda h, i: (vb0 + h, i, 0)),
            pl.BlockSpec((RET_DV // LANES, SEQ, LANES), lambda h, i: (gfb0 + h, i, 0)),
            pl.BlockSpec((RET_DV // LANES, SEQ, LANES), lambda h, i: (gbb0 + h, i, 0)),
            pl.BlockSpec((1, CTX_LEN, RET_DK), lambda h, i: (i, 0, h)),
            pl.BlockSpec((1, CTX_LEN, RET_DV), lambda h, i: (i, 0, kcv0 + h)),
        ],
        out_specs=pl.BlockSpec((1, SEQ, RET_DV), lambda h, i: (i, 0, h)),
        out_shape=jax.ShapeDtypeStruct((b, SEQ, RET_HEADS * RET_DV), BF16),
        scratch_shapes=[
            pltpu.VMEM((SEQ, RET_DV), F32),
            pltpu.VMEM((RET_BLOCK, RET_BLOCK), F32),
            pltpu.VMEM((RET_BLOCK, RET_BLOCK), F32),
            pltpu.VMEM((RET_BLOCK, RET_DV), F32),
            pltpu.VMEM((RET_BLOCK, RET_DV), F32),
            pltpu.VMEM((RET_BLOCK, RET_DK), F32),
            pltpu.VMEM((RET_BLOCK, RET_DK), F32),
        ],
        compiler_params=_params("arbitrary", "arbitrary"),
        name="retention",
    )(lg, pc, pc, pc, pc, pc, kvc3, kvc3)


def _merge_kernel(fm_ref, ret_ref, wf_ref, wr_ref, mf_ref, mr_ref, o_ref, wfb_ref, wrb_ref):
    @pl.when(pl.program_id(1) == 0)
    def _():
        wfb_ref[...] = wf_ref[...].astype(BF16)
        wrb_ref[...] = wr_ref[...].astype(BF16)

    four = _dot(fm_ref[...], wfb_ref[...])
    ret = _dot(ret_ref[...], wrb_ref[...])
    for c in range(MERGE_COL_TILE // LANES):
        cols = slice(c * LANES, (c + 1) * LANES)
        merged = (jax.nn.sigmoid(mf_ref[c].astype(F32)) * four[:, cols]
                  + jax.nn.sigmoid(mr_ref[c].astype(F32)) * ret[:, cols])
        o_ref[:, cols] = merged.astype(o_ref.dtype)


def _merge(fm2, ret2, w_four_out, w_ret_out, pc, rows=512):
    m = fm2.shape[0]
    tile = MERGE_COL_TILE
    mf0, mr0 = MF_OFF // tile, MR_OFF // tile
    chunks = tile // LANES
    return pl.pallas_call(
        _merge_kernel,
        grid=(D_MODEL // tile, m // rows),
        in_specs=[
            pl.BlockSpec((rows, FOUR_WIDTH), lambda j, i: (i, 0)),
            pl.BlockSpec((rows, D_MODEL), lambda j, i: (i, 0)),
            pl.BlockSpec((FOUR_WIDTH, tile), lambda j, i: (0, j)),
            pl.BlockSpec((D_MODEL, tile), lambda j, i: (0, j)),
            pl.BlockSpec((chunks, rows, LANES), lambda j, i: (mf0 + j, i, 0)),
            pl.BlockSpec((chunks, rows, LANES), lambda j, i: (mr0 + j, i, 0)),
        ],
        out_specs=pl.BlockSpec((rows, tile), lambda j, i: (i, j)),
        out_shape=jax.ShapeDtypeStruct((m, D_MODEL), BF16),
        scratch_shapes=[pltpu.VMEM((FOUR_WIDTH, tile), BF16), pltpu.VMEM((D_MODEL, tile), BF16)],
        compiler_params=_params("arbitrary", "arbitrary"),
        name="merge",
    )(fm2, ret2, w_four_out, w_ret_out, pc, pc)


def _out_proj_kernel(a_ref, w_ref, x_ref, g_ref, o_ref, wb_ref):
    @pl.when(pl.program_id(1) == 0)
    def _():
        wb_ref[...] = w_ref[...].astype(BF16)

    o_ref[...] = x_ref[...] + g_ref[0] * _dot(a_ref[...], wb_ref[...])


def _out_proj(merged, w_out, x2, mod3, gate_blk0, rows=1024):
    m = merged.shape[0]
    return pl.pallas_call(
        _out_proj_kernel,
        grid=(D_MODEL // COL_TILE, m // rows),
        in_specs=[
            pl.BlockSpec((rows, D_MODEL), lambda j, i: (i, 0)),
            pl.BlockSpec((D_MODEL, COL_TILE), lambda j, i: (0, j)),
            pl.BlockSpec((rows, COL_TILE), lambda j, i: (i, j)),
            pl.BlockSpec((1, 1, COL_TILE), lambda j, i: (i * rows // SEQ, 0, gate_blk0 + j)),
        ],
        out_specs=pl.BlockSpec((rows, COL_TILE), lambda j, i: (i, j)),
        out_shape=jax.ShapeDtypeStruct((m, D_MODEL), F32),
        scratch_shapes=[pltpu.VMEM((D_MODEL, COL_TILE), BF16)],
        compiler_params=_params("arbitrary", "arbitrary"),
        name="out_proj",
    )(merged, w_out, x2, mod3)


def _route_kernel(x_ref, g_ref, sh_ref, sc_ref, wboth_ref, whi_ref, b_ref, r_ref):
    h = _rms_modulate(x_ref[...], g_ref[...], sh_ref[0], sc_ref[0])
    hi = h.astype(BF16)
    lo = (h - hi.astype(F32)).astype(BF16)
    both = _dot(hi, wboth_ref[...])
    logits = both[:, :LANES] + both[:, LANES:] + _dot(lo, whi_ref[...]) + b_ref[...]

    lane = lax.broadcasted_iota(jnp.int32, logits.shape, 1).astype(F32)
    neg = -jnp.inf
    first = lambda hit: jnp.min(jnp.where(hit, lane, float(LANES)), axis=1, keepdims=True)
    is_grp = lane < float(N_GROUPS)
    gl = jnp.where(is_grp, logits, neg)
    gmax = jnp.max(gl, axis=1, keepdims=True)
    grp = first(gl == gmax)
    g_w = 1.0 / jnp.sum(jnp.where(is_grp, jnp.exp(logits - gmax), 0.0), axis=1, keepdims=True)
    lo_lane = float(N_GROUPS) + grp * float(EXPERTS_PER_GROUP)
    el = jnp.where(lane >= lo_lane, jnp.where(lane < lo_lane + float(EXPERTS_PER_GROUP), logits, neg), neg)
    v1 = jnp.max(el, axis=1, keepdims=True)
    i1 = first(el == v1)
    el2 = jnp.where(lane == i1, neg, el)
    v2 = jnp.max(el2, axis=1, keepdims=True)
    i2 = first(el2 == v2)
    e = jnp.exp(v2 - v1)
    w1 = g_w / (1.0 + e)
    w2 = g_w * e / (1.0 + e)
    out = jnp.where(lane == 0.0, i1 - float(N_GROUPS),
                    jnp.where(lane == 1.0, i2 - float(N_GROUPS),
                              jnp.where(lane == 2.0, w1, jnp.where(lane == 3.0, w2, 0.0))))
    r_ref[...] = out


def _route(x1, g, mod3, shift_blk, scale_blk, w_hi, w_lo, bias, rows=512):
    m = x1.shape[0]
    return pl.pallas_call(
        _route_kernel,
        grid=(m // rows,),
        in_specs=[
            pl.BlockSpec((rows, D_MODEL), lambda i: (i, 0)),
            pl.BlockSpec((1, D_MODEL), lambda i: (0, 0)),
            pl.BlockSpec((1, 1, D_MODEL), lambda i: (i * rows // SEQ, 0, shift_blk)),
            pl.BlockSpec((1, 1, D_MODEL), lambda i: (i * rows // SEQ, 0, scale_blk)),
            pl.BlockSpec((D_MODEL, 2 * LANES), lambda i: (0, 0)),
            pl.BlockSpec((D_MODEL, LANES), lambda i: (0, 0)),
            pl.BlockSpec((1, LANES), lambda i: (0, 0)),
        ],
        out_specs=pl.BlockSpec((rows, LANES), lambda i: (i, 0)),
        out_shape=jax.ShapeDtypeStruct((m, LANES), F32),
        compiler_params=_params("arbitrary"),
        name="route",
    )(x1, g, mod3, mod3, jnp.concatenate([w_hi, w_lo], axis=1), w_hi, bias)


def _dispatch_plan(route):
    rows = EXPERT_ROWS
    flat = route[:, :2].astype(jnp.int32).reshape(-1)
    n_tiles = flat.shape[0] // rows
    experts = jnp.arange(N_EXPERTS, dtype=jnp.int32)
    onehot = (flat[:, None] == experts[None, :]).astype(jnp.int32)
    csum = jnp.cumsum(onehot, axis=0)
    rank = jnp.sum(csum * onehot, axis=1) - 1
    counts = csum[-1]
    cend = jnp.cumsum(counts)
    cstart = cend - counts
    pos = jnp.sum(cstart[None, :] * onehot, axis=1) + rank
    first_tile = cstart // rows
    items = jnp.where(counts > 0, (cend - 1) // rows - first_tile + 1, 0)
    wend = jnp.cumsum(items)
    wstart = wend - items
    n_work = wend[-1]
    w = jnp.arange(n_tiles + N_EXPERTS, dtype=jnp.int32)
    valid = w < n_work
    last_expert = jnp.max(jnp.where(counts > 0, experts, 0))
    e_w = jnp.minimum(jnp.sum((wend[None, :] <= w[:, None]).astype(jnp.int32), axis=1), N_EXPERTS - 1)
    e_w = jnp.where(valid, e_w, last_expert)
    sel = (e_w[:, None] == experts[None, :]).astype(jnp.int32)
    pick = lambda v: jnp.sum(sel * v[None, :], axis=1)
    tile_w = jnp.where(valid, pick(first_tile) + w - pick(wstart), n_tiles - 1)
    lo_w = jnp.where(valid, jnp.clip(pick(cstart) - tile_w * rows, 0, rows), 0)
    hi_w = jnp.where(valid, jnp.clip(pick(cend) - tile_w * rows, 0, rows), 0)
    prev = lambda v: jnp.concatenate([jnp.full((1,), -1, jnp.int32), v[:-1]])
    new_expert = (e_w != prev(e_w)).astype(jnp.int32)
    new_tile = (tile_w != prev(tile_w)).astype(jnp.int32)
    later = jnp.logical_and(counts[None, :] > 0, experts[None, :] > experts[:, None])
    next_of = jnp.min(jnp.where(later, experts[None, :], N_EXPERTS), axis=1)
    next_of = jnp.where(next_of < N_EXPERTS, next_of, -1)
    buf_w = (jnp.cumsum(new_expert) - 1) % 2

    def follow(e):
        hit = (e[:, None] == experts[None, :]).astype(jnp.int32)
        return jnp.where(e >= 0, jnp.sum(hit * next_of[None, :], axis=1), -1)

    nxt1 = follow(e_w)
    nxt2 = follow(nxt1)
    nxt3 = follow(nxt2)
    plan = jnp.stack([tile_w, e_w, new_expert, new_tile, lo_w, hi_w,
                      jnp.broadcast_to(n_work, w.shape), nxt1, buf_w, nxt2, nxt3]).astype(jnp.int32)
    return pos, plan


def _dispatch_kernel(pos_ref, x_ref, g_ref, sh_ref, sc_ref, xs_hbm, hbuf, sem):
    rows = DISPATCH_ROWS
    i = pl.program_id(0)
    n = pl.num_programs(0)
    slot = i % 2

    def row_copy(s, r, dst_row):
        return pltpu.make_async_copy(hbuf.at[s, pl.ds(r, 1), :], xs_hbm.at[pl.ds(dst_row, 1), :], sem.at[s])

    def drain(s):
        for _ in range(2 * rows):
            row_copy(s, 0, 0).wait()

    @pl.when(i >= 2)
    def _():
        drain(slot)

    hbuf[slot] = _rms_modulate(x_ref[...], g_ref[...], sh_ref[0], sc_ref[0])

    for r in range(rows):
        row_copy(slot, r, pos_ref[0, 0, r]).start(priority=0)
        row_copy(slot, r, pos_ref[0, 0, rows + r]).start(priority=1)

    @pl.when(i == n - 1)
    def _():
        drain(slot)
        drain(1 - slot)


def _dispatch(x1, g, mod3, shift_blk, scale_blk, pos3, n_sorted):
    rows = DISPATCH_ROWS
    m = x1.shape[0]
    assert m // rows >= 2
    return pl.pallas_call(
        _dispatch_kernel,
        grid=(m // rows,),
        in_specs=[
            pl.BlockSpec((1, 1, 2 * rows), lambda i: (i, 0, 0), memory_space=pltpu.SMEM),
            pl.BlockSpec((rows, D_MODEL), lambda i: (i, 0)),
            pl.BlockSpec((1, D_MODEL), lambda i: (0, 0)),
            pl.BlockSpec((1, 1, D_MODEL), lambda i: (i * rows // SEQ, 0, shift_blk)),
            pl.BlockSpec((1, 1, D_MODEL), lambda i: (i * rows // SEQ, 0, scale_blk)),
        ],
        out_specs=pl.BlockSpec(memory_space=pl.ANY),
        out_shape=jax.ShapeDtypeStruct((n_sorted, D_MODEL), F32),
        scratch_shapes=[pltpu.VMEM((2, rows, D_MODEL), F32), pltpu.SemaphoreType.DMA((2,))],
        compiler_params=_params("arbitrary"),
        name="dispatch",
    )(pos3, x1, g, mod3, mod3)


def _experts_kernel(plan_ref, x_ref, wg_hbm, wu_hbm, wd_hbm, y_ref,
                    wg_buf, wu_buf, wd_buf, sem, wgb_ref, wub_ref, wdb_ref):
    w = pl.program_id(0)

    def weight_copies(e, s):
        return (pltpu.make_async_copy(wg_hbm.at[e], wg_buf.at[s], sem.at[s, 0]),
                pltpu.make_async_copy(wu_hbm.at[e], wu_buf.at[s], sem.at[s, 1]),
                pltpu.make_async_copy(wd_hbm.at[e], wd_buf.at[s], sem.at[s, 2]))

    def fetch(e, s):
        for cp in weight_copies(e, s):
            cp.start(priority=1)

    def take(e, s):
        for cp in weight_copies(e, s):
            cp.wait()
        wgb_ref[s] = wg_buf[s].astype(BF16)
        wub_ref[s] = wu_buf[s].astype(BF16)
        wdb_ref[s] = wd_buf[s].astype(BF16)

    s = plan_ref[8, w]
    nxt1, nxt2, nxt3 = plan_ref[7, w], plan_ref[9, w], plan_ref[10, w]

    @pl.when(w == 0)
    def _():
        fetch(plan_ref[1, 0], 0)

        @pl.when(nxt1 >= 0)
        def _():
            fetch(nxt1, 1)

        take(plan_ref[1, 0], 0)

        @pl.when(nxt2 >= 0)
        def _():
            fetch(nxt2, 0)

    @pl.when(w < plan_ref[6, w])
    def _():
        @pl.when(jnp.logical_and(plan_ref[2, w] == 1, nxt1 >= 0))
        def _():
            take(nxt1, 1 - s)

            @pl.when(nxt3 >= 0)
            def _():
                fetch(nxt3, 1 - s)

        x = x_ref[...].astype(BF16)
        hg = _dot(x, wgb_ref[s])
        hu = _dot(x, wub_ref[s])
        hid = _silu(hg) * hu
        y = _dot(hid.astype(BF16), wdb_ref[s])
        row = lax.broadcasted_iota(jnp.int32, (EXPERT_ROWS, 1), 0)
        mine = jnp.logical_and(row >= plan_ref[4, w], row < plan_ref[5, w])

        @pl.when(plan_ref[3, w] == 1)
        def _():
            y_ref[...] = jnp.where(mine, y, 0.0)

        @pl.when(plan_ref[3, w] == 0)
        def _():
            y_ref[...] = jnp.where(mine, y, y_ref[...])


def _experts(xs, plan, w_gate, w_up, w_down):
    rows = EXPERT_ROWS
    grid_spec = pltpu.PrefetchScalarGridSpec(
        num_scalar_prefetch=1,
        grid=(plan.shape[1],),
        in_specs=[
            pl.BlockSpec((rows, D_MODEL), lambda w, p: (p[0, w], 0)),
            pl.BlockSpec(memory_space=pl.ANY),
            pl.BlockSpec(memory_space=pl.ANY),
            pl.BlockSpec(memory_space=pl.ANY),
        ],
        out_specs=pl.BlockSpec((rows, D_MODEL), lambda w, p: (p[0, w], 0)),
        scratch_shapes=[
            pltpu.VMEM((2, D_MODEL, EXPERT_FF), F32),
            pltpu.VMEM((2, D_MODEL, EXPERT_FF), F32),
            pltpu.VMEM((2, EXPERT_FF, D_MODEL), F32),
            pltpu.SemaphoreType.DMA((2, 3)),
            pltpu.VMEM((2, D_MODEL, EXPERT_FF), BF16),
            pltpu.VMEM((2, D_MODEL, EXPERT_FF), BF16),
            pltpu.VMEM((2, EXPERT_FF, D_MODEL), BF16),
        ],
    )
    return pl.pallas_call(
        _experts_kernel,
        grid_spec=grid_spec,
        out_shape=jax.ShapeDtypeStruct(xs.shape, F32),
        compiler_params=_params("arbitrary"),
        name="experts",
    )(plan, xs, w_gate, w_up, w_down)


def _final_kernel(pos_ref, posn_ref, x_ref, r_ref, g2_ref, fg_ref, y_hbm, o_ref, ybuf, sem):
    rows = FINAL_ROWS
    i = pl.program_id(0)
    n = pl.num_programs(0)
    slot = i % 2

    def row_copy(src_row, s, r):
        return pltpu.make_async_copy(y_hbm.at[pl.ds(src_row, 1), :], ybuf.at[s, pl.ds(r, 1), :], sem.at[s])

    def issue(idx_ref, s):
        for r in range(rows):
            row_copy(idx_ref[0, 0, r], s, r).start(priority=0)
            row_copy(idx_ref[0, 0, rows + r], s, rows + r).start(priority=1)

    @pl.when(i == 0)
    def _():
        issue(pos_ref, 0)

    @pl.when(i + 1 < n)
    def _():
        issue(posn_ref, 1 - slot)

    for _ in range(2 * rows):
        row_copy(0, slot, 0).wait()

    route = r_ref[...]
    w1 = route[:, 2:3]
    w2 = route[:, 3:4]
    moe = w1 * ybuf[slot, 0:rows, :] + w2 * ybuf[slot, rows:2 * rows, :]
    x = x_ref[...] + g2_ref[0] * moe
    o_ref[...] = x * lax.rsqrt(jnp.mean(x * x, axis=-1, keepdims=True) + EPS) * fg_ref[...]


def _final(x1, route, mod3, gate_blk, final_g, y, pos3):
    rows = FINAL_ROWS
    m = x1.shape[0]
    nt = m // rows
    return pl.pallas_call(
        _final_kernel,
        grid=(nt,),
        in_specs=[
            pl.BlockSpec((1, 1, 2 * rows), lambda i: (i, 0, 0), memory_space=pltpu.SMEM),
            pl.BlockSpec((1, 1, 2 * rows), lambda i: (jnp.minimum(i + 1, nt - 1), 0, 0), memory_space=pltpu.SMEM),
            pl.BlockSpec((rows, D_MODEL), lambda i: (i, 0)),
            pl.BlockSpec((rows, LANES), lambda i: (i, 0)),
            pl.BlockSpec((1, 1, D_MODEL), lambda i: (i * rows // SEQ, 0, gate_blk)),
            pl.BlockSpec((1, D_MODEL), lambda i: (0, 0)),
            pl.BlockSpec(memory_space=pl.ANY),
        ],
        out_specs=pl.BlockSpec((rows, D_MODEL), lambda i: (i, 0)),
        out_shape=jax.ShapeDtypeStruct((m, D_MODEL), F32),
        scratch_shapes=[pltpu.VMEM((2, 2 * rows, D_MODEL), F32), pltpu.SemaphoreType.DMA((2,))],
        compiler_params=_params("arbitrary"),
        name="final",
    )(pos3, pos3, x1, route, mod3, final_g, y)


def _rope_tables():
    rows = SEQ // GRID_W
    row = jnp.repeat(jnp.arange(rows, dtype=F32), GRID_W)
    col = jnp.tile(jnp.arange(GRID_W, dtype=F32), rows)
    n_freq = RET_DK // 4
    inv = ROPE_BASE ** (-jnp.arange(n_freq, dtype=F32) / n_freq)
    ang = jnp.concatenate([row[:, None] * inv, col[:, None] * inv], axis=-1)
    cos, sin = jnp.cos(ang), jnp.sin(ang)
    return jnp.concatenate([cos, cos], axis=-1), jnp.concatenate([-sin, sin], axis=-1)


def kernel(x, c, ctx, c_ctx, w_mod, b_mod, norm1_g, norm2_g, w_in, w_four_out, w_ret_out, w_out,
           ret_decay_f, ret_decay_b, w_group_router, b_group_router, w_expert_router, b_expert_router,
           w_gate, w_up, w_down, final_norm_g):
    b, n, d = x.shape
    assert (n, d) == (SEQ, D_MODEL) and ctx.shape[1] == CTX_LEN and w_mod.shape[0] == 1
    assert b + 1 <= 8
    t = b * n

    c8 = jnp.zeros((8, d), F32).at[:b].set(c).at[b].set(c_ctx)
    mod3 = _modulation(c8, w_mod[0], b_mod).reshape(8, 1, N_MOD * d)

    hx = _prenorm(x, norm1_g, mod3, 0, 1, rows=1024)

    cos2, sin2 = _rope_tables()
    pc = _in_proj(hx.reshape(t, d), w_in[0], cos2, sin2)
    kvc = _ctx_proj(ctx.reshape(b * CTX_LEN, d), norm1_g, mod3, b, w_in[0],
                    K_OFF // COL_TILE, (GF_OFF - K_OFF) // COL_TILE)
    kvc3 = kvc.reshape(b, CTX_LEN, GF_OFF - K_OFF)

    fm = _fourier(pc, b)

    lg_f = jax.nn.log_sigmoid(ret_decay_f[0].astype(F32))
    lg_b = jax.nn.log_sigmoid(ret_decay_b[0].astype(F32))
    lg = jnp.stack([lg_f, lg_b, jnp.exp(RET_BLOCK * lg_f), jnp.exp(RET_BLOCK * lg_b)])
    ret = _retention(pc, kvc3, lg)

    merged = _merge(fm.reshape(t, FOUR_WIDTH), ret.reshape(t, d), w_four_out[0], w_ret_out[0], pc)
    x1 = _out_proj(merged, w_out[0], x.reshape(t, d), mod3, 2 * d // COL_TILE)

    w_router = jnp.concatenate(
        [w_group_router[0], w_expert_router[0].transpose(1, 0, 2).reshape(d, N_EXPERTS)], axis=1).astype(F32)
    w_router = jnp.pad(w_router, ((0, 0), (0, LANES - w_router.shape[1])))
    w_hi = w_router.astype(BF16)
    w_lo = (w_router - w_hi.astype(F32)).astype(BF16)
    bias = jnp.pad(jnp.concatenate([b_group_router[0], b_expert_router[0].reshape(-1)]).astype(F32),
                   (0, LANES - N_GROUPS - N_EXPERTS)).reshape(1, LANES)
    route = _route(x1, norm2_g, mod3, 3, 4, w_hi, w_lo, bias)

    pos, plan = _dispatch_plan(route)

    def per_tile(rows):
        return pos.reshape(t // rows, rows, 2).transpose(0, 2, 1).reshape(t // rows, 1, 2 * rows)

    xs = _dispatch(x1, norm2_g, mod3, 3, 4, per_tile(DISPATCH_ROWS), 2 * t)
    y = _experts(xs, plan, w_gate[0], w_up[0], w_down[0])
    out = _final(x1, route, mod3, 5, final_norm_g.reshape(1, d), y, per_tile(FINAL_ROWS))
    return out.reshape(b, n, d)
```

```python
import jax
import jax.numpy as jnp
import numpy as np
from jax import lax
from jax.experimental import pallas as pl
from jax.experimental.pallas import tpu as pltpu

F32 = jnp.float32
BF16 = jnp.bfloat16

D_MODEL = 2048
SEQ = 2048
CTX_LEN = 256
GRID_W = 64
N_MOD = 6
FOUR_GROUPS = 8
FOUR_GROUP_DIM = 128
FOUR_WIDTH = FOUR_GROUPS * FOUR_GROUP_DIM
RET_HEADS = 8
RET_DK = 128
RET_DV = 256
ROPE_BASE = 10000.0
N_GROUPS = 4
EXPERTS_PER_GROUP = 8
N_EXPERTS = N_GROUPS * EXPERTS_PER_GROUP
EXPERT_FF = 512
EPS = 1e-6
Q_OFF = FOUR_WIDTH
K_OFF = Q_OFF + RET_HEADS * RET_DK
V_OFF = K_OFF + RET_HEADS * RET_DK
GF_OFF = V_OFF + RET_HEADS * RET_DV
GB_OFF = GF_OFF + RET_HEADS * RET_DV
MF_OFF = GB_OFF + RET_HEADS * RET_DV
MR_OFF = MF_OFF + D_MODEL
IN_WIDTH = MR_OFF + D_MODEL

V7X_VMEM_LIMIT_BYTES = 56 * 1024 * 1024
LANES = 128
RET_BLOCK = 256
COL_TILE = 1024
MERGE_COL_TILE = 1024
IN_COL_TILE = 1024
IN_ROWS = 1024
EXPERT_ROWS = 256
DISPATCH_ROWS = 512
FINAL_ROWS = 256


def _params(*sem):
    return pltpu.CompilerParams(dimension_semantics=sem, vmem_limit_bytes=V7X_VMEM_LIMIT_BYTES)


def _dot(a, b):
    return jnp.dot(a, b, preferred_element_type=F32)


def _mod_kernel(c_ref, w_ref, b_ref, o_ref):
    c = c_ref[...]
    sc = c * jax.nn.sigmoid(c)
    o_ref[...] = _dot(sc.astype(BF16), w_ref[...].astype(BF16)) + b_ref[...]


def _modulation(c8, w_mod, b_mod):
    n_out = w_mod.shape[1]
    return pl.pallas_call(
        _mod_kernel,
        grid=(n_out // COL_TILE,),
        in_specs=[
            pl.BlockSpec((8, D_MODEL), lambda j: (0, 0)),
            pl.BlockSpec((D_MODEL, COL_TILE), lambda j: (0, j)),
            pl.BlockSpec((1, COL_TILE), lambda j: (0, j)),
        ],
        out_specs=pl.BlockSpec((8, COL_TILE), lambda j: (0, j)),
        out_shape=jax.ShapeDtypeStruct((8, n_out), F32),
        compiler_params=_params("arbitrary"),
        name="mod",
    )(c8, w_mod, b_mod)


def _rms_modulate(x, g, shift, scale):
    y = x * lax.rsqrt(jnp.mean(x * x, axis=-1, keepdims=True) + EPS) * g
    return y * (1.0 + scale) + shift


def _prenorm_kernel(x_ref, g_ref, sh_ref, sc_ref, o_ref):
    o_ref[0] = _rms_modulate(x_ref[0], g_ref[...], sh_ref[0], sc_ref[0]).astype(o_ref.dtype)


def _prenorm(x, g, mod3, shift_blk, scale_blk, rows):
    b, n, d = x.shape
    return pl.pallas_call(
        _prenorm_kernel,
        grid=(b, n // rows),
        in_specs=[
            pl.BlockSpec((1, rows, d), lambda i, j: (i, j, 0)),
            pl.BlockSpec((1, d), lambda i, j: (0, 0)),
            pl.BlockSpec((1, 1, d), lambda i, j: (i, 0, shift_blk)),
            pl.BlockSpec((1, 1, d), lambda i, j: (i, 0, scale_blk)),
        ],
        out_specs=pl.BlockSpec((1, rows, d), lambda i, j: (i, j, 0)),
        out_shape=jax.ShapeDtypeStruct((b, n, d), BF16),
        compiler_params=_params("arbitrary", "arbitrary"),
        name="prenorm",
    )(x, g, mod3, mod3)


def _ctx_proj_kernel(x_ref, g_ref, sh_ref, sc_ref, w_ref, o_ref, a_ref, wb_ref):
    @pl.when(pl.program_id(0) == 0)
    def _():
        a_ref[...] = _rms_modulate(x_ref[...], g_ref[...], sh_ref[0], sc_ref[0]).astype(BF16)

    wb_ref[...] = w_ref[...].astype(BF16)
    o_ref[...] = _dot(a_ref[...], wb_ref[...]).astype(o_ref.dtype)


def _ctx_proj(ctx2, g, mod3, mod_row, w, col_tile0, n_col_tiles):
    m, k = ctx2.shape
    return pl.pallas_call(
        _ctx_proj_kernel,
        grid=(n_col_tiles,),
        in_specs=[
            pl.BlockSpec((m, k), lambda j: (0, 0)),
            pl.BlockSpec((1, k), lambda j: (0, 0)),
            pl.BlockSpec((1, 1, k), lambda j: (mod_row, 0, 0)),
            pl.BlockSpec((1, 1, k), lambda j: (mod_row, 0, 1)),
            pl.BlockSpec((k, COL_TILE), lambda j: (0, j + col_tile0)),
        ],
        out_specs=pl.BlockSpec((m, COL_TILE), lambda j: (0, j)),
        out_shape=jax.ShapeDtypeStruct((m, n_col_tiles * COL_TILE), BF16),
        scratch_shapes=[pltpu.VMEM((m, k), BF16), pltpu.VMEM((k, COL_TILE), BF16)],
        compiler_params=_params("arbitrary"),
        name="ctx_proj",
    )(ctx2, g, mod3, mod3, w)


def _rope(t, cos2, sin2):
    return t * cos2 + pltpu.roll(t, RET_DK // 2, axis=1) * sin2


def _in_proj_kernel(a_ref, w_ref, cos_ref, sin_ref, o_ref, wb_ref, acc_ref):
    j = pl.program_id(0)

    @pl.when(pl.program_id(1) == 0)
    def _():
        wb_ref[...] = w_ref[...].astype(BF16)

    q_tile, k_tile, v_tile = Q_OFF // IN_COL_TILE, K_OFF // IN_COL_TILE, V_OFF // IN_COL_TILE
    gate_tile, merge_tile = GF_OFF // IN_COL_TILE, MF_OFF // IN_COL_TILE
    is_qk = jnp.logical_and(j >= q_tile, j < v_tile)
    is_gate = jnp.logical_and(j >= gate_tile, j < merge_tile)

    def store_chunks(val):
        for c in range(IN_COL_TILE // LANES):
            o_ref[c] = val[:, c * LANES:(c + 1) * LANES].astype(o_ref.dtype)

    is_four = j < q_tile

    @pl.when(jnp.logical_not(jnp.logical_or(jnp.logical_or(is_qk, is_gate), is_four)))
    def _():
        store_chunks(_dot(a_ref[...], wb_ref[...]))

    @pl.when(is_four)
    def _():
        acc = _dot(a_ref[...], wb_ref[...])
        half = acc_ref.shape[1] // 2
        for c in range(IN_COL_TILE // LANES):
            acc_ref[c] = acc[:, c * LANES:(c + 1) * LANES]
            for parity in (0, 1):
                val = acc_ref[c, pl.ds(parity, half, stride=2), :]
                o_ref[c, parity * half:(parity + 1) * half, :] = val.astype(o_ref.dtype)

    @pl.when(is_qk)
    def _():
        acc = _dot(a_ref[...], wb_ref[...])
        scale = jnp.where(j < k_tile, RET_DK ** -0.5, 1.0).astype(F32)
        cos2, sin2 = cos_ref[...], sin_ref[...]
        assert RET_DK == LANES
        for h in range(IN_COL_TILE // RET_DK):
            cols = slice(h * RET_DK, (h + 1) * RET_DK)
            o_ref[h] = (_rope(acc[:, cols], cos2, sin2) * scale).astype(o_ref.dtype)

    @pl.when(is_gate)
    def _():
        store_chunks(_silu(_dot(a_ref[...], wb_ref[...])))


def _in_proj(a, w, cos2, sin2, rows=IN_ROWS):
    m, k = a.shape
    n_col_tiles = IN_WIDTH // IN_COL_TILE
    pos_blocks = SEQ // rows
    return pl.pallas_call(
        _in_proj_kernel,
        grid=(n_col_tiles, m // rows),
        in_specs=[
            pl.BlockSpec((rows, k), lambda j, i: (i, 0)),
            pl.BlockSpec((k, IN_COL_TILE), lambda j, i: (0, j)),
            pl.BlockSpec((rows, RET_DK), lambda j, i: (i % pos_blocks, 0)),
            pl.BlockSpec((rows, RET_DK), lambda j, i: (i % pos_blocks, 0)),
        ],
        out_specs=pl.BlockSpec((IN_COL_TILE // LANES, rows, LANES), lambda j, i: (j, i, 0)),
        out_shape=jax.ShapeDtypeStruct((IN_WIDTH // LANES, m, LANES), BF16),
        scratch_shapes=[pltpu.VMEM((k, IN_COL_TILE), BF16),
                        pltpu.VMEM((IN_COL_TILE // LANES, rows, LANES), F32)],
        compiler_params=_params("arbitrary", "arbitrary"),
        name="in_proj",
    )(a, w, cos2, sin2)


def _dft_tables():
    half = SEQ // 2
    k = np.arange(half, dtype=np.int64)[:, None]
    n2 = np.arange(half, dtype=np.int64)[None, :]
    tables = []
    for parity in (0, 1):
        ang = 2.0 * np.pi * ((k * (2 * n2 + parity)) % SEQ) / SEQ
        tables.append(np.concatenate([np.cos(ang), -np.sin(ang)], axis=1).astype(np.float32))
    c = np.arange(FOUR_GROUP_DIM, dtype=np.int64)
    ang_c = 2.0 * np.pi * ((c[:, None] * c[None, :]) % FOUR_GROUP_DIM) / FOUR_GROUP_DIM
    norm = 1.0 / np.sqrt(float(SEQ * FOUR_GROUP_DIM))
    chan = (np.concatenate([np.cos(ang_c), np.sin(ang_c)], axis=1) * norm).astype(np.float32)
    return tables[0], tables[1], chan


def _fourier_kernel(u_ref, chan_ref, even_ref, odd_ref, o_ref, y_ref):
    half = SEQ // 2
    tile_half = IN_ROWS // 2
    chan = chan_ref[...]
    for g in range(FOUR_GROUPS):
        cols = slice(g * FOUR_GROUP_DIM, (g + 1) * FOUR_GROUP_DIM)
        y = _dot(u_ref[g], chan).astype(BF16)
        for t in range(SEQ // IN_ROWS):
            for parity in (0, 1):
                src = slice(t * IN_ROWS + parity * tile_half, t * IN_ROWS + (parity + 1) * tile_half)
                dst = 2 * parity * half + t * tile_half
                y_ref[dst:dst + tile_half, cols] = y[src, :FOUR_GROUP_DIM]
                y_ref[half + dst:half + dst + tile_half, cols] = y[src, FOUR_GROUP_DIM:]
    e = _dot(even_ref[...], y_ref[0:SEQ, :])
    o = _dot(odd_ref[...], y_ref[SEQ:2 * SEQ, :])
    o_ref[0, 0:half, :] = (e + o).astype(o_ref.dtype)
    o_ref[0, half:SEQ, :] = (e - o).astype(o_ref.dtype)


def _fourier(pc, b):
    assert FOUR_GROUP_DIM == LANES and FOUR_WIDTH == IN_COL_TILE and SEQ % IN_ROWS == 0
    even_np, odd_np, chan_np = _dft_tables()
    even = jnp.asarray(even_np).astype(BF16)
    odd = jnp.asarray(odd_np).astype(BF16)
    chan = jnp.asarray(chan_np).astype(BF16)
    return pl.pallas_call(
        _fourier_kernel,
        grid=(b,),
        in_specs=[
            pl.BlockSpec((FOUR_GROUPS, SEQ, LANES), lambda i: (0, i, 0)),
            pl.BlockSpec((FOUR_GROUP_DIM, 2 * FOUR_GROUP_DIM), lambda i: (0, 0)),
            pl.BlockSpec((SEQ // 2, SEQ), lambda i: (0, 0)),
            pl.BlockSpec((SEQ // 2, SEQ), lambda i: (0, 0)),
        ],
        out_specs=pl.BlockSpec((1, SEQ, FOUR_WIDTH), lambda i: (i, 0, 0)),
        out_shape=jax.ShapeDtypeStruct((b, SEQ, FOUR_WIDTH), BF16),
        scratch_shapes=[pltpu.VMEM((2 * SEQ, FOUR_WIDTH), BF16)],
        compiler_params=_params("arbitrary"),
        name="fourier",
    )(pc, chan, even, odd)


def _head_norm(o):
    mu = jnp.mean(o, axis=-1, keepdims=True)
    d = o - mu
    var = jnp.mean(d * d, axis=-1, keepdims=True)
    return d * lax.rsqrt(var + EPS)


def _silu(x):
    return x * jax.nn.sigmoid(x)


def _retention_kernel(lg_ref, q_ref, k_ref, v_ref, gf_ref, gb_ref, kc_ref, vc_ref,
                      o_ref, acc_ref, df_ref, db_ref, xif_ref, xib_ref, zf_ref, zb_ref):
    blk = RET_BLOCK
    n_blk = SEQ // blk
    h = pl.program_id(0)
    lgf, lgb, gchf, gchb = lg_ref[0, h], lg_ref[1, h], lg_ref[2, h], lg_ref[3, h]

    @pl.when(pl.program_id(1) == 0)
    def _():
        row = lax.broadcasted_iota(jnp.int32, (blk, blk), 0)
        col = lax.broadcasted_iota(jnp.int32, (blk, blk), 1)
        diff = (row - col).astype(F32)
        df_ref[...] = jnp.where(diff >= 0, jnp.exp(jnp.maximum(diff, 0.0) * lgf), 0.0)
        db_ref[...] = jnp.where(diff <= 0, jnp.exp(jnp.maximum(-diff, 0.0) * lgb), 0.0)
        pos_v = lax.broadcasted_iota(jnp.int32, (blk, RET_DV), 0).astype(F32)
        xif_ref[...] = jnp.exp((pos_v + 1.0) * lgf)
        xib_ref[...] = jnp.exp((blk - pos_v) * lgb)
        pos_k = lax.broadcasted_iota(jnp.int32, (blk, RET_DK), 0).astype(F32)
        zf_ref[...] = jnp.exp((blk - 1.0 - pos_k) * lgf)
        zb_ref[...] = jnp.exp(pos_k * lgb)

    def state_update(r, k, z_ref, v, gch):
        kz = (k.astype(F32) * z_ref[...]).astype(BF16)
        return gch * r + lax.dot_general(kz, v, (((0,), (0,)), ((), ())), preferred_element_type=F32)

    def block_out(qb, kb, vb, r, d_ref, xi_ref):
        s = lax.dot_general(qb, kb, (((1,), (1,)), ((), ())), preferred_element_type=F32)
        inner = _dot((s * d_ref[...]).astype(BF16), vb)
        cross = _dot(qb, r.astype(BF16)) * xi_ref[...]
        return inner + cross

    def wide(ref, rows):
        return jnp.concatenate([ref[0, rows, :], ref[1, rows, :]], axis=1)

    zero = jnp.zeros((RET_DK, RET_DV), F32)
    r_f = state_update(zero, kc_ref[0], zf_ref, vc_ref[0], gchf)
    r_b = state_update(zero, kc_ref[0], zb_ref, vc_ref[0], gchb)

    def emit(i, gate_ref, normed, reached_first):
        rows = slice(i * blk, (i + 1) * blk)
        for c in range(RET_DV // LANES):
            cols = slice(c * LANES, (c + 1) * LANES)
            part = gate_ref[c, rows, :].astype(F32) * normed[:, cols]
            if reached_first:
                acc_ref[rows, cols] = part
            else:
                o_ref[0, rows, cols] = (acc_ref[rows, cols] + part).astype(o_ref.dtype)

    for j in range(n_blk):
        i_f, i_b = j, n_blk - 1 - j
        rows_f = slice(i_f * blk, (i_f + 1) * blk)
        rows_b = slice(i_b * blk, (i_b + 1) * blk)
        qf, kf, vf = q_ref[0, rows_f, :], k_ref[0, rows_f, :], wide(v_ref, rows_f)
        qk, kk, vk = q_ref[0, rows_b, :], k_ref[0, rows_b, :], wide(v_ref, rows_b)
        o_f = block_out(qf, kf, vf, r_f, df_ref, xif_ref)
        o_b = block_out(qk, kk, vk, r_b, db_ref, xib_ref)
        r_f = state_update(r_f, kf, zf_ref, vf, gchf)
        r_b = state_update(r_b, kk, zb_ref, vk, gchb)
        first_visit = j < n_blk - 1 - j
        emit(i_f, gf_ref, _head_norm(o_f), first_visit)
        emit(i_b, gb_ref, _head_norm(o_b), first_visit)


def _retention(pc, kvc3, lg):
    b = kvc3.shape[0]
    assert CTX_LEN == RET_BLOCK and (SEQ // RET_BLOCK) % 2 == 0 and RET_DK == LANES and RET_DV == 2 * LANES
    qb0, kb0 = Q_OFF // RET_DK, K_OFF // RET_DK
    vb0, gfb0, gbb0 = V_OFF // RET_DV, GF_OFF // RET_DV, GB_OFF // RET_DV
    kcv0 = (RET_HEADS * RET_DK) // RET_DV
    return pl.pallas_call(
        _retention_kernel,
        grid=(RET_HEADS, b),
        in_specs=[
            pl.BlockSpec(memory_space=pltpu.SMEM),
            pl.BlockSpec((1, SEQ, LANES), lambda h, i: (qb0 + h, i, 0)),
            pl.BlockSpec((1, SEQ, LANES), lambda h, i: (kb0 + h, i, 0)),
            pl.BlockSpec((RET_DV // LANES, SEQ, LANES), lambda h, i: (vb0 + h, i, 0)),
            pl.BlockSpec((RET_DV // LANES, SEQ, LANES), lambda h, i: (gfb0 + h, i, 0)),
            pl.BlockSpec((RET_DV // LANES, SEQ, LANES), lambda h, i: (gbb0 + h, i, 0)),
            pl.BlockSpec((1, CTX_LEN, RET_DK), lambda h, i: (i, 0, h)),
            pl.BlockSpec((1, CTX_LEN, RET_DV), lambda h, i: (i, 0, kcv0 + h)),
        ],
        out_specs=pl.BlockSpec((1, SEQ, RET_DV), lambda h, i: (i, 0, h)),
        out_shape=jax.ShapeDtypeStruct((b, SEQ, RET_HEADS * RET_DV), BF16),
        scratch_shapes=[
            pltpu.VMEM((SEQ, RET_DV), F32),
            pltpu.VMEM((RET_BLOCK, RET_BLOCK), F32),
            pltpu.VMEM((RET_BLOCK, RET_BLOCK), F32),
            pltpu.VMEM((RET_BLOCK, RET_DV), F32),
            pltpu.VMEM((RET_BLOCK, RET_DV), F32),
            pltpu.VMEM((RET_BLOCK, RET_DK), F32),
            pltpu.VMEM((RET_BLOCK, RET_DK), F32),
        ],
        compiler_params=_params("arbitrary", "arbitrary"),
        name="retention",
    )(lg, pc, pc, pc, pc, pc, kvc3, kvc3)


def _merge_kernel(fm_ref, ret_ref, wf_ref, wr_ref, mf_ref, mr_ref, o_ref, wfb_ref, wrb_ref):
    @pl.when(pl.program_id(1) == 0)
    def _():
        wfb_ref[...] = wf_ref[...].astype(BF16)
        wrb_ref[...] = wr_ref[...].astype(BF16)

    four = _dot(fm_ref[...], wfb_ref[...])
    ret = _dot(ret_ref[...], wrb_ref[...])
    for c in range(MERGE_COL_TILE // LANES):
        cols = slice(c * LANES, (c + 1) * LANES)
        merged = (jax.nn.sigmoid(mf_ref[c].astype(F32)) * four[:, cols]
                  + jax.nn.sigmoid(mr_ref[c].astype(F32)) * ret[:, cols])
        o_ref[:, cols] = merged.astype(o_ref.dtype)


def _merge(fm2, ret2, w_four_out, w_ret_out, pc, rows=512):
    m = fm2.shape[0]
    tile = MERGE_COL_TILE
    mf0, mr0 = MF_OFF // tile, MR_OFF // tile
    chunks = tile // LANES
    return pl.pallas_call(
        _merge_kernel,
        grid=(D_MODEL // tile, m // rows),
        in_specs=[
            pl.BlockSpec((rows, FOUR_WIDTH), lambda j, i: (i, 0)),
            pl.BlockSpec((rows, D_MODEL), lambda j, i: (i, 0)),
            pl.BlockSpec((FOUR_WIDTH, tile), lambda j, i: (0, j)),
            pl.BlockSpec((D_MODEL, tile), lambda j, i: (0, j)),
            pl.BlockSpec((chunks, rows, LANES), lambda j, i: (mf0 + j, i, 0)),
            pl.BlockSpec((chunks, rows, LANES), lambda j, i: (mr0 + j, i, 0)),
        ],
        out_specs=pl.BlockSpec((rows, tile), lambda j, i: (i, j)),
        out_shape=jax.ShapeDtypeStruct((m, D_MODEL), BF16),
        scratch_shapes=[pltpu.VMEM((FOUR_WIDTH, tile), BF16), pltpu.VMEM((D_MODEL, tile), BF16)],
        compiler_params=_params("arbitrary", "arbitrary"),
        name="merge",
    )(fm2, ret2, w_four_out, w_ret_out, pc, pc)


def _out_proj_kernel(a_ref, w_ref, x_ref, g_ref, o_ref, wb_ref):
    @pl.when(pl.program_id(1) == 0)
    def _():
        wb_ref[...] = w_ref[...].astype(BF16)

    o_ref[...] = x_ref[...] + g_ref[0] * _dot(a_ref[...], wb_ref[...])


def _out_proj(merged, w_out, x2, mod3, gate_blk0, rows=1024):
    m = merged.shape[0]
    return pl.pallas_call(
        _out_proj_kernel,
        grid=(D_MODEL // COL_TILE, m // rows),
        in_specs=[
            pl.BlockSpec((rows, D_MODEL), lambda j, i: (i, 0)),
            pl.BlockSpec((D_MODEL, COL_TILE), lambda j, i: (0, j)),
            pl.BlockSpec((rows, COL_TILE), lambda j, i: (i, j)),
            pl.BlockSpec((1, 1, COL_TILE), lambda j, i: (i * rows // SEQ, 0, gate_blk0 + j)),
        ],
        out_specs=pl.BlockSpec((rows, COL_TILE), lambda j, i: (i, j)),
        out_shape=jax.ShapeDtypeStruct((m, D_MODEL), F32),
        scratch_shapes=[pltpu.VMEM((D_MODEL, COL_TILE), BF16)],
        compiler_params=_params("arbitrary", "arbitrary"),
        name="out_proj",
    )(merged, w_out, x2, mod3)


def _route_kernel(x_ref, g_ref, sh_ref, sc_ref, wboth_ref, whi_ref, b_ref, r_ref):
    h = _rms_modulate(x_ref[...], g_ref[...], sh_ref[0], sc_ref[0])
    hi = h.astype(BF16)
    lo = (h - hi.astype(F32)).astype(BF16)
    both = _dot(hi, wboth_ref[...])
    logits = both[:, :LANES] + both[:, LANES:] + _dot(lo, whi_ref[...]) + b_ref[...]

    lane = lax.broadcasted_iota(jnp.int32, logits.shape, 1).astype(F32)
    neg = -jnp.inf
    first = lambda hit: jnp.min(jnp.where(hit, lane, float(LANES)), axis=1, keepdims=True)
    is_grp = lane < float(N_GROUPS)
    gl = jnp.where(is_grp, logits, neg)
    gmax = jnp.max(gl, axis=1, keepdims=True)
    grp = first(gl == gmax)
    g_w = 1.0 / jnp.sum(jnp.where(is_grp, jnp.exp(logits - gmax), 0.0), axis=1, keepdims=True)
    lo_lane = float(N_GROUPS) + grp * float(EXPERTS_PER_GROUP)
    el = jnp.where(lane >= lo_lane, jnp.where(lane < lo_lane + float(EXPERTS_PER_GROUP), logits, neg), neg)
    v1 = jnp.max(el, axis=1, keepdims=True)
    i1 = first(el == v1)
    el2 = jnp.where(lane == i1, neg, el)
    v2 = jnp.max(el2, axis=1, keepdims=True)
    i2 = first(el2 == v2)
    e = jnp.exp(v2 - v1)
    w1 = g_w / (1.0 + e)
    w2 = g_w * e / (1.0 + e)
    out = jnp.where(lane == 0.0, i1 - float(N_GROUPS),
                    jnp.where(lane == 1.0, i2 - float(N_GROUPS),
                              jnp.where(lane == 2.0, w1, jnp.where(lane == 3.0, w2, 0.0))))
    r_ref[...] = out


def _route(x1, g, mod3, shift_blk, scale_blk, w_hi, w_lo, bias, rows=512):
    m = x1.shape[0]
    return pl.pallas_call(
        _route_kernel,
        grid=(m // rows,),
        in_specs=[
            pl.BlockSpec((rows, D_MODEL), lambda i: (i, 0)),
            pl.BlockSpec((1, D_MODEL), lambda i: (0, 0)),
            pl.BlockSpec((1, 1, D_MODEL), lambda i: (i * rows // SEQ, 0, shift_blk)),
            pl.BlockSpec((1, 1, D_MODEL), lambda i: (i * rows // SEQ, 0, scale_blk)),
            pl.BlockSpec((D_MODEL, 2 * LANES), lambda i: (0, 0)),
            pl.BlockSpec((D_MODEL, LANES), lambda i: (0, 0)),
            pl.BlockSpec((1, LANES), lambda i: (0, 0)),
        ],
        out_specs=pl.BlockSpec((rows, LANES), lambda i: (i, 0)),
        out_shape=jax.ShapeDtypeStruct((m, LANES), F32),
        compiler_params=_params("arbitrary"),
        name="route",
    )(x1, g, mod3, mod3, jnp.concatenate([w_hi, w_lo], axis=1), w_hi, bias)


def _dispatch_plan(route):
    rows = EXPERT_ROWS
    flat = route[:, :2].astype(jnp.int32).reshape(-1)
    n_tiles = flat.shape[0] // rows
    experts = jnp.arange(N_EXPERTS, dtype=jnp.int32)
    onehot = (flat[:, None] == experts[None, :]).astype(jnp.int32)
    csum = jnp.cumsum(onehot, axis=0)
    rank = jnp.sum(csum * onehot, axis=1) - 1
    counts = csum[-1]
    cend = jnp.cumsum(counts)
    cstart = cend - counts
    pos = jnp.sum(cstart[None, :] * onehot, axis=1) + rank
    first_tile = cstart // rows
    items = jnp.where(counts > 0, (cend - 1) // rows - first_tile + 1, 0)
    wend = jnp.cumsum(items)
    wstart = wend - items
    n_work = wend[-1]
    w = jnp.arange(n_tiles + N_EXPERTS, dtype=jnp.int32)
    valid = w < n_work
    last_expert = jnp.max(jnp.where(counts > 0, experts, 0))
    e_w = jnp.minimum(jnp.sum((wend[None, :] <= w[:, None]).astype(jnp.int32), axis=1), N_EXPERTS - 1)
    e_w = jnp.where(valid, e_w, last_expert)
    sel = (e_w[:, None] == experts[None, :]).astype(jnp.int32)
    pick = lambda v: jnp.sum(sel * v[None, :], axis=1)
    tile_w = jnp.where(valid, pick(first_tile) + w - pick(wstart), n_tiles - 1)
    lo_w = jnp.where(valid, jnp.clip(pick(cstart) - tile_w * rows, 0, rows), 0)
    hi_w = jnp.where(valid, jnp.clip(pick(cend) - tile_w * rows, 0, rows), 0)
    prev = lambda v: jnp.concatenate([jnp.full((1,), -1, jnp.int32), v[:-1]])
    new_expert = (e_w != prev(e_w)).astype(jnp.int32)
    new_tile = (tile_w != prev(tile_w)).astype(jnp.int32)
    later = jnp.logical_and(counts[None, :] > 0, experts[None, :] > experts[:, None])
    next_of = jnp.min(jnp.where(later, experts[None, :], N_EXPERTS), axis=1)
    next_of = jnp.where(next_of < N_EXPERTS, next_of, -1)
    buf_w = (jnp.cumsum(new_expert) - 1) % 2

    def follow(e):
        hit = (e[:, None] == experts[None, :]).astype(jnp.int32)
        return jnp.where(e >= 0, jnp.sum(hit * next_of[None, :], axis=1), -1)

    nxt1 = follow(e_w)
    nxt2 = follow(nxt1)
    nxt3 = follow(nxt2)
    plan = jnp.stack([tile_w, e_w, new_expert, new_tile, lo_w, hi_w,
                      jnp.broadcast_to(n_work, w.shape), nxt1, buf_w, nxt2, nxt3]).astype(jnp.int32)
    return pos, plan


def _dispatch_kernel(pos_ref, x_ref, g_ref, sh_ref, sc_ref, xs_hbm, hbuf, sem):
    rows = DISPATCH_ROWS
    i = pl.program_id(0)
    n = pl.num_programs(0)
    slot = i % 2

    def row_copy(s, r, dst_row):
        return pltpu.make_async_copy(hbuf.at[s, pl.ds(r, 1), :], xs_hbm.at[pl.ds(dst_row, 1), :], sem.at[s])

    def drain(s):
        for _ in range(2 * rows):
            row_copy(s, 0, 0).wait()

    @pl.when(i >= 2)
    def _():
        drain(slot)

    hbuf[slot] = _rms_modulate(x_ref[...], g_ref[...], sh_ref[0], sc_ref[0])

    for r in range(rows):
        row_copy(slot, r, pos_ref[0, 0, r]).start(priority=0)
        row_copy(slot, r, pos_ref[0, 0, rows + r]).start(priority=1)

    @pl.when(i == n - 1)
    def _():
        drain(slot)
        drain(1 - slot)


def _dispatch(x1, g, mod3, shift_blk, scale_blk, pos3, n_sorted):
    rows = DISPATCH_ROWS
    m = x1.shape[0]
    assert m // rows >= 2
    return pl.pallas_call(
        _dispatch_kernel,
        grid=(m // rows,),
        in_specs=[
            pl.BlockSpec((1, 1, 2 * rows), lambda i: (i, 0, 0), memory_space=pltpu.SMEM),
            pl.BlockSpec((rows, D_MODEL), lambda i: (i, 0)),
            pl.BlockSpec((1, D_MODEL), lambda i: (0, 0)),
            pl.BlockSpec((1, 1, D_MODEL), lambda i: (i * rows // SEQ, 0, shift_blk)),
            pl.BlockSpec((1, 1, D_MODEL), lambda i: (i * rows // SEQ, 0, scale_blk)),
        ],
        out_specs=pl.BlockSpec(memory_space=pl.ANY),
        out_shape=jax.ShapeDtypeStruct((n_sorted, D_MODEL), F32),
        scratch_shapes=[pltpu.VMEM((2, rows, D_MODEL), F32), pltpu.SemaphoreType.DMA((2,))],
        compiler_params=_params("arbitrary"),
        name="dispatch",
    )(pos3, x1, g, mod3, mod3)


def _experts_kernel(plan_ref, x_ref, wg_hbm, wu_hbm, wd_hbm, y_ref,
                    wg_buf, wu_buf, wd_buf, sem, wgb_ref, wub_ref, wdb_ref):
    w = pl.program_id(0)

    def weight_copies(e, s):
        return (pltpu.make_async_copy(wg_hbm.at[e], wg_buf.at[s], sem.at[s, 0]),
                pltpu.make_async_copy(wu_hbm.at[e], wu_buf.at[s], sem.at[s, 1]),
                pltpu.make_async_copy(wd_hbm.at[e], wd_buf.at[s], sem.at[s, 2]))

    def fetch(e, s):
        for cp in weight_copies(e, s):
            cp.start(priority=1)

    def take(e, s):
        for cp in weight_copies(e, s):
            cp.wait()
        wgb_ref[s] = wg_buf[s].astype(BF16)
        wub_ref[s] = wu_buf[s].astype(BF16)
        wdb_ref[s] = wd_buf[s].astype(BF16)

    s = plan_ref[8, w]
    nxt1, nxt2, nxt3 = plan_ref[7, w], plan_ref[9, w], plan_ref[10, w]

    @pl.when(w == 0)
    def _():
        fetch(plan_ref[1, 0], 0)

        @pl.when(nxt1 >= 0)
        def _():
            fetch(nxt1, 1)

        take(plan_ref[1, 0], 0)

        @pl.when(nxt2 >= 0)
        def _():
            fetch(nxt2, 0)

    @pl.when(w < plan_ref[6, w])
    def _():
        @pl.when(jnp.logical_and(plan_ref[2, w] == 1, nxt1 >= 0))
        def _():
            take(nxt1, 1 - s)

            @pl.when(nxt3 >= 0)
            def _():
                fetch(nxt3, 1 - s)

        x = x_ref[...].astype(BF16)
        hg = _dot(x, wgb_ref[s])
        hu = _dot(x, wub_ref[s])
        hid = _silu(hg) * hu
        y = _dot(hid.astype(BF16), wdb_ref[s])
        row = lax.broadcasted_iota(jnp.int32, (EXPERT_ROWS, 1), 0)
        mine = jnp.logical_and(row >= plan_ref[4, w], row < plan_ref[5, w])

        @pl.when(plan_ref[3, w] == 1)
        def _():
            y_ref[...] = jnp.where(mine, y, 0.0)

        @pl.when(plan_ref[3, w] == 0)
        def _():
            y_ref[...] = jnp.where(mine, y, y_ref[...])


def _experts(xs, plan, w_gate, w_up, w_down):
    rows = EXPERT_ROWS
    grid_spec = pltpu.PrefetchScalarGridSpec(
        num_scalar_prefetch=1,
        grid=(plan.shape[1],),
        in_specs=[
            pl.BlockSpec((rows, D_MODEL), lambda w, p: (p[0, w], 0)),
            pl.BlockSpec(memory_space=pl.ANY),
            pl.BlockSpec(memory_space=pl.ANY),
            pl.BlockSpec(memory_space=pl.ANY),
        ],
        out_specs=pl.BlockSpec((rows, D_MODEL), lambda w, p: (p[0, w], 0)),
        scratch_shapes=[
            pltpu.VMEM((2, D_MODEL, EXPERT_FF), F32),
            pltpu.VMEM((2, D_MODEL, EXPERT_FF), F32),
            pltpu.VMEM((2, EXPERT_FF, D_MODEL), F32),
            pltpu.SemaphoreType.DMA((2, 3)),
            pltpu.VMEM((2, D_MODEL, EXPERT_FF), BF16),
            pltpu.VMEM((2, D_MODEL, EXPERT_FF), BF16),
            pltpu.VMEM((2, EXPERT_FF, D_MODEL), BF16),
        ],
    )
    return pl.pallas_call(
        _experts_kernel,
        grid_spec=grid_spec,
        out_shape=jax.ShapeDtypeStruct(xs.shape, F32),
        compiler_params=_params("arbitrary"),
        name="experts",
    )(plan, xs, w_gate, w_up, w_down)


def _final_kernel(pos_ref, posn_ref, x_ref, r_ref, g2_ref, fg_ref, y_hbm, o_ref, ybuf, sem):
    rows = FINAL_ROWS
    i = pl.program_id(0)
    n = pl.num_programs(0)
    slot = i % 2

    def row_copy(src_row, s, r):
        return pltpu.make_async_copy(y_hbm.at[pl.ds(src_row, 1), :], ybuf.at[s, pl.ds(r, 1), :], sem.at[s])

    def issue(idx_ref, s):
        for r in range(rows):
            row_copy(idx_ref[0, 0, r], s, r).start(priority=0)
            row_copy(idx_ref[0, 0, rows + r], s, rows + r).start(priority=1)

    @pl.when(i == 0)
    def _():
        issue(pos_ref, 0)

    @pl.when(i + 1 < n)
    def _():
        issue(posn_ref, 1 - slot)

    for _ in range(2 * rows):
        row_copy(0, slot, 0).wait()

    route = r_ref[...]
    w1 = route[:, 2:3]
    w2 = route[:, 3:4]
    moe = w1 * ybuf[slot, 0:rows, :] + w2 * ybuf[slot, rows:2 * rows, :]
    x = x_ref[...] + g2_ref[0] * moe
    o_ref[...] = x * lax.rsqrt(jnp.mean(x * x, axis=-1, keepdims=True) + EPS) * fg_ref[...]


def _final(x1, route, mod3, gate_blk, final_g, y, pos3):
    rows = FINAL_ROWS
    m = x1.shape[0]
    nt = m // rows
    return pl.pallas_call(
        _final_kernel,
        grid=(nt,),
        in_specs=[
            pl.BlockSpec((1, 1, 2 * rows), lambda i: (i, 0, 0), memory_space=pltpu.SMEM),
            pl.BlockSpec((1, 1, 2 * rows), lambda i: (jnp.minimum(i + 1, nt - 1), 0, 0), memory_space=pltpu.SMEM),
            pl.BlockSpec((rows, D_MODEL), lambda i: (i, 0)),
            pl.BlockSpec((rows, LANES), lambda i: (i, 0)),
            pl.BlockSpec((1, 1, D_MODEL), lambda i: (i * rows // SEQ, 0, gate_blk)),
            pl.BlockSpec((1, D_MODEL), lambda i: (0, 0)),
            pl.BlockSpec(memory_space=pl.ANY),
        ],
        out_specs=pl.BlockSpec((rows, D_MODEL), lambda i: (i, 0)),
        out_shape=jax.ShapeDtypeStruct((m, D_MODEL), F32),
        scratch_shapes=[pltpu.VMEM((2, 2 * rows, D_MODEL), F32), pltpu.SemaphoreType.DMA((2,))],
        compiler_params=_params("arbitrary"),
        name="final",
    )(pos3, pos3, x1, route, mod3, final_g, y)


def _rope_tables():
    rows = SEQ // GRID_W
    row = jnp.repeat(jnp.arange(rows, dtype=F32), GRID_W)
    col = jnp.tile(jnp.arange(GRID_W, dtype=F32), rows)
    n_freq = RET_DK // 4
    inv = ROPE_BASE ** (-jnp.arange(n_freq, dtype=F32) / n_freq)
    ang = jnp.concatenate([row[:, None] * inv, col[:, None] * inv], axis=-1)
    cos, sin = jnp.cos(ang), jnp.sin(ang)
    return jnp.concatenate([cos, cos], axis=-1), jnp.concatenate([-sin, sin], axis=-1)


def kernel(x, c, ctx, c_ctx, w_mod, b_mod, norm1_g, norm2_g, w_in, w_four_out, w_ret_out, w_out,
           ret_decay_f, ret_decay_b, w_group_router, b_group_router, w_expert_router, b_expert_router,
           w_gate, w_up, w_down, final_norm_g):
    b, n, d = x.shape
    assert (n, d) == (SEQ, D_MODEL) and ctx.shape[1] == CTX_LEN and w_mod.shape[0] == 1
    assert b + 1 <= 8
    t = b * n

    c8 = jnp.zeros((8, d), F32).at[:b].set(c).at[b].set(c_ctx)
    mod3 = _modulation(c8, w_mod[0], b_mod).reshape(8, 1, N_MOD * d)

    hx = _prenorm(x, norm1_g, mod3, 0, 1, rows=1024)

    cos2, sin2 = _rope_tables()
    pc = _in_proj(hx.reshape(t, d), w_in[0], cos2, sin2)
    kvc = _ctx_proj(ctx.reshape(b * CTX_LEN, d), norm1_g, mod3, b, w_in[0],
                    K_OFF // COL_TILE, (GF_OFF - K_OFF) // COL_TILE)
    kvc3 = kvc.reshape(b, CTX_LEN, GF_OFF - K_OFF)

    fm = _fourier(pc, b)

    lg_f = jax.nn.log_sigmoid(ret_decay_f[0].astype(F32))
    lg_b = jax.nn.log_sigmoid(ret_decay_b[0].astype(F32))
    lg = jnp.stack([lg_f, lg_b, jnp.exp(RET_BLOCK * lg_f), jnp.exp(RET_BLOCK * lg_b)])
    ret = _retention(pc, kvc3, lg)

    merged = _merge(fm.reshape(t, FOUR_WIDTH), ret.reshape(t, d), w_four_out[0], w_ret_out[0], pc)
    x1 = _out_proj(merged, w_out[0], x.reshape(t, d), mod3, 2 * d // COL_TILE)

    w_router = jnp.concatenate(
        [w_group_router[0], w_expert_router[0].transpose(1, 0, 2).reshape(d, N_EXPERTS)], axis=1).astype(F32)
    w_router = jnp.pad(w_router, ((0, 0), (0, LANES - w_router.shape[1])))
    w_hi = w_router.astype(BF16)
    w_lo = (w_router - w_hi.astype(F32)).astype(BF16)
    bias = jnp.pad(jnp.concatenate([b_group_router[0], b_expert_router[0].reshape(-1)]).astype(F32),
                   (0, LANES - N_GROUPS - N_EXPERTS)).reshape(1, LANES)
    route = _route(x1, norm2_g, mod3, 3, 4, w_hi, w_lo, bias)

    pos, plan = _dispatch_plan(route)

    def per_tile(rows):
        return pos.reshape(t // rows, rows, 2).transpose(0, 2, 1).reshape(t // rows, 1, 2 * rows)

    xs = _dispatch(x1, norm2_g, mod3, 3, 4, per_tile(DISPATCH_ROWS), 2 * t)
    y = _experts(xs, plan, w_gate[0], w_up[0], w_down[0])
    out = _final(x1, route, mod3, 5, final_norm_g.reshape(1, d), y, per_tile(FINAL_ROWS))
    return out.reshape(b, n, d)
```

```python
import jax
import jax.numpy as jnp
import numpy as np
from jax import lax
from jax.experimental import pallas as pl
from jax.experimental.pallas import tpu as pltpu

F32 = jnp.float32
BF16 = jnp.bfloat16

D_MODEL = 2048
SEQ = 2048
CTX_LEN = 256
GRID_W = 64
N_MOD = 6
FOUR_GROUPS = 8
FOUR_GROUP_DIM = 128
FOUR_WIDTH = FOUR_GROUPS * FOUR_GROUP_DIM
RET_HEADS = 8
RET_DK = 128
RET_DV = 256
ROPE_BASE = 10000.0
N_GROUPS = 4
EXPERTS_PER_GROUP = 8
N_EXPERTS = N_GROUPS * EXPERTS_PER_GROUP
EXPERT_FF = 512
EPS = 1e-6
Q_OFF = FOUR_WIDTH
K_OFF = Q_OFF + RET_HEADS * RET_DK
V_OFF = K_OFF + RET_HEADS * RET_DK
GF_OFF = V_OFF + RET_HEADS * RET_DV
GB_OFF = GF_OFF + RET_HEADS * RET_DV
MF_OFF = GB_OFF + RET_HEADS * RET_DV
MR_OFF = MF_OFF + D_MODEL
IN_WIDTH = MR_OFF + D_MODEL

V7X_VMEM_LIMIT_BYTES = 56 * 1024 * 1024
LANES = 128
RET_BLOCK = 256
COL_TILE = 1024
MERGE_COL_TILE = 1024
IN_COL_TILE = 1024
IN_ROWS = 1024
EXPERT_ROWS = 256
DISPATCH_ROWS = 256
FINAL_ROWS = 256


def _params(*sem):
    return pltpu.CompilerParams(dimension_semantics=sem, vmem_limit_bytes=V7X_VMEM_LIMIT_BYTES)


def _dot(a, b):
    return jnp.dot(a, b, preferred_element_type=F32)


def _mod_kernel(c_ref, w_ref, b_ref, o_ref):
    c = c_ref[...]
    sc = c * jax.nn.sigmoid(c)
    o_ref[...] = _dot(sc.astype(BF16), w_ref[...].astype(BF16)) + b_ref[...]


def _modulation(c8, w_mod, b_mod):
    n_out = w_mod.shape[1]
    return pl.pallas_call(
        _mod_kernel,
        grid=(n_out // COL_TILE,),
        in_specs=[
            pl.BlockSpec((8, D_MODEL), lambda j: (0, 0)),
            pl.BlockSpec((D_MODEL, COL_TILE), lambda j: (0, j)),
            pl.BlockSpec((1, COL_TILE), lambda j: (0, j)),
        ],
        out_specs=pl.BlockSpec((8, COL_TILE), lambda j: (0, j)),
        out_shape=jax.ShapeDtypeStruct((8, n_out), F32),
        compiler_params=_params("arbitrary"),
        name="mod",
    )(c8, w_mod, b_mod)


def _rms_modulate(x, g, shift, scale):
    y = x * lax.rsqrt(jnp.mean(x * x, axis=-1, keepdims=True) + EPS) * g
    return y * (1.0 + scale) + shift


def _prenorm_kernel(x_ref, g_ref, sh_ref, sc_ref, o_ref):
    o_ref[0] = _rms_modulate(x_ref[0], g_ref[...], sh_ref[0], sc_ref[0]).astype(o_ref.dtype)


def _prenorm(x, g, mod3, shift_blk, scale_blk, rows):
    b, n, d = x.shape
    return pl.pallas_call(
        _prenorm_kernel,
        grid=(b, n // rows),
        in_specs=[
            pl.BlockSpec((1, rows, d), lambda i, j: (i, j, 0)),
            pl.BlockSpec((1, d), lambda i, j: (0, 0)),
            pl.BlockSpec((1, 1, d), lambda i, j: (i, 0, shift_blk)),
            pl.BlockSpec((1, 1, d), lambda i, j: (i, 0, scale_blk)),
        ],
        out_specs=pl.BlockSpec((1, rows, d), lambda i, j: (i, j, 0)),
        out_shape=jax.ShapeDtypeStruct((b, n, d), BF16),
        compiler_params=_params("arbitrary", "arbitrary"),
        name="prenorm",
    )(x, g, mod3, mod3)


def _ctx_proj_kernel(x_ref, g_ref, sh_ref, sc_ref, w_ref, o_ref, a_ref, wb_ref):
    @pl.when(pl.program_id(0) == 0)
    def _():
        a_ref[...] = _rms_modulate(x_ref[...], g_ref[...], sh_ref[0], sc_ref[0]).astype(BF16)

    wb_ref[...] = w_ref[...].astype(BF16)
    o_ref[...] = _dot(a_ref[...], wb_ref[...]).astype(o_ref.dtype)


def _ctx_proj(ctx2, g, mod3, mod_row, w, col_tile0, n_col_tiles):
    m, k = ctx2.shape
    return pl.pallas_call(
        _ctx_proj_kernel,
        grid=(n_col_tiles,),
        in_specs=[
            pl.BlockSpec((m, k), lambda j: (0, 0)),
            pl.BlockSpec((1, k), lambda j: (0, 0)),
            pl.BlockSpec((1, 1, k), lambda j: (mod_row, 0, 0)),
            pl.BlockSpec((1, 1, k), lambda j: (mod_row, 0, 1)),
            pl.BlockSpec((k, COL_TILE), lambda j: (0, j + col_tile0)),
        ],
        out_specs=pl.BlockSpec((m, COL_TILE), lambda j: (0, j)),
        out_shape=jax.ShapeDtypeStruct((m, n_col_tiles * COL_TILE), BF16),
        scratch_shapes=[pltpu.VMEM((m, k), BF16), pltpu.VMEM((k, COL_TILE), BF16)],
        compiler_params=_params("arbitrary"),
        name="ctx_proj",
    )(ctx2, g, mod3, mod3, w)


def _rope(t, cos2, sin2):
    return t * cos2 + pltpu.roll(t, RET_DK // 2, axis=1) * sin2


def _in_proj_kernel(a_ref, w_ref, cos_ref, sin_ref, o_ref, wb_ref, acc_ref):
    j = pl.program_id(0)

    @pl.when(pl.program_id(1) == 0)
    def _():
        wb_ref[...] = w_ref[...].astype(BF16)

    q_tile, k_tile, v_tile = Q_OFF // IN_COL_TILE, K_OFF // IN_COL_TILE, V_OFF // IN_COL_TILE
    gate_tile, merge_tile = GF_OFF // IN_COL_TILE, MF_OFF // IN_COL_TILE
    is_qk = jnp.logical_and(j >= q_tile, j < v_tile)
    is_gate = jnp.logical_and(j >= gate_tile, j < merge_tile)

    def store_chunks(val):
        for c in range(IN_COL_TILE // LANES):
            o_ref[c] = val[:, c * LANES:(c + 1) * LANES].astype(o_ref.dtype)

    is_four = j < q_tile

    @pl.when(jnp.logical_not(jnp.logical_or(jnp.logical_or(is_qk, is_gate), is_four)))
    def _():
        store_chunks(_dot(a_ref[...], wb_ref[...]))

    @pl.when(is_four)
    def _():
        acc = _dot(a_ref[...], wb_ref[...])
        half = acc_ref.shape[1] // 2
        for c in range(IN_COL_TILE // LANES):
            acc_ref[c] = acc[:, c * LANES:(c + 1) * LANES]
            for parity in (0, 1):
                val = acc_ref[c, pl.ds(parity, half, stride=2), :]
                o_ref[c, parity * half:(parity + 1) * half, :] = val.astype(o_ref.dtype)

    @pl.when(is_qk)
    def _():
        acc = _dot(a_ref[...], wb_ref[...])
        scale = jnp.where(j < k_tile, RET_DK ** -0.5, 1.0).astype(F32)
        cos2, sin2 = cos_ref[...], sin_ref[...]
        assert RET_DK == LANES
        for h in range(IN_COL_TILE // RET_DK):
            cols = slice(h * RET_DK, (h + 1) * RET_DK)
            o_ref[h] = (_rope(acc[:, cols], cos2, sin2) * scale).astype(o_ref.dtype)

    @pl.when(is_gate)
    def _():
        store_chunks(_silu(_dot(a_ref[...], wb_ref[...])))


def _in_proj(a, w, cos2, sin2, rows=IN_ROWS):
    m, k = a.shape
    n_col_tiles = IN_WIDTH // IN_COL_TILE
    pos_blocks = SEQ // rows
    return pl.pallas_call(
        _in_proj_kernel,
        grid=(n_col_tiles, m // rows),
        in_specs=[
            pl.BlockSpec((rows, k), lambda j, i: (i, 0)),
            pl.BlockSpec((k, IN_COL_TILE), lambda j, i: (0, j)),
            pl.BlockSpec((rows, RET_DK), lambda j, i: (i % pos_blocks, 0)),
            pl.BlockSpec((rows, RET_DK), lambda j, i: (i % pos_blocks, 0)),
        ],
        out_specs=pl.BlockSpec((IN_COL_TILE // LANES, rows, LANES), lambda j, i: (j, i, 0)),
        out_shape=jax.ShapeDtypeStruct((IN_WIDTH // LANES, m, LANES), BF16),
        scratch_shapes=[pltpu.VMEM((k, IN_COL_TILE), BF16),
                        pltpu.VMEM((IN_COL_TILE // LANES, rows, LANES), F32)],
        compiler_params=_params("arbitrary", "arbitrary"),
        name="in_proj",
    )(a, w, cos2, sin2)


def _dft_tables():
    half = SEQ // 2
    k = np.arange(half, dtype=np.int64)[:, None]
    n2 = np.arange(half, dtype=np.int64)[None, :]
    tables = []
    for parity in (0, 1):
        ang = 2.0 * np.pi * ((k * (2 * n2 + parity)) % SEQ) / SEQ
        tables.append(np.concatenate([np.cos(ang), -np.sin(ang)], axis=1).astype(np.float32))
    c = np.arange(FOUR_GROUP_DIM, dtype=np.int64)
    ang_c = 2.0 * np.pi * ((c[:, None] * c[None, :]) % FOUR_GROUP_DIM) / FOUR_GROUP_DIM
    norm = 1.0 / np.sqrt(float(SEQ * FOUR_GROUP_DIM))
    chan = (np.concatenate([np.cos(ang_c), np.sin(ang_c)], axis=1) * norm).astype(np.float32)
    return tables[0], tables[1], chan


def _fourier_kernel(u_ref, chan_ref, even_ref, odd_ref, o_ref, y_ref):
    half = SEQ // 2
    tile_half = IN_ROWS // 2
    chan = chan_ref[...]
    for g in range(FOUR_GROUPS):
        cols = slice(g * FOUR_GROUP_DIM, (g + 1) * FOUR_GROUP_DIM)
        y = _dot(u_ref[g], chan).astype(BF16)
        for t in range(SEQ // IN_ROWS):
            for parity in (0, 1):
                src = slice(t * IN_ROWS + parity * tile_half, t * IN_ROWS + (parity + 1) * tile_half)
                dst = 2 * parity * half + t * tile_half
                y_ref[dst:dst + tile_half, cols] = y[src, :FOUR_GROUP_DIM]
                y_ref[half + dst:half + dst + tile_half, cols] = y[src, FOUR_GROUP_DIM:]
    e = _dot(even_ref[...], y_ref[0:SEQ, :])
    o = _dot(odd_ref[...], y_ref[SEQ:2 * SEQ, :])
    o_ref[0, 0:half, :] = (e + o).astype(o_ref.dtype)
    o_ref[0, half:SEQ, :] = (e - o).astype(o_ref.dtype)


def _fourier(pc, b):
    assert FOUR_GROUP_DIM == LANES and FOUR_WIDTH == IN_COL_TILE and SEQ % IN_ROWS == 0
    even_np, odd_np, chan_np = _dft_tables()
    even = jnp.asarray(even_np).astype(BF16)
    odd = jnp.asarray(odd_np).astype(BF16)
    chan = jnp.asarray(chan_np).astype(BF16)
    return pl.pallas_call(
        _fourier_kernel,
        grid=(b,),
        in_specs=[
            pl.BlockSpec((FOUR_GROUPS, SEQ, LANES), lambda i: (0, i, 0)),
            pl.BlockSpec((FOUR_GROUP_DIM, 2 * FOUR_GROUP_DIM), lambda i: (0, 0)),
            pl.BlockSpec((SEQ // 2, SEQ), lambda i: (0, 0)),
            pl.BlockSpec((SEQ // 2, SEQ), lambda i: (0, 0)),
        ],
        out_specs=pl.BlockSpec((1, SEQ, FOUR_WIDTH), lambda i: (i, 0, 0)),
        out_shape=jax.ShapeDtypeStruct((b, SEQ, FOUR_WIDTH), BF16),
        scratch_shapes=[pltpu.VMEM((2 * SEQ, FOUR_WIDTH), BF16)],
        compiler_params=_params("arbitrary"),
        name="fourier",
    )(pc, chan, even, odd)


def _head_norm(o):
    mu = jnp.mean(o, axis=-1, keepdims=True)
    d = o - mu
    var = jnp.mean(d * d, axis=-1, keepdims=True)
    return d * lax.rsqrt(var + EPS)


def _silu(x):
    return x * jax.nn.sigmoid(x)


def _retention_kernel(lg_ref, q_ref, k_ref, v_ref, gf_ref, gb_ref, kc_ref, vc_ref,
                      o_ref, acc_ref, df_ref, db_ref, xif_ref, xib_ref, zf_ref, zb_ref):
    blk = RET_BLOCK
    n_blk = SEQ // blk
    h = pl.program_id(0)
    lgf, lgb, gchf, gchb = lg_ref[0, h], lg_ref[1, h], lg_ref[2, h], lg_ref[3, h]

    @pl.when(pl.program_id(1) == 0)
    def _():
        row = lax.broadcasted_iota(jnp.int32, (blk, blk), 0)
        col = lax.broadcasted_iota(jnp.int32, (blk, blk), 1)
        diff = (row - col).astype(F32)
        df_ref[...] = jnp.where(diff >= 0, jnp.exp(jnp.maximum(diff, 0.0) * lgf), 0.0)
        db_ref[...] = jnp.where(diff <= 0, jnp.exp(jnp.maximum(-diff, 0.0) * lgb), 0.0)
        pos_v = lax.broadcasted_iota(jnp.int32, (blk, RET_DV), 0).astype(F32)
        xif_ref[...] = jnp.exp((pos_v + 1.0) * lgf)
        xib_ref[...] = jnp.exp((blk - pos_v) * lgb)
        pos_k = lax.broadcasted_iota(jnp.int32, (blk, RET_DK), 0).astype(F32)
        zf_ref[...] = jnp.exp((blk - 1.0 - pos_k) * lgf)
        zb_ref[...] = jnp.exp(pos_k * lgb)

    def state_update(r, k, z_ref, v, gch):
        kz = (k.astype(F32) * z_ref[...]).astype(BF16)
        return gch * r + lax.dot_general(kz, v, (((0,), (0,)), ((), ())), preferred_element_type=F32)

    def block_out(qb, kb, vb, r, d_ref, xi_ref):
        s = lax.dot_general(qb, kb, (((1,), (1,)), ((), ())), preferred_element_type=F32)
        inner = _dot((s * d_ref[...]).astype(BF16), vb)
        cross = _dot(qb, r.astype(BF16)) * xi_ref[...]
        return inner + cross

    def wide(ref, rows):
        return jnp.concatenate([ref[0, rows, :], ref[1, rows, :]], axis=1)

    zero = jnp.zeros((RET_DK, RET_DV), F32)
    r_f = state_update(zero, kc_ref[0], zf_ref, vc_ref[0], gchf)
    r_b = state_update(zero, kc_ref[0], zb_ref, vc_ref[0], gchb)

    def emit(i, gate_ref, normed, reached_first):
        rows = slice(i * blk, (i + 1) * blk)
        for c in range(RET_DV // LANES):
            cols = slice(c * LANES, (c + 1) * LANES)
            part = gate_ref[c, rows, :].astype(F32) * normed[:, cols]
            if reached_first:
                acc_ref[rows, cols] = part
            else:
                o_ref[0, rows, cols] = (acc_ref[rows, cols] + part).astype(o_ref.dtype)

    for j in range(n_blk):
        i_f, i_b = j, n_blk - 1 - j
        rows_f = slice(i_f * blk, (i_f + 1) * blk)
        rows_b = slice(i_b * blk, (i_b + 1) * blk)
        qf, kf, vf = q_ref[0, rows_f, :], k_ref[0, rows_f, :], wide(v_ref, rows_f)
        qk, kk, vk = q_ref[0, rows_b, :], k_ref[0, rows_b, :], wide(v_ref, rows_b)
        o_f = block_out(qf, kf, vf, r_f, df_ref, xif_ref)
        o_b = block_out(qk, kk, vk, r_b, db_ref, xib_ref)
        r_f = state_update(r_f, kf, zf_ref, vf, gchf)
        r_b = state_update(r_b, kk, zb_ref, vk, gchb)
        first_visit = j < n_blk - 1 - j
        emit(i_f, gf_ref, _head_norm(o_f), first_visit)
        emit(i_b, gb_ref, _head_norm(o_b), first_visit)


def _retention(pc, kvc3, lg):
    b = kvc3.shape[0]
    assert CTX_LEN == RET_BLOCK and (SEQ // RET_BLOCK) % 2 == 0 and RET_DK == LANES and RET_DV == 2 * LANES
    qb0, kb0 = Q_OFF // RET_DK, K_OFF // RET_DK
    vb0, gfb0, gbb0 = V_OFF // RET_DV, GF_OFF // RET_DV, GB_OFF // RET_DV
    kcv0 = (RET_HEADS * RET_DK) // RET_DV
    return pl.pallas_call(
        _retention_kernel,
        grid=(RET_HEADS, b),
        in_specs=[
            pl.BlockSpec(memory_space=pltpu.SMEM),
            pl.BlockSpec((1, SEQ, LANES), lambda h, i: (qb0 + h, i, 0)),
            pl.BlockSpec((1, SEQ, LANES), lambda h, i: (kb0 + h, i, 0)),
            pl.BlockSpec((RET_DV // LANES, SEQ, LANES), lambda h, i: (vb0 + h, i, 0)),
            pl.BlockSpec((RET_DV // LANES, SEQ, LANES), lambda h, i: (gfb0 + h, i, 0)),
            pl.BlockSpec((RET_DV // LANES, SEQ, LANES), lambda h, i: (gbb0 + h, i, 0)),
            pl.BlockSpec((1, CTX_LEN, RET_DK), lambda h, i: (i, 0, h)),
            pl.BlockSpec((1, CTX_LEN, RET_DV), lambda h, i: (i, 0, kcv0 + h)),
        ],
        out_specs=pl.BlockSpec((1, SEQ, RET_DV), lambda h, i: (i, 0, h)),
        out_shape=jax.ShapeDtypeStruct((b, SEQ, RET_HEADS * RET_DV), BF16),
        scratch_shapes=[
            pltpu.VMEM((SEQ, RET_DV), F32),
            pltpu.VMEM((RET_BLOCK, RET_BLOCK), F32),
            pltpu.VMEM((RET_BLOCK, RET_BLOCK), F32),
            pltpu.VMEM((RET_BLOCK, RET_DV), F32),
            pltpu.VMEM((RET_BLOCK, RET_DV), F32),
            pltpu.VMEM((RET_BLOCK, RET_DK), F32),
            pltpu.VMEM((RET_BLOCK, RET_DK), F32),
        ],
        compiler_params=_params("arbitrary", "arbitrary"),
        name="retention",
    )(lg, pc, pc, pc, pc, pc, kvc3, kvc3)


def _merge_kernel(fm_ref, ret_ref, wf_ref, wr_ref, mf_ref, mr_ref, o_ref, wfb_ref, wrb_ref):
    @pl.when(pl.program_id(1) == 0)
    def _():
        wfb_ref[...] = wf_ref[...].astype(BF16)
        wrb_ref[...] = wr_ref[...].astype(BF16)

    four = _dot(fm_ref[...], wfb_ref[...])
    ret = _dot(ret_ref[...], wrb_ref[...])
    for c in range(MERGE_COL_TILE // LANES):
        cols = slice(c * LANES, (c + 1) * LANES)
        merged = (jax.nn.sigmoid(mf_ref[c].astype(F32)) * four[:, cols]
                  + jax.nn.sigmoid(mr_ref[c].astype(F32)) * ret[:, cols])
        o_ref[:, cols] = merged.astype(o_ref.dtype)


def _merge(fm2, ret2, w_four_out, w_ret_out, pc, rows=512):
    m = fm2.shape[0]
    tile = MERGE_COL_TILE
    mf0, mr0 = MF_OFF // tile, MR_OFF // tile
    chunks = tile // LANES
    return pl.pallas_call(
        _merge_kernel,
        grid=(D_MODEL // tile, m // rows),
        in_specs=[
            pl.BlockSpec((rows, FOUR_WIDTH), lambda j, i: (i, 0)),
            pl.BlockSpec((rows, D_MODEL), lambda j, i: (i, 0)),
            pl.BlockSpec((FOUR_WIDTH, tile), lambda j, i: (0, j)),
            pl.BlockSpec((D_MODEL, tile), lambda j, i: (0, j)),
            pl.BlockSpec((chunks, rows, LANES), lambda j, i: (mf0 + j, i, 0)),
            pl.BlockSpec((chunks, rows, LANES), lambda j, i: (mr0 + j, i, 0)),
        ],
        out_specs=pl.BlockSpec((rows, tile), lambda j, i: (i, j)),
        out_shape=jax.ShapeDtypeStruct((m, D_MODEL), BF16),
        scratch_shapes=[pltpu.VMEM((FOUR_WIDTH, tile), BF16), pltpu.VMEM((D_MODEL, tile), BF16)],
        compiler_params=_params("arbitrary", "arbitrary"),
        name="merge",
    )(fm2, ret2, w_four_out, w_ret_out, pc, pc)


def _out_proj_kernel(a_ref, w_ref, x_ref, g_ref, o_ref, wb_ref):
    @pl.when(pl.program_id(1) == 0)
    def _():
        wb_ref[...] = w_ref[...].astype(BF16)

    o_ref[...] = x_ref[...] + g_ref[0] * _dot(a_ref[...], wb_ref[...])


def _out_proj(merged, w_out, x2, mod3, gate_blk0, rows=1024):
    m = merged.shape[0]
    return pl.pallas_call(
        _out_proj_kernel,
        grid=(D_MODEL // COL_TILE, m // rows),
        in_specs=[
            pl.BlockSpec((rows, D_MODEL), lambda j, i: (i, 0)),
            pl.BlockSpec((D_MODEL, COL_TILE), lambda j, i: (0, j)),
            pl.BlockSpec((rows, COL_TILE), lambda j, i: (i, j)),
            pl.BlockSpec((1, 1, COL_TILE), lambda j, i: (i * rows // SEQ, 0, gate_blk0 + j)),
        ],
        out_specs=pl.BlockSpec((rows, COL_TILE), lambda j, i: (i, j)),
        out_shape=jax.ShapeDtypeStruct((m, D_MODEL), F32),
        scratch_shapes=[pltpu.VMEM((D_MODEL, COL_TILE), BF16)],
        compiler_params=_params("arbitrary", "arbitrary"),
        name="out_proj",
    )(merged, w_out, x2, mod3)


def _route_kernel(x_ref, g_ref, sh_ref, sc_ref, wboth_ref, whi_ref, b_ref, r_ref):
    h = _rms_modulate(x_ref[...], g_ref[...], sh_ref[0], sc_ref[0])
    hi = h.astype(BF16)
    lo = (h - hi.astype(F32)).astype(BF16)
    both = _dot(hi, wboth_ref[...])
    logits = both[:, :LANES] + both[:, LANES:] + _dot(lo, whi_ref[...]) + b_ref[...]

    lane = lax.broadcasted_iota(jnp.int32, logits.shape, 1).astype(F32)
    neg = -jnp.inf
    first = lambda hit: jnp.min(jnp.where(hit, lane, float(LANES)), axis=1, keepdims=True)
    is_grp = lane < float(N_GROUPS)
    gl = jnp.where(is_grp, logits, neg)
    gmax = jnp.max(gl, axis=1, keepdims=True)
    grp = first(gl == gmax)
    g_w = 1.0 / jnp.sum(jnp.where(is_grp, jnp.exp(logits - gmax), 0.0), axis=1, keepdims=True)
    lo_lane = float(N_GROUPS) + grp * float(EXPERTS_PER_GROUP)
    el = jnp.where(lane >= lo_lane, jnp.where(lane < lo_lane + float(EXPERTS_PER_GROUP), logits, neg), neg)
    v1 = jnp.max(el, axis=1, keepdims=True)
    i1 = first(el == v1)
    el2 = jnp.where(lane == i1, neg, el)
    v2 = jnp.max(el2, axis=1, keepdims=True)
    i2 = first(el2 == v2)
    e = jnp.exp(v2 - v1)
    w1 = g_w / (1.0 + e)
    w2 = g_w * e / (1.0 + e)
    out = jnp.where(lane == 0.0, i1 - float(N_GROUPS),
                    jnp.where(lane == 1.0, i2 - float(N_GROUPS),
                              jnp.where(lane == 2.0, w1, jnp.where(lane == 3.0, w2, 0.0))))
    r_ref[...] = out


def _route(x1, g, mod3, shift_blk, scale_blk, w_hi, w_lo, bias, rows=512):
    m = x1.shape[0]
    return pl.pallas_call(
        _route_kernel,
        grid=(m // rows,),
        in_specs=[
            pl.BlockSpec((rows, D_MODEL), lambda i: (i, 0)),
            pl.BlockSpec((1, D_MODEL), lambda i: (0, 0)),
            pl.BlockSpec((1, 1, D_MODEL), lambda i: (i * rows // SEQ, 0, shift_blk)),
            pl.BlockSpec((1, 1, D_MODEL), lambda i: (i * rows // SEQ, 0, scale_blk)),
            pl.BlockSpec((D_MODEL, 2 * LANES), lambda i: (0, 0)),
            pl.BlockSpec((D_MODEL, LANES), lambda i: (0, 0)),
            pl.BlockSpec((1, LANES), lambda i: (0, 0)),
        ],
        out_specs=pl.BlockSpec((rows, LANES), lambda i: (i, 0)),
        out_shape=jax.ShapeDtypeStruct((m, LANES), F32),
        compiler_params=_params("arbitrary"),
        name="route",
    )(x1, g, mod3, mod3, jnp.concatenate([w_hi, w_lo], axis=1), w_hi, bias)


def _dispatch_plan(route):
    rows = EXPERT_ROWS
    flat = route[:, :2].astype(jnp.int32).reshape(-1)
    n_tiles = flat.shape[0] // rows
    experts = jnp.arange(N_EXPERTS, dtype=jnp.int32)
    onehot = (flat[:, None] == experts[None, :]).astype(jnp.int32)
    csum = jnp.cumsum(onehot, axis=0)
    rank = jnp.sum(csum * onehot, axis=1) - 1
    counts = csum[-1]
    cend = jnp.cumsum(counts)
    cstart = cend - counts
    pos = jnp.sum(cstart[None, :] * onehot, axis=1) + rank
    first_tile = cstart // rows
    items = jnp.where(counts > 0, (cend - 1) // rows - first_tile + 1, 0)
    wend = jnp.cumsum(items)
    wstart = wend - items
    n_work = wend[-1]
    w = jnp.arange(n_tiles + N_EXPERTS, dtype=jnp.int32)
    valid = w < n_work
    last_expert = jnp.max(jnp.where(counts > 0, experts, 0))
    e_w = jnp.minimum(jnp.sum((wend[None, :] <= w[:, None]).astype(jnp.int32), axis=1), N_EXPERTS - 1)
    e_w = jnp.where(valid, e_w, last_expert)
    sel = (e_w[:, None] == experts[None, :]).astype(jnp.int32)
    pick = lambda v: jnp.sum(sel * v[None, :], axis=1)
    tile_w = jnp.where(valid, pick(first_tile) + w - pick(wstart), n_tiles - 1)
    lo_w = jnp.where(valid, jnp.clip(pick(cstart) - tile_w * rows, 0, rows), 0)
    hi_w = jnp.where(valid, jnp.clip(pick(cend) - tile_w * rows, 0, rows), 0)
    prev = lambda v: jnp.concatenate([jnp.full((1,), -1, jnp.int32), v[:-1]])
    new_expert = (e_w != prev(e_w)).astype(jnp.int32)
    new_tile = (tile_w != prev(tile_w)).astype(jnp.int32)
    later = jnp.logical_and(counts[None, :] > 0, experts[None, :] > experts[:, None])
    next_of = jnp.min(jnp.where(later, experts[None, :], N_EXPERTS), axis=1)
    next_of = jnp.where(next_of < N_EXPERTS, next_of, -1)
    buf_w = (jnp.cumsum(new_expert) - 1) % 2

    def follow(e):
        hit = (e[:, None] == experts[None, :]).astype(jnp.int32)
        return jnp.where(e >= 0, jnp.sum(hit * next_of[None, :], axis=1), -1)

    nxt1 = follow(e_w)
    nxt2 = follow(nxt1)
    nxt3 = follow(nxt2)
    plan = jnp.stack([tile_w, e_w, new_expert, new_tile, lo_w, hi_w,
                      jnp.broadcast_to(n_work, w.shape), nxt1, buf_w, nxt2, nxt3]).astype(jnp.int32)
    return pos, plan


def _dispatch_kernel(pos_ref, x_ref, g_ref, sh_ref, sc_ref, xs_hbm, hbuf, sem):
    rows = DISPATCH_ROWS
    i = pl.program_id(0)
    n = pl.num_programs(0)
    slot = i % 2

    def row_copy(s, r, dst_row):
        return pltpu.make_async_copy(hbuf.at[s, pl.ds(r, 1), :], xs_hbm.at[pl.ds(dst_row, 1), :], sem.at[s])

    def drain(s):
        for _ in range(2 * rows):
            row_copy(s, 0, 0).wait()

    @pl.when(i >= 2)
    def _():
        drain(slot)

    hbuf[slot] = _rms_modulate(x_ref[...], g_ref[...], sh_ref[0], sc_ref[0])

    for r in range(rows):
        row_copy(slot, r, pos_ref[0, 0, r]).start(priority=0)
        row_copy(slot, r, pos_ref[0, 0, rows + r]).start(priority=1)

    @pl.when(i == n - 1)
    def _():
        drain(slot)
        drain(1 - slot)


def _dispatch(x1, g, mod3, shift_blk, scale_blk, pos3, n_sorted):
    rows = DISPATCH_ROWS
    m = x1.shape[0]
    assert m // rows >= 2
    return pl.pallas_call(
        _dispatch_kernel,
        grid=(m // rows,),
        in_specs=[
            pl.BlockSpec((1, 1, 2 * rows), lambda i: (i, 0, 0), memory_space=pltpu.SMEM),
            pl.BlockSpec((rows, D_MODEL), lambda i: (i, 0)),
            pl.BlockSpec((1, D_MODEL), lambda i: (0, 0)),
            pl.BlockSpec((1, 1, D_MODEL), lambda i: (i * rows // SEQ, 0, shift_blk)),
            pl.BlockSpec((1, 1, D_MODEL), lambda i: (i * rows // SEQ, 0, scale_blk)),
        ],
        out_specs=pl.BlockSpec(memory_space=pl.ANY),
        out_shape=jax.ShapeDtypeStruct((n_sorted, D_MODEL), F32),
        scratch_shapes=[pltpu.VMEM((2, rows, D_MODEL), F32), pltpu.SemaphoreType.DMA((2,))],
        compiler_params=_params("arbitrary"),
        name="dispatch",
    )(pos3, x1, g, mod3, mod3)


def _experts_kernel(plan_ref, x_ref, wg_hbm, wu_hbm, wd_hbm, y_ref,
                    wg_buf, wu_buf, wd_buf, sem, wgb_ref, wub_ref, wdb_ref):
    w = pl.program_id(0)

    def weight_copies(e, s):
        return (pltpu.make_async_copy(wg_hbm.at[e], wg_buf.at[s], sem.at[s, 0]),
                pltpu.make_async_copy(wu_hbm.at[e], wu_buf.at[s], sem.at[s, 1]),
                pltpu.make_async_copy(wd_hbm.at[e], wd_buf.at[s], sem.at[s, 2]))

    def fetch(e, s):
        for cp in weight_copies(e, s):
            cp.start(priority=1)

    def take(e, s):
        for cp in weight_copies(e, s):
            cp.wait()
        wgb_ref[s] = wg_buf[s].astype(BF16)
        wub_ref[s] = wu_buf[s].astype(BF16)
        wdb_ref[s] = wd_buf[s].astype(BF16)

    s = plan_ref[8, w]
    nxt1, nxt2, nxt3 = plan_ref[7, w], plan_ref[9, w], plan_ref[10, w]

    @pl.when(w == 0)
    def _():
        fetch(plan_ref[1, 0], 0)

        @pl.when(nxt1 >= 0)
        def _():
            fetch(nxt1, 1)

        take(plan_ref[1, 0], 0)

        @pl.when(nxt2 >= 0)
        def _():
            fetch(nxt2, 0)

    @pl.when(w < plan_ref[6, w])
    def _():
        @pl.when(jnp.logical_and(plan_ref[2, w] == 1, nxt1 >= 0))
        def _():
            take(nxt1, 1 - s)

            @pl.when(nxt3 >= 0)
            def _():
                fetch(nxt3, 1 - s)

        x = x_ref[...].astype(BF16)
        hg = _dot(x, wgb_ref[s])
        hu = _dot(x, wub_ref[s])
        hid = _silu(hg) * hu
        y = _dot(hid.astype(BF16), wdb_ref[s])
        row = lax.broadcasted_iota(jnp.int32, (EXPERT_ROWS, 1), 0)
        mine = jnp.logical_and(row >= plan_ref[4, w], row < plan_ref[5, w])

        @pl.when(plan_ref[3, w] == 1)
        def _():
            y_ref[...] = jnp.where(mine, y, 0.0)

        @pl.when(plan_ref[3, w] == 0)
        def _():
            y_ref[...] = jnp.where(mine, y, y_ref[...])


def _experts(xs, plan, w_gate, w_up, w_down):
    rows = EXPERT_ROWS
    grid_spec = pltpu.PrefetchScalarGridSpec(
        num_scalar_prefetch=1,
        grid=(plan.shape[1],),
        in_specs=[
            pl.BlockSpec((rows, D_MODEL), lambda w, p: (p[0, w], 0)),
            pl.BlockSpec(memory_space=pl.ANY),
            pl.BlockSpec(memory_space=pl.ANY),
            pl.BlockSpec(memory_space=pl.ANY),
        ],
        out_specs=pl.BlockSpec((rows, D_MODEL), lambda w, p: (p[0, w], 0)),
        scratch_shapes=[
            pltpu.VMEM((2, D_MODEL, EXPERT_FF), F32),
            pltpu.VMEM((2, D_MODEL, EXPERT_FF), F32),
            pltpu.VMEM((2, EXPERT_FF, D_MODEL), F32),
            pltpu.SemaphoreType.DMA((2, 3)),
            pltpu.VMEM((2, D_MODEL, EXPERT_FF), BF16),
            pltpu.VMEM((2, D_MODEL, EXPERT_FF), BF16),
            pltpu.VMEM((2, EXPERT_FF, D_MODEL), BF16),
        ],
    )
    return pl.pallas_call(
        _experts_kernel,
        grid_spec=grid_spec,
        out_shape=jax.ShapeDtypeStruct(xs.shape, F32),
        compiler_params=_params("arbitrary"),
        name="experts",
    )(plan, xs, w_gate, w_up, w_down)


def _final_kernel(pos_ref, posn_ref, x_ref, r_ref, g2_ref, fg_ref, y_hbm, o_ref, ybuf, sem):
    rows = FINAL_ROWS
    i = pl.program_id(0)
    n = pl.num_programs(0)
    slot = i % 2

    def row_copy(src_row, s, r):
        return pltpu.make_async_copy(y_hbm.at[pl.ds(src_row, 1), :], ybuf.at[s, pl.ds(r, 1), :], sem.at[s])

    def issue(idx_ref, s):
        for r in range(rows):
            row_copy(idx_ref[0, 0, r], s, r).start(priority=0)
            row_copy(idx_ref[0, 0, rows + r], s, rows + r).start(priority=1)

    @pl.when(i == 0)
    def _():
        issue(pos_ref, 0)

    @pl.when(i + 1 < n)
    def _():
        issue(posn_ref, 1 - slot)

    for _ in range(2 * rows):
        row_copy(0, slot, 0).wait()

    route = r_ref[...]
    w1 = route[:, 2:3]
    w2 = route[:, 3:4]
    moe = w1 * ybuf[slot, 0:rows, :] + w2 * ybuf[slot, rows:2 * rows, :]
    x = x_ref[...] + g2_ref[0] * moe
    o_ref[...] = x * lax.rsqrt(jnp.mean(x * x, axis=-1, keepdims=True) + EPS) * fg_ref[...]


def _final(x1, route, mod3, gate_blk, final_g, y, pos3):
    rows = FINAL_ROWS
    m = x1.shape[0]
    nt = m // rows
    return pl.pallas_call(
        _final_kernel,
        grid=(nt,),
        in_specs=[
            pl.BlockSpec((1, 1, 2 * rows), lambda i: (i, 0, 0), memory_space=pltpu.SMEM),
            pl.BlockSpec((1, 1, 2 * rows), lambda i: (jnp.minimum(i + 1, nt - 1), 0, 0), memory_space=pltpu.SMEM),
            pl.BlockSpec((rows, D_MODEL), lambda i: (i, 0)),
            pl.BlockSpec((rows, LANES), lambda i: (i, 0)),
            pl.BlockSpec((1, 1, D_MODEL), lambda i: (i * rows // SEQ, 0, gate_blk)),
            pl.BlockSpec((1, D_MODEL), lambda i: (0, 0)),
            pl.BlockSpec(memory_space=pl.ANY),
        ],
        out_specs=pl.BlockSpec((rows, D_MODEL), lambda i: (i, 0)),
        out_shape=jax.ShapeDtypeStruct((m, D_MODEL), F32),
        scratch_shapes=[pltpu.VMEM((2, 2 * rows, D_MODEL), F32), pltpu.SemaphoreType.DMA((2,))],
        compiler_params=_params("arbitrary"),
        name="final",
    )(pos3, pos3, x1, route, mod3, final_g, y)


def _rope_tables():
    rows = SEQ // GRID_W
    row = jnp.repeat(jnp.arange(rows, dtype=F32), GRID_W)
    col = jnp.tile(jnp.arange(GRID_W, dtype=F32), rows)
    n_freq = RET_DK // 4
    inv = ROPE_BASE ** (-jnp.arange(n_freq, dtype=F32) / n_freq)
    ang = jnp.concatenate([row[:, None] * inv, col[:, None] * inv], axis=-1)
    cos, sin = jnp.cos(ang), jnp.sin(ang)
    return jnp.concatenate([cos, cos], axis=-1), jnp.concatenate([-sin, sin], axis=-1)


def kernel(x, c, ctx, c_ctx, w_mod, b_mod, norm1_g, norm2_g, w_in, w_four_out, w_ret_out, w_out,
           ret_decay_f, ret_decay_b, w_group_router, b_group_router, w_expert_router, b_expert_router,
           w_gate, w_up, w_down, final_norm_g):
    b, n, d = x.shape
    assert (n, d) == (SEQ, D_MODEL) and ctx.shape[1] == CTX_LEN and w_mod.shape[0] == 1
    assert b + 1 <= 8
    t = b * n

    c8 = jnp.zeros((8, d), F32).at[:b].set(c).at[b].set(c_ctx)
    mod3 = _modulation(c8, w_mod[0], b_mod).reshape(8, 1, N_MOD * d)

    hx = _prenorm(x, norm1_g, mod3, 0, 1, rows=1024)

    cos2, sin2 = _rope_tables()
    pc = _in_proj(hx.reshape(t, d), w_in[0], cos2, sin2)
    kvc = _ctx_proj(ctx.reshape(b * CTX_LEN, d), norm1_g, mod3, b, w_in[0],
                    K_OFF // COL_TILE, (GF_OFF - K_OFF) // COL_TILE)
    kvc3 = kvc.reshape(b, CTX_LEN, GF_OFF - K_OFF)

    fm = _fourier(pc, b)

    lg_f = jax.nn.log_sigmoid(ret_decay_f[0].astype(F32))
    lg_b = jax.nn.log_sigmoid(ret_decay_b[0].astype(F32))
    lg = jnp.stack([lg_f, lg_b, jnp.exp(RET_BLOCK * lg_f), jnp.exp(RET_BLOCK * lg_b)])
    ret = _retention(pc, kvc3, lg)

    merged = _merge(fm.reshape(t, FOUR_WIDTH), ret.reshape(t, d), w_four_out[0], w_ret_out[0], pc)
    x1 = _out_proj(merged, w_out[0], x.reshape(t, d), mod3, 2 * d // COL_TILE)

    w_router = jnp.concatenate(
        [w_group_router[0], w_expert_router[0].transpose(1, 0, 2).reshape(d, N_EXPERTS)], axis=1).astype(F32)
    w_router = jnp.pad(w_router, ((0, 0), (0, LANES - w_router.shape[1])))
    w_hi = w_router.astype(BF16)
    w_lo = (w_router - w_hi.astype(F32)).astype(BF16)
    bias = jnp.pad(jnp.concatenate([b_group_router[0], b_expert_router[0].reshape(-1)]).astype(F32),
                   (0, LANES - N_GROUPS - N_EXPERTS)).reshape(1, LANES)
    route = _route(x1, norm2_g, mod3, 3, 4, w_hi, w_lo, bias)

    pos, plan = _dispatch_plan(route)

    def per_tile(rows):
        return pos.reshape(t // rows, rows, 2).transpose(0, 2, 1).reshape(t // rows, 1, 2 * rows)

    xs = _dispatch(x1, norm2_g, mod3, 3, 4, per_tile(DISPATCH_ROWS), 2 * t)
    y = _experts(xs, plan, w_gate[0], w_up[0], w_down[0])
    out = _final(x1, route, mod3, 5, final_norm_g.reshape(1, d), y, per_tile(FINAL_ROWS))
    return out.reshape(b, n, d)
```
